```python
import jax, jax.numpy as jnp
from jax import lax
import numpy as np

D_MODEL = 2048
BATCH = 2
SEQ = 4096
DEPTH = 4
DEC_BATCH = 8
DEC_SEQ = 8
PAST_LEN = 16384
PAGE_SIZE = 128

HD = 128
N_HEADS = D_MODEL // HD
MIX_W = N_HEADS * HD
H_A = N_HEADS // 2
KV_A = max(1, H_A // 4)
H_IDX = max(1, H_A // 2)
D_IDX = HD // 2
TOPK_MAX = 256
H_B = N_HEADS - H_A
KV_B = max(1, H_B // 4)
CMP_BLOCK = 32
SEL_BLOCK = 64
N_SEL = 16
WINDOW = 512
H_C = N_HEADS
KV_C = max(1, H_C // 4)
N_MEM = 256
MEM_HEADS = 4
MEM_W = MEM_HEADS * HD
D_FF = -((-8 * D_MODEL) // (3 * 256)) * 256
ROPE_THETA = 10000.0
Q_BLOCK = 128
LN_EPS = 1e-5
N_EVEN = (DEPTH + 1) // 2
N_ODD = DEPTH // 2
ALPHA = (2 * DEPTH) ** 0.25
BETA_INIT = (8 * DEPTH) ** -0.25
SCALE = HD ** -0.5
AB_SIZES = (H_A * HD, 2 * KV_A * HD, H_IDX * D_IDX, D_IDX, H_IDX,
            H_B * HD, 2 * KV_B * HD, 2 * KV_B * HD, 2 * KV_B * HD, 3 * H_B)
IN_AB = sum(AB_SIZES)
C_SIZES = (H_C * HD, 2 * KV_C * HD)
IN_C = sum(C_SIZES)
F32 = jnp.float32

kernel_name = 'hybrid_dsa_nsa_stickbreak_decoder_step'


def split_cols(h, sizes):
    offs = np.cumsum(np.array(sizes))[:-1].tolist()
    return jnp.split(h, offs, axis=-1)


def rope(x, pos):
    d = x.shape[-1]
    half = d // 2
    inv = ROPE_THETA ** (-(jnp.arange(half, dtype=F32) * 2.0 / d))
    ang = pos.astype(F32)[:, None] * inv[None, :]
    cos = jnp.cos(ang)[None, :, None, :]
    sin = jnp.sin(ang)[None, :, None, :]
    x32 = x.astype(F32)
    x1, x2 = x32[..., :half], x32[..., half:]
    return jnp.concatenate([x1 * cos - x2 * sin, x2 * cos + x1 * sin], axis=-1).astype(x.dtype)


def rot_k(kv, pos):
    return jnp.stack([rope(kv[:, :, 0], pos), kv[:, :, 1]], axis=2)


def layer_norm(x, g, b):
    x32 = x.astype(F32)
    mu = x32.mean(-1, keepdims=True)
    var = jnp.square(x32 - mu).mean(-1, keepdims=True)
    return ((x32 - mu) * lax.rsqrt(var + LN_EPS) * g.astype(F32) + b.astype(F32)).astype(x.dtype)


def masked_softmax(s, mask):
    s = jnp.where(mask, s, -jnp.inf)
    m = jnp.max(s, axis=-1, keepdims=True)
    m = jnp.where(jnp.isfinite(m), m, 0.0)
    p = jnp.exp(s - m)
    den = p.sum(-1, keepdims=True)
    return p / jnp.where(den > 0, den, 1.0)


def blockwise(fn, xs, pos):
    t = pos.shape[0]
    qb = min(Q_BLOCK, t)
    nb = t // qb
    def to_blocks(a):
        return jnp.moveaxis(a.reshape(a.shape[0], nb, qb, *a.shape[2:]), 1, 0)
    args = tuple(to_blocks(a) for a in xs) + (pos.reshape(nb, qb),)
    out = lax.map(lambda a: fn(*a), args)
    return jnp.moveaxis(out, 0, 1).reshape(out.shape[1], t, *out.shape[3:])


def gather_pages(pool, page_table):
    g = pool[page_table]
    return g.reshape(g.shape[0], g.shape[1] * g.shape[2], *g.shape[3:])


def dsa_block(q, qi, wi, pos, k, v, kidx, topk):
    b, qb = q.shape[:2]
    s_len = k.shape[1]
    causal = jnp.arange(s_len)[None, :] <= pos[:, None]
    dots = jnp.einsum('bqhe,bse->bqhs', qi.astype(F32), kidx.astype(F32)) * (D_IDX ** -0.5)
    score = jnp.einsum('bqh,bqhs->bqs', wi.astype(F32), jax.nn.relu(dots))
    score = jnp.where(causal[None], score, -jnp.inf)
    _, sel = lax.top_k(score, topk)
    take = jax.vmap(lambda kb, ib: kb[ib])
    kg = take(k, sel).astype(F32)
    vg = take(v, sel).astype(F32)
    qg = q.reshape(b, qb, KV_A, H_A // KV_A, HD).astype(F32)
    s = jnp.einsum('bqgrd,bqkgd->bqgrk', qg, kg) * SCALE
    valid = (sel <= pos[None, :, None])[:, :, None, None, :]
    p = masked_softmax(s, valid)
    o = jnp.einsum('bqgrk,bqkgd->bqgrd', p, vg)
    return o.reshape(b, qb, H_A, HD).astype(q.dtype)


def nsa_block(q, pos, k_c, v_c, k_s, v_s):
    b, qb = q.shape[:2]
    nc = k_c.shape[1]
    ns = k_s.shape[2]
    r = H_B // KV_B
    qg = q.reshape(b, qb, KV_B, r, HD).astype(F32)
    c_end = (jnp.arange(nc) + 1) * CMP_BLOCK - 1
    c_mask = (c_end[None, :] <= pos[:, None])[None, :, None, None, :]
    sc = jnp.einsum('bqgrd,bngd->bqgrn', qg, k_c.astype(F32)) * SCALE
    pc = masked_softmax(sc, c_mask)
    o_c = jnp.einsum('bqgrn,bngd->bqgrd', pc, v_c.astype(F32))
    ratio = SEL_BLOCK // CMP_BLOCK
    imp = pc.sum(3)
    imp = jnp.pad(imp, ((0, 0), (0, 0), (0, 0), (0, ns * ratio - nc))).reshape(b, qb, KV_B, ns, ratio).sum(-1)
    blk = jnp.arange(ns)
    cur = pos // SEL_BLOCK
    forced = (blk[None, :] == 0) | (blk[None, :] == cur[:, None]) | (blk[None, :] == cur[:, None] - 1)
    b_causal = blk[None, :] * SEL_BLOCK <= pos[:, None]
    imp = jnp.where(forced[None, :, None, :], jnp.inf, jnp.where(b_causal[None, :, None, :], imp, -jnp.inf))
    nsel = min(N_SEL, ns)
    _, sel = lax.top_k(imp, nsel)
    sel_t = jnp.transpose(sel, (0, 2, 1, 3))
    take = jax.vmap(jax.vmap(lambda kb, ib: kb[ib]))
    kg = take(k_s, sel_t).astype(F32)
    vg = take(v_s, sel_t).astype(F32)
    ss = jnp.einsum('bqgrd,bgqnld->bqgrnl', qg, kg).reshape(b, qb, KV_B, r, nsel * SEL_BLOCK) * SCALE
    tok = sel_t[..., None] * SEL_BLOCK + jnp.arange(SEL_BLOCK)
    s_mask = jnp.transpose(tok <= pos[None, None, :, None, None], (0, 2, 1, 3, 4)).reshape(b, qb, KV_B, 1, nsel * SEL_BLOCK)
    ps = masked_softmax(ss, s_mask)
    o_s = jnp.einsum('bqgrm,bgqmd->bqgrd', ps, vg.reshape(b, KV_B, qb, nsel * SEL_BLOCK, HD))
    return jnp.stack([o_c.reshape(b, qb, H_B, HD), o_s.reshape(b, qb, H_B, HD)], axis=2).astype(q.dtype)


def window_attend(q, pos, kv, k_start):
    b = q.shape[0]
    qb = min(Q_BLOCK, pos.shape[0])
    span = WINDOW + qb
    r = H_B // KV_B
    kvp = jnp.pad(kv, ((0, 0), (WINDOW, 0), (0, 0), (0, 0), (0, 0)))
    def blk(qq, pb):
        q0 = pb[0]
        kvb = lax.dynamic_slice_in_dim(kvp, q0 - k_start, span, axis=1)
        kpos = q0 - WINDOW + jnp.arange(span)
        mask = (kpos[None, :] <= pb[:, None]) & (kpos[None, :] > pb[:, None] - WINDOW) & (kpos[None, :] >= k_start)
        qg = qq.reshape(b, qb, KV_B, r, HD).astype(F32)
        s = jnp.einsum('bqgrd,bsgd->bqgrs', qg, kvb[:, :, 0].astype(F32)) * SCALE
        p = masked_softmax(s, mask[None, :, None, None, :])
        o = jnp.einsum('bqgrs,bsgd->bqgrd', p, kvb[:, :, 1].astype(F32))
        return o.reshape(b, qb, H_B, HD).astype(qq.dtype)
    return blockwise(blk, [q], pos)


def stick_break_block(q, pos, k, v):
    b, qb = q.shape[:2]
    s_len = k.shape[1]
    qg = q.reshape(b, qb, KV_C, H_C // KV_C, HD).astype(F32)
    z = jnp.einsum('bqgrd,bsgd->bqgrs', qg, k.astype(F32)) * SCALE
    before = (jnp.arange(s_len)[None, :] < pos[:, None])[None, :, None, None, :]
    log_keep = jnp.where(before, jax.nn.log_sigmoid(-z), 0.0)
    suffix = lax.cumsum(log_keep, axis=z.ndim - 1, reverse=True) - log_keep
    a = jnp.where(before, jnp.exp(jax.nn.log_sigmoid(z) + suffix), 0.0)
    o = jnp.einsum('bqgrs,bsgd->bqgrd', a, v.astype(F32))
    return o.reshape(b, qb, H_C, HD).astype(q.dtype)


def ab_mixer(x, pos, w_in, w_phi, w_o, past):
    b, t, _ = x.shape
    q_a, kv_a, q_i, k_i, w_i, q_b, cmp_kv, sel_kv, win_kv, g_b = split_cols(x @ w_in, AB_SIZES)
    q_a = rope(q_a.reshape(b, t, H_A, HD), pos)
    kv_a = rot_k(kv_a.reshape(b, t, 2, KV_A, HD), pos)
    q_i = rope(q_i.reshape(b, t, H_IDX, D_IDX), pos)
    k_i = rope(k_i.reshape(b, t, 1, D_IDX), pos)[:, :, 0]
    w_i = w_i * (H_IDX ** -0.5)
    q_b = rope(q_b.reshape(b, t, H_B, HD), pos)
    cmp_kv = cmp_kv.reshape(b, t, 2, KV_B, HD)
    sel_kv = rot_k(sel_kv.reshape(b, t, 2, KV_B, HD), pos)
    win_kv = rot_k(win_kv.reshape(b, t, 2, KV_B, HD), pos)
    if past is None:
        a_all, ki_all, cmp_all, sel_all, win_all = kv_a, k_i, cmp_kv, sel_kv, win_kv
        k_start = 0
        n_keep = min(WINDOW, t)
    else:
        p_a, p_ki, p_cmp, p_sel, p_win = past
        a_all = jnp.concatenate([p_a, kv_a], axis=1)
        ki_all = jnp.concatenate([p_ki, k_i], axis=1)
        cmp_all = jnp.concatenate([p_cmp, cmp_kv], axis=1)
        sel_all = jnp.concatenate([p_sel, sel_kv], axis=1)
        win_all = jnp.concatenate([p_win, win_kv], axis=1)
        k_start = p_a.shape[1] - p_win.shape[1]
        n_keep = p_win.shape[1]
    s_len = a_all.shape[1]
    topk = min(TOPK_MAX, s_len // 4)
    k_a, v_a = a_all[:, :, 0], a_all[:, :, 1]
    o_a = blockwise(lambda qq, qi, wi, pb: dsa_block(qq, qi, wi, pb, k_a, v_a, ki_all, topk), [q_a, q_i, w_i], pos)
    nc = s_len // CMP_BLOCK
    blocks = cmp_all[:, :nc * CMP_BLOCK].reshape(b, nc, CMP_BLOCK, 2, KV_B, HD)
    phi = w_phi.reshape(2, CMP_BLOCK, HD, HD)
    k_c = jnp.einsum('bnlgd,lde->bnge', blocks[:, :, :, 0], phi[0])
    v_c = jnp.einsum('bnlgd,lde->bnge', blocks[:, :, :, 1], phi[1])
    k_c = rope(k_c, (jnp.arange(nc, dtype=jnp.int32) + 1) * CMP_BLOCK - 1)
    ns = -(-s_len // SEL_BLOCK)
    sel_p = jnp.pad(sel_all, ((0, 0), (0, ns * SEL_BLOCK - s_len), (0, 0), (0, 0), (0, 0))).reshape(b, ns, SEL_BLOCK, 2, KV_B, HD)
    k_s = jnp.moveaxis(sel_p[:, :, :, 0], 3, 1)
    v_s = jnp.moveaxis(sel_p[:, :, :, 1], 3, 1)
    o_cs = blockwise(lambda qq, pb: nsa_block(qq, pb, k_c, v_c, k_s, v_s), [q_b], pos)
    o_w = window_attend(q_b, pos, win_all, k_start)
    g = jax.nn.sigmoid(g_b.reshape(b, t, H_B, 3).astype(F32))
    o_b = (g[..., 0:1] * o_cs[:, :, 0].astype(F32) + g[..., 1:2] * o_cs[:, :, 1].astype(F32)
           + g[..., 2:3] * o_w.astype(F32)).astype(x.dtype)
    o = jnp.concatenate([o_a.reshape(b, t, H_A * HD), o_b.reshape(b, t, H_B * HD)], axis=-1)
    return o @ w_o, (kv_a, k_i, cmp_kv, sel_kv, win_all[:, win_all.shape[1] - n_keep:])


def sb_mixer(x, pos, w_in, w_o, past):
    b, t, _ = x.shape
    q, kv = split_cols(x @ w_in, C_SIZES)
    q = q.reshape(b, t, H_C, HD)
    kv = kv.reshape(b, t, 2, KV_C, HD)
    kv_all = kv if past is None else jnp.concatenate([past, kv], axis=1)
    k, v = kv_all[:, :, 0], kv_all[:, :, 1]
    o = blockwise(lambda qq, pb: stick_break_block(qq, pb, k, v), [q], pos)
    return o.reshape(b, t, MIX_W) @ w_o, kv


def mem_attend(x, mem_kv, w_q, w_o):
    b, t, _ = x.shape
    q = (x @ w_q).reshape(b, t, MEM_HEADS, HD).astype(F32)
    s = jnp.einsum('bthd,bmhd->bhtm', q, mem_kv[:, :, 0].astype(F32)) * SCALE
    p = jax.nn.softmax(s, axis=-1)
    o = jnp.einsum('bhtm,bmhd->bthd', p, mem_kv[:, :, 1].astype(F32)).reshape(b, t, MEM_W)
    return o.astype(x.dtype) @ w_o


def swiglu(x, w_gu, w_d):
    gate, up = jnp.split(x @ w_gu, 2, axis=-1)
    return (jax.nn.silu(gate) * up) @ w_d


def run_group(x, pos, pasts, mem_kvs, w_in_ab, w_cmp_phi, w_in_c, w_out, w_mem_q, w_mem_o, w_gate_up, w_down, ln_g, ln_b):
    even_states, odd_states = [], []
    for i in range(DEPTH):
        if i % 2 == 0:
            mix, st = ab_mixer(x, pos, w_in_ab[i // 2], w_cmp_phi[i // 2], w_out[i], pasts[i])
            even_states.append(st)
        else:
            mix, st = sb_mixer(x, pos, w_in_c[i // 2], w_out[i], pasts[i])
            odd_states.append(st)
        x = layer_norm(ALPHA * x + mix, ln_g[i, 0], ln_b[i, 0])
        x = layer_norm(ALPHA * x + mem_attend(x, mem_kvs[i], w_mem_q[i], w_mem_o[i]), ln_g[i, 1], ln_b[i, 1])
        x = layer_norm(ALPHA * x + swiglu(x, w_gate_up[i], w_down[i]), ln_g[i, 2], ln_b[i, 2])
    return x, even_states, odd_states


def setup_inputs(seed: int = 0) -> dict:
    key = jax.random.key(seed)
    ks = jax.random.split(key, 24)
    n_pages = PAST_LEN // PAGE_SIZE
    n_used = DEC_BATCH * n_pages
    n_pool = n_used + max(1, n_used // 4)
    win_buf = min(WINDOW, PAST_LEN)
    def nrm(k, shape, scale=1.0):
        return jax.random.normal(k, shape, F32) * scale
    page_table = jax.random.permutation(ks[0], n_pool)[:n_used].reshape(DEC_BATCH, n_pages).astype(jnp.int32)
    return {
        'x_prompt': nrm(ks[1], (BATCH, SEQ, D_MODEL)),
        'x_sample': nrm(ks[2], (DEC_BATCH, DEC_SEQ, D_MODEL)),
        'cache_a_kv': nrm(ks[3], (N_EVEN, n_pool, PAGE_SIZE, 2, KV_A, HD)),
        'cache_a_kidx': nrm(ks[4], (N_EVEN, n_pool, PAGE_SIZE, D_IDX)),
        'cache_b_cmp_kv': nrm(ks[5], (N_EVEN, n_pool, PAGE_SIZE, 2, KV_B, HD)),
        'cache_b_sel_kv': nrm(ks[6], (N_EVEN, n_pool, PAGE_SIZE, 2, KV_B, HD)),
        'state_b_win_kv': nrm(ks[7], (N_EVEN, DEC_BATCH, win_buf, 2, KV_B, HD)),
        'cache_c_kv': nrm(ks[8], (N_ODD, n_pool, PAGE_SIZE, 2, KV_C, HD)),
        'cache_mem_kv': nrm(ks[9], (DEPTH, DEC_BATCH, N_MEM, 2, MEM_HEADS, HD)),
        'page_table': page_table,
        'mem_prompt': nrm(ks[10], (BATCH, N_MEM, D_MODEL)),
        'w_in_ab': nrm(ks[11], (N_EVEN, D_MODEL, IN_AB), D_MODEL ** -0.5),
        'w_cmp_phi': nrm(ks[12], (N_EVEN, 2, CMP_BLOCK * HD, HD), (CMP_BLOCK * HD) ** -0.5),
        'w_in_c': nrm(ks[13], (N_ODD, D_MODEL, IN_C), D_MODEL ** -0.5),
        'w_out': nrm(ks[14], (DEPTH, MIX_W, D_MODEL), MIX_W ** -0.5 * BETA_INIT),
        'w_mem_q': nrm(ks[15], (DEPTH, D_MODEL, MEM_W), D_MODEL ** -0.5),
        'w_mem_kv': nrm(ks[16], (DEPTH, D_MODEL, 2 * MEM_W), D_MODEL ** -0.5),
        'w_mem_o': nrm(ks[17], (DEPTH, MEM_W, D_MODEL), MEM_W ** -0.5 * BETA_INIT),
        'w_gate_up': nrm(ks[18], (DEPTH, D_MODEL, 2 * D_FF), D_MODEL ** -0.5),
        'w_down': nrm(ks[19], (DEPTH, D_FF, D_MODEL), D_FF ** -0.5 * BETA_INIT),
        'ln_g': 1.0 + nrm(ks[20], (DEPTH, 3, D_MODEL), 0.02),
        'ln_b': nrm(ks[21], (DEPTH, 3, D_MODEL), 0.02),
    }


def reference(x_prompt, x_sample, cache_a_kv, cache_a_kidx, cache_b_cmp_kv, cache_b_sel_kv, state_b_win_kv,
              cache_c_kv, cache_mem_kv, page_table, mem_prompt, w_in_ab, w_cmp_phi, w_in_c, w_out,
              w_mem_q, w_mem_kv, w_mem_o, w_gate_up, w_down, ln_g, ln_b):
    b_p, t_p, _ = x_prompt.shape
    past_len = page_table.shape[1] * cache_a_kv.shape[2]
    pos_p = jnp.arange(t_p, dtype=jnp.int32)
    pos_s = past_len + jnp.arange(x_sample.shape[1], dtype=jnp.int32)
    mem_p = [(mem_prompt @ w_mem_kv[i]).reshape(b_p, mem_prompt.shape[1], 2, MEM_HEADS, HD) for i in range(DEPTH)]
    y_prompt, ev_p, od_p = run_group(x_prompt, pos_p, [None] * DEPTH, mem_p, w_in_ab, w_cmp_phi, w_in_c, w_out,
                                     w_mem_q, w_mem_o, w_gate_up, w_down, ln_g, ln_b)
    pasts = []
    for i in range(DEPTH):
        j = i // 2
        if i % 2 == 0:
            pasts.append((gather_pages(cache_a_kv[j], page_table), gather_pages(cache_a_kidx[j], page_table),
                          gather_pages(cache_b_cmp_kv[j], page_table), gather_pages(cache_b_sel_kv[j], page_table),
                          state_b_win_kv[j]))
        else:
            pasts.append(gather_pages(cache_c_kv[j], page_table))
    mem_s = [cache_mem_kv[i] for i in range(DEPTH)]
    y_sample, ev_s, od_s = run_group(x_sample, pos_s, pasts, mem_s, w_in_ab, w_cmp_phi, w_in_c, w_out,
                                     w_mem_q, w_mem_o, w_gate_up, w_down, ln_g, ln_b)
    new_a_kv_p = jnp.stack([s[0] for s in ev_p])
    new_a_kidx_p = jnp.stack([s[1] for s in ev_p])
    new_b_cmp_kv_p = jnp.stack([s[2] for s in ev_p])
    new_b_sel_kv_p = jnp.stack([s[3] for s in ev_p])
    new_b_win_kv_p = jnp.stack([s[4] for s in ev_p])
    new_c_kv_p = jnp.stack(od_p)
    new_mem_kv_p = jnp.stack(mem_p)
    new_a_kv_s = jnp.stack([s[0] for s in ev_s])
    new_a_kidx_s = jnp.stack([s[1] for s in ev_s])
    new_b_cmp_kv_s = jnp.stack([s[2] for s in ev_s])
    new_b_sel_kv_s = jnp.stack([s[3] for s in ev_s])
    new_b_win_kv_s = jnp.stack([s[4] for s in ev_s])
    new_c_kv_s = jnp.stack(od_s)
    return (y_prompt, y_sample, new_a_kv_p, new_a_kidx_p, new_b_cmp_kv_p, new_b_sel_kv_p, new_b_win_kv_p,
            new_c_kv_p, new_mem_kv_p, new_a_kv_s, new_a_kidx_s, new_b_cmp_kv_s, new_b_sel_kv_s,
            new_b_win_kv_s, new_c_kv_s)
```

```python
import functools
import math

import jax
import jax.numpy as jnp
import numpy as np
from jax import lax
from jax.experimental import pallas as pl
from jax.experimental.pallas import tpu as pltpu

F32 = jnp.float32
BF16 = jnp.bfloat16
I32 = jnp.int32

HD = 128
LANES = 128
H_A, KV_A, H_IDX, D_IDX = 8, 2, 4, 64
H_B, KV_B = 8, 2
H_C, KV_C = 16, 4
MEM_HEADS = 4
TOPK_MAX = 256
CMP_BLOCK, SEL_BLOCK, N_SEL, WINDOW = 32, 64, 16, 512
ROPE_THETA = 10000.0
LN_EPS = 1e-5
DEPTH = 4
ALPHA = (2 * DEPTH) ** 0.25
SCALE = HD ** -0.5
NEG = -1e30
INT_MIN = -2 ** 31
VMEM_LIMIT = 56 * 1024 * 1024

C_QA, C_KVA, C_QI, C_QB, C_CMP, C_SEL, C_WIN, C_MISC, C_END = (
    0, 1024, 1536, 1792, 2816, 3328, 3840, 4352, 4480)
MISC_WI = D_IDX
MISC_G = D_IDX + H_IDX


def _params(sem):
    return pltpu.CompilerParams(dimension_semantics=sem, vmem_limit_bytes=VMEM_LIMIT)


def _tile(n, pref, align):
    t = (min(pref, n) // align) * align
    while t >= align:
        if n % t == 0:
            return t
        t -= align
    return n


def _ln(y, g, b):
    mu = jnp.mean(y, axis=-1, keepdims=True)
    d = y - mu
    var = jnp.mean(d * d, axis=-1, keepdims=True)
    return d * lax.rsqrt(var + LN_EPS) * g + b


MXU = BF16


def _dot(a, b):
    return jnp.dot(a.astype(MXU), b.astype(MXU), preferred_element_type=F32)


def _dot_nt(a, b):
    return lax.dot_general(a.astype(MXU), b.astype(MXU), (((1,), (1,)), ((), ())),
                           preferred_element_type=F32)


def _mm_body(x_ref, w_ref, o_ref, acc_ref, *, nk):
    part = _dot(x_ref[...], w_ref[...])
    if nk == 1:
        o_ref[...] = part
        return
    k = pl.program_id(2)

    @pl.when(k == 0)
    def _():
        acc_ref[...] = part

    @pl.when(k > 0)
    def _():
        acc_ref[...] += part

    @pl.when(k == nk - 1)
    def _():
        o_ref[...] = acc_ref[...]


def _mm(x, w, *, tm=1024, tn=512, tk=2048, rows=None):
    row0, m = rows if rows is not None else (0, x.shape[0])
    kd = x.shape[1]
    n = w.shape[1]
    tm = _tile(math.gcd(m, row0), tm, 8)
    tn = _tile(n, tn, LANES)
    tk = _tile(kd, tk, LANES)
    nk = kd // tk
    blk0 = row0 // tm
    return pl.pallas_call(
        functools.partial(_mm_body, nk=nk),
        grid=(m // tm, n // tn, nk),
        in_specs=[pl.BlockSpec((tm, tk), lambda i, j, k: (blk0 + i, k)),
                  pl.BlockSpec((tk, tn), lambda i, j, k: (k, j))],
        out_specs=pl.BlockSpec((tm, tn), lambda i, j, k: (i, j)),
        out_shape=jax.ShapeDtypeStruct((m, n), F32),
        scratch_shapes=[pltpu.VMEM((tm, tn), F32)],
        compiler_params=_params(("parallel", "parallel", "arbitrary")),
        name="mm",
    )(x, w)


def _mm_ln_body(x_ref, w_ref, r_ref, g_ref, b_ref, o_ref, acc_ref, *, nk):
    k = pl.program_id(1)
    part = _dot(x_ref[...], w_ref[...])

    @pl.when(k == 0)
    def _():
        acc_ref[...] = part

    @pl.when(k > 0)
    def _():
        acc_ref[...] += part

    @pl.when(k == nk - 1)
    def _():
        o_ref[...] = _ln(ALPHA * r_ref[...] + acc_ref[...], g_ref[...], b_ref[...])


def _mm_ln(x, w, res, g, b, *, tm=512, tk=512):
    m, kd = x.shape
    d = w.shape[1]
    tm = _tile(m, tm, 8)
    tk = _tile(kd, tk, LANES)
    nk = kd // tk
    return pl.pallas_call(
        functools.partial(_mm_ln_body, nk=nk),
        grid=(m // tm, nk),
        in_specs=[pl.BlockSpec((tm, tk), lambda i, k: (i, k)),
                  pl.BlockSpec((tk, d), lambda i, k: (k, 0)),
                  pl.BlockSpec((tm, d), lambda i, k: (i, 0)),
                  pl.BlockSpec((1, d), lambda i, k: (0, 0)),
                  pl.BlockSpec((1, d), lambda i, k: (0, 0))],
        out_specs=pl.BlockSpec((tm, d), lambda i, k: (i, 0)),
        out_shape=jax.ShapeDtypeStruct((m, d), F32),
        scratch_shapes=[pltpu.VMEM((tm, d), F32)],
        compiler_params=_params(("parallel", "arbitrary")),
        name="mm_ln",
    )(x, w, res, g.reshape(1, d), b.reshape(1, d))


def _post_ab_body(h_ref, c128_ref, s128_ref, c64_ref, s64_ref,
                  qa_ref, kva_ref, qi_ref, qb_ref, cmp_ref, sel_ref, win_ref, misc_ref):
    cos = c128_ref[...]
    sin = s128_ref[...]
    ci = c64_ref[...]
    si = s64_ref[...]
    lane = lax.broadcasted_iota(I32, cos.shape, 1)
    first_half = (lane & (D_IDX - 1)) < (D_IDX // 2)

    def rope128(x):
        return x * cos + pltpu.roll(x, HD // 2, 1) * sin

    def rope64(x):
        partner = jnp.where(first_half, pltpu.roll(x, LANES - D_IDX // 2, 1),
                            pltpu.roll(x, D_IDX // 2, 1))
        return x * ci + partner * si

    def head(c0, i):
        return h_ref[:, c0 + i * HD:c0 + (i + 1) * HD]

    for i in range(H_A):
        qa_ref[:, i * HD:(i + 1) * HD] = rope128(head(C_QA, i))
    for i in range(H_B):
        qb_ref[:, i * HD:(i + 1) * HD] = rope128(head(C_QB, i))
    for src, dst, kv in ((C_KVA, kva_ref, KV_A), (C_SEL, sel_ref, KV_B), (C_WIN, win_ref, KV_B)):
        for i in range(kv):
            dst[:, i * HD:(i + 1) * HD] = rope128(head(src, i))
            dst[:, (kv + i) * HD:(kv + i + 1) * HD] = head(src, kv + i)
    cmp_ref[...] = h_ref[:, C_CMP:C_SEL]
    for i in range(H_IDX * D_IDX // LANES):
        qi_ref[:, i * LANES:(i + 1) * LANES] = rope64(head(C_QI, i))
    raw = h_ref[:, C_MISC:C_END]
    misc_ref[...] = jnp.where(
        lane < MISC_WI, rope64(raw),
        jnp.where(lane < MISC_G, raw * (H_IDX ** -0.5),
                  jnp.where(lane < MISC_G + 3 * H_B, 1.0 / (1.0 + jnp.exp(-raw)), 0.0)))


def _post_ab(h, tabs, t):
    m = h.shape[0]
    tm = _tile(t, 256, 8)
    nt = t // tm
    widths = (H_A * HD, 2 * KV_A * HD, H_IDX * D_IDX, H_B * HD, 2 * KV_B * HD,
              2 * KV_B * HD, 2 * KV_B * HD, LANES)
    tab_spec = pl.BlockSpec((tm, LANES), lambda i: (i % nt, 0))
    return pl.pallas_call(
        _post_ab_body,
        grid=(m // tm,),
        in_specs=[pl.BlockSpec((tm, C_END), lambda i: (i, 0))] + [tab_spec] * 4,
        out_specs=[pl.BlockSpec((tm, w), lambda i: (i, 0)) for w in widths],
        out_shape=[jax.ShapeDtypeStruct((m, w), F32) for w in widths],
        compiler_params=_params(("parallel",)),
        name="post_ab",
    )(h, *tabs)


def _rope_tables(pos, d):
    half = d // 2
    inv = ROPE_THETA ** (-(jnp.arange(half, dtype=F32) * 2.0 / d))
    ang = pos.astype(F32)[:, None] * inv[None, :]
    cos, sin = jnp.cos(ang), jnp.sin(ang)
    reps = LANES // d
    return (jnp.tile(jnp.concatenate([cos, cos], axis=1), (1, reps)),
            jnp.tile(jnp.concatenate([-sin, sin], axis=1), (1, reps)))


def _sortable(x):
    bits = lax.bitcast_convert_type(jnp.where(x == 0.0, 0.0, x), I32)
    return bits ^ ((bits >> 31) & 0x7FFFFFFF)


def _dsa_index_body(*refs, tq, tk, nk, n_pages, topk, pos0, paged):
    if paged:
        _, qi_ref, wi_ref, kx_ref, tail_ref, mask_ref, key_ref = refs
    else:
        qi_ref, wi_ref, kx_ref, mask_ref, key_ref = refs
    qb = pl.program_id(1)
    kb = pl.program_id(2)
    sub = tk // LANES

    if paged:
        kx = jnp.where(kb < n_pages, kx_ref[0], tail_ref[0])
    else:
        kx = kx_ref[0]
    kx = kx[:, :D_IDX]
    qi = qi_ref[0]
    wi = wi_ref[0]
    score = jnp.zeros((tq, tk), F32)
    for i in range(H_IDX):
        dots = _dot_nt(qi[:, i * D_IDX:(i + 1) * D_IDX], kx) * (D_IDX ** -0.5)
        score = score + wi[:, MISC_WI + i:MISC_WI + i + 1] * jnp.maximum(dots, 0.0)
    qpos = pos0 + qb * tq + lax.broadcasted_iota(I32, (tq, tk), 0)
    kpos = kb * tk + lax.broadcasted_iota(I32, (tq, tk), 1)
    keys = jnp.where(kpos <= qpos, _sortable(score), INT_MIN)
    for u in range(sub):
        key_ref[kb * sub + u] = keys[:, u * LANES:(u + 1) * LANES]

    @pl.when(kb == nk - 1)
    def _():
        nch = nk * sub

        def count_ge(c):
            hit = (key_ref[...] >= c).astype(I32)
            return jnp.sum(jnp.sum(hit, axis=0, keepdims=True), axis=2, keepdims=True)

        ans0 = jnp.where(count_ge(jnp.zeros((1, tq, 1), I32)) >= topk, 0, INT_MIN)

        def bit_step(i, ans):
            cand = ans | jnp.left_shift(jnp.int32(1), 30 - i)
            return jnp.where(count_ge(cand) >= topk, cand, ans)

        kth = lax.fori_loop(0, 31, bit_step, ans0)
        above = (key_ref[...] > kth).astype(I32)
        n_above = jnp.sum(jnp.sum(above, axis=0, keepdims=True), axis=2, keepdims=True)
        need = (topk - n_above)[0].astype(F32)
        kth2 = kth[0]
        r_i = lax.broadcasted_iota(I32, (LANES, LANES), 0)
        c_i = lax.broadcasted_iota(I32, (LANES, LANES), 1)
        strict = jnp.where(r_i < c_i, 1.0, 0.0).astype(BF16)
        seen = jnp.zeros((tq, 1), F32)
        for c in range(nch):
            kc = key_ref[c]
            tie = jnp.where(kc == kth2, 1.0, 0.0)
            rank = seen + jnp.dot(tie.astype(BF16), strict, preferred_element_type=F32)
            take = jnp.where(kc > kth2, 1.0, jnp.where(rank < need, tie, 0.0))
            mask_ref[0, 0, :, c * LANES:(c + 1) * LANES] = jnp.where(kc > INT_MIN, take, 0.0)
            seen = seen + jnp.sum(tie, axis=1, keepdims=True)


def _dsa_index(qi, misc, kx, *, topk, pos0, tq, tk, page=None):
    b, t, _ = qi.shape
    paged = page is not None
    if paged:
        pt, off, tail = page
        n_pages = pt.shape[1]
        nk = n_pages + 1
        kx_spec = pl.BlockSpec((1, tk, kx.shape[2]),
                               lambda bi, qb, kb, p: (off + p[bi, jnp.minimum(kb, n_pages - 1)], 0, 0))
        tail_spec = pl.BlockSpec((1, tk, tail.shape[2]), lambda bi, qb, kb, p: (bi, 0, 0))
        ix = lambda f: (lambda bi, qb, kb, p: f(bi, qb, kb))
        extra, args = [tail_spec], (pt, qi, misc, kx, tail)
    else:
        n_pages = 0
        nk = kx.shape[1] // tk
        kx_spec = pl.BlockSpec((1, tk, kx.shape[2]), lambda bi, qb, kb: (bi, kb, 0))
        ix = lambda f: f
        extra, args = [], (qi, misc, kx)
    s_pad = nk * tk
    in_specs = [pl.BlockSpec((1, tq, qi.shape[2]), ix(lambda bi, qb, kb: (bi, qb, 0))),
                pl.BlockSpec((1, tq, LANES), ix(lambda bi, qb, kb: (bi, qb, 0))),
                kx_spec] + extra
    out_spec = pl.BlockSpec((1, 1, tq, s_pad), ix(lambda bi, qb, kb: (bi, 0, qb, 0)))
    body = functools.partial(_dsa_index_body, tq=tq, tk=tk, nk=nk, n_pages=n_pages,
                             topk=topk, pos0=pos0, paged=paged)
    scratch = [pltpu.VMEM((s_pad // LANES, tq, LANES), I32)]
    grid = (b, t // tq, nk)
    out_shape = jax.ShapeDtypeStruct((b, 1, t, s_pad), F32)
    sem = _params(("parallel", "parallel", "arbitrary"))
    if paged:
        gs = pltpu.PrefetchScalarGridSpec(num_scalar_prefetch=1, grid=grid, in_specs=in_specs,
                                          out_specs=out_spec, scratch_shapes=scratch)
        return pl.pallas_call(body, grid_spec=gs, out_shape=out_shape, compiler_params=sem,
                              name="dsa_index_paged")(*args)
    return pl.pallas_call(body, grid=grid, in_specs=in_specs, out_specs=out_spec,
                          out_shape=out_shape, scratch_shapes=scratch, compiler_params=sem,
                          name="dsa_index")(*args)


def _attn_body(*refs, groups, rep, tq, tk, nj, n_pages, pos0, k_start, band, lead, paged, gm):
    refs = list(refs)
    if paged:
        refs.pop(0)
    q_ref, kv_ref = refs[0], refs[1]
    i = 2
    tail_ref = mask_ref = None
    if paged:
        tail_ref = refs[i]
        i += 1
    if not band:
        mask_ref = refs[i]
        i += 1
    o_ref, q_scr, m_scr, l_scr, acc_scr = refs[i:i + 5]
    qb = pl.program_id(1)
    j = pl.program_id(2)
    kb = qb - lead + j if band else j

    @pl.when(j == 0)
    def _():
        m_scr[...] = jnp.full(m_scr.shape, NEG, F32)
        l_scr[...] = jnp.zeros(l_scr.shape, F32)
        acc_scr[...] = jnp.zeros(acc_scr.shape, F32)
        for g in range(groups):
            for r in range(rep):
                h = g * rep + r
                q_scr[g, r * tq:(r + 1) * tq, :] = q_ref[0, :, h * HD:(h + 1) * HD].astype(MXU)

    last_q = pos0 + qb * tq + tq - 1
    needed = (kb >= 0) & (k_start + kb * tk <= last_q)

    @pl.when(needed)
    def _():
        if paged:
            kv = jnp.where(j < n_pages, kv_ref[0], tail_ref[0])
        else:
            kv = kv_ref[0]
        if band:
            qpos = pos0 + qb * tq + lax.broadcasted_iota(I32, (tq, tk), 0)
            kpos = k_start + kb * tk + lax.broadcasted_iota(I32, (tq, tk), 1)
            ok = jnp.where(kpos <= qpos, jnp.where(kpos > qpos - WINDOW, 1.0, 0.0), 0.0)
        for g in range(groups):
            if not band:
                ok = mask_ref[0, g if gm > 1 else 0]
            okr = jnp.concatenate([ok] * rep, axis=0) > 0.5
            k_g = kv[:, g * HD:(g + 1) * HD]
            v_g = kv[:, (groups + g) * HD:(groups + g + 1) * HD]
            s = _dot_nt(q_scr[g], k_g) * SCALE
            s = jnp.where(okr, s, NEG)
            m_prev = m_scr[g][:, :1]
            l_prev = l_scr[g][:, :1]
            m_new = jnp.maximum(m_prev, jnp.max(s, axis=1, keepdims=True))
            p = jnp.where(okr, jnp.exp(s - m_new), 0.0)
            a = jnp.exp(m_prev - m_new)
            l_new = a * l_prev + jnp.sum(p, axis=1, keepdims=True)
            acc_scr[g] = a * acc_scr[g] + _dot(p, v_g)
            m_scr[g] = jnp.broadcast_to(m_new, m_scr.shape[1:])
            l_scr[g] = jnp.broadcast_to(l_new, l_scr.shape[1:])

    @pl.when(j == nj - 1)
    def _():
        for g in range(groups):
            for r in range(rep):
                h = g * rep + r
                l = l_scr[g][r * tq:(r + 1) * tq, :1]
                o_ref[0, :, h * HD:(h + 1) * HD] = (
                    acc_scr[g][r * tq:(r + 1) * tq, :] / jnp.where(l > 0.0, l, 1.0))


def _attn(q, kv, *, groups, rep, tq, tk, pos0, mask=None, k_start=0, page=None):
    b, t, qw = q.shape
    band = mask is None
    paged = page is not None
    cw = kv.shape[2]
    nq = t // tq
    lead = WINDOW // tk if (band and nq > 1) else 0
    if paged:
        pt, off, tail = page
        n_pages = pt.shape[1]
        nj = n_pages + 1
    else:
        n_pages = 0
        nj = (lead + 1) if (band and nq > 1) else kv.shape[1] // tk

    def kblock(qb, j):
        if band:
            return jnp.maximum(qb - lead + j, 0)
        return jnp.minimum(j, (pos0 + (qb + 1) * tq - 1) // tk)

    if paged:
        ix = lambda f: (lambda bi, qb, j, p: f(bi, qb, j))
        kv_spec = pl.BlockSpec((1, tk, cw),
                               lambda bi, qb, j, p: (off + p[bi, jnp.minimum(j, n_pages - 1)], 0, 0))
        extra = [pl.BlockSpec((1, tk, cw), ix(lambda bi, qb, j: (bi, 0, 0)))]
        args = [pt, q, kv, tail]
    else:
        ix = lambda f: f
        kv_spec = pl.BlockSpec((1, tk, cw), lambda bi, qb, j: (bi, kblock(qb, j), 0))
        extra = []
        args = [q, kv]
    gm = 1
    if not band:
        gm = mask.shape[1]
        extra.append(pl.BlockSpec((1, gm, tq, tk),
                                  ix(lambda bi, qb, j: (bi, 0, qb, j if paged else kblock(qb, j)))))
        args.append(mask)
    in_specs = [pl.BlockSpec((1, tq, qw), ix(lambda bi, qb, j: (bi, qb, 0))), kv_spec] + extra
    out_spec = pl.BlockSpec((1, tq, qw), ix(lambda bi, qb, j: (bi, qb, 0)))
    rows = rep * tq
    scratch = [pltpu.VMEM((groups, rows, HD), MXU),
               pltpu.VMEM((groups, rows, LANES), F32),
               pltpu.VMEM((groups, rows, LANES), F32),
               pltpu.VMEM((groups, rows, HD), F32)]
    body = functools.partial(_attn_body, groups=groups, rep=rep, tq=tq, tk=tk, nj=nj,
                             n_pages=n_pages, pos0=pos0, k_start=k_start, band=band,
                             lead=lead, paged=paged, gm=gm)
    grid = (b, nq, nj)
    out_shape = jax.ShapeDtypeStruct((b, t, qw), F32)
    sem = _params(("parallel", "parallel", "arbitrary"))
    if paged:
        gs = pltpu.PrefetchScalarGridSpec(num_scalar_prefetch=1, grid=grid, in_specs=in_specs,
                                          out_specs=out_spec, scratch_shapes=scratch)
        return pl.pallas_call(body, grid_spec=gs, out_shape=out_shape, compiler_params=sem,
                              name="attn_paged")(*args)
    return pl.pallas_call(body, grid=grid, in_specs=in_specs, out_specs=out_spec,
                          out_shape=out_shape, scratch_shapes=scratch, compiler_params=sem,
                          name="attn_band" if band else "attn_mask")(*args)


def _nsa_cmp_body(q_ref, kc_ref, cc_ref, cs_ref, oc_ref, mask_ref, *, tq, nc, ns, nsp, s_pad,
                  chunk, pos0):
    qb = pl.program_id(1)
    half = nc // 2
    rep = H_B // KV_B
    kc = kc_ref[0]
    cos = cc_ref[...]
    sin = cs_ref[...]
    qpos1 = pos0 + qb * tq + lax.broadcasted_iota(I32, (tq, 1), 0)
    n_perm = lax.broadcasted_iota(I32, (tq, nc), 1)
    n_orig = jnp.where(n_perm < half, 2 * n_perm, 2 * (n_perm - half) + 1)
    c_ok = jnp.where(((n_orig + 1) * CMP_BLOCK - 1) <= qpos1, 1.0, 0.0)
    c_okr = jnp.concatenate([c_ok] * rep, axis=0) > 0.5
    blk = lax.broadcasted_iota(I32, (tq, nsp), 1)
    sel_shift = SEL_BLOCK.bit_length() - 1
    cur = qpos1 >> sel_shift
    forced = (blk == 0) | (blk == cur) | (blk == cur - 1)
    b_causal = blk * SEL_BLOCK <= qpos1
    for g in range(KV_B):
        k_raw = kc[:, g * HD:(g + 1) * HD]
        k_g = k_raw * cos + pltpu.roll(k_raw, HD // 2, 1) * sin
        v_g = kc[:, (KV_B + g) * HD:(KV_B + g + 1) * HD]
        qg = jnp.concatenate(
            [q_ref[0, :, (g * rep + r) * HD:(g * rep + r + 1) * HD] for r in range(rep)], axis=0)
        s = _dot_nt(qg, k_g) * SCALE
        s = jnp.where(c_okr, s, NEG)
        m = jnp.max(s, axis=1, keepdims=True)
        p = jnp.where(c_okr, jnp.exp(s - m), 0.0)
        den = jnp.sum(p, axis=1, keepdims=True)
        p = p / jnp.where(den > 0.0, den, 1.0)
        o = _dot(p, v_g)
        imp_c = p[0:tq]
        for r in range(rep):
            oc_ref[0, :, (g * rep + r) * HD:(g * rep + r + 1) * HD] = o[r * tq:(r + 1) * tq]
            if r > 0:
                imp_c = imp_c + p[r * tq:(r + 1) * tq]
        imp = imp_c[:, :half] + imp_c[:, half:]
        if nsp > half:
            imp = jnp.concatenate([imp, jnp.zeros((tq, nsp - half), F32)], axis=1)
        val = jnp.where(forced, jnp.inf, jnp.where(b_causal, imp, -jnp.inf))
        rank = jnp.zeros((tq, nsp), F32)
        for i in range(ns):
            col = val[:, i:i + 1]
            low = jnp.where(blk > i, 1.0, 0.0)
            rank = rank + jnp.where(col > val, 1.0, jnp.where(col == val, low, 0.0))
        chosen = jnp.where(rank < min(N_SEL, ns), 1.0, 0.0).astype(BF16)
        row_blk = lax.broadcasted_iota(I32, (nsp, chunk), 0)
        for c in range(s_pad // chunk):
            tok_b = c * chunk + lax.broadcasted_iota(I32, (nsp, chunk), 1)
            expand = jnp.where((tok_b >> sel_shift) == row_blk, 1.0, 0.0).astype(BF16)
            hit = jnp.dot(chosen, expand, preferred_element_type=F32)
            tok = c * chunk + lax.broadcasted_iota(I32, (tq, chunk), 1)
            mask_ref[0, g, :, c * chunk:(c + 1) * chunk] = jnp.where(tok <= qpos1, hit, 0.0)


def _nsa_cmp(qb, kc, ctab, *, s_len, s_pad, pos0, tq):
    b, t, qw = qb.shape
    nc = kc.shape[1]
    ns = -(-s_len // SEL_BLOCK)
    nsp = -(-ns // LANES) * LANES
    chunk = next(c for c in (512, 384, 256, 128) if s_pad % c == 0)
    body = functools.partial(_nsa_cmp_body, tq=tq, nc=nc, ns=ns, nsp=nsp, s_pad=s_pad,
                             chunk=chunk, pos0=pos0)
    return pl.pallas_call(
        body,
        grid=(b, t // tq),
        in_specs=[pl.BlockSpec((1, tq, qw), lambda bi, qi: (bi, qi, 0)),
                  pl.BlockSpec((1, nc, kc.shape[2]), lambda bi, qi: (bi, 0, 0)),
                  pl.BlockSpec((nc, HD), lambda bi, qi: (0, 0)),
                  pl.BlockSpec((nc, HD), lambda bi, qi: (0, 0))],
        out_specs=[pl.BlockSpec((1, tq, qw), lambda bi, qi: (bi, qi, 0)),
                   pl.BlockSpec((1, KV_B, tq, s_pad), lambda bi, qi: (bi, 0, qi, 0))],
        out_shape=[jax.ShapeDtypeStruct((b, t, qw), F32),
                   jax.ShapeDtypeStruct((b, KV_B, t, s_pad), F32)],
        compiler_params=_params(("parallel", "parallel")),
        name="nsa_cmp",
    )(qb, kc, *ctab)


def _gather_pages_body(_, *refs, per_step, rows):
    o_ref = refs[per_step]
    for u in range(per_step):
        o_ref[0, u * rows:(u + 1) * rows, :] = refs[u][0]


def _gather_pages(pool, pt, off, per_step=16):
    b, n_pages = pt.shape
    _, rows, c = pool.shape
    per_step = _tile(n_pages, per_step, 1)

    def spec(u):
        return pl.BlockSpec((1, rows, c), lambda bi, i, p: (off + p[bi, i * per_step + u], 0, 0))

    gs = pltpu.PrefetchScalarGridSpec(
        num_scalar_prefetch=1, grid=(b, n_pages // per_step),
        in_specs=[spec(u) for u in range(per_step)],
        out_specs=pl.BlockSpec((1, per_step * rows, c), lambda bi, i, p: (bi, i, 0)))
    return pl.pallas_call(
        functools.partial(_gather_pages_body, per_step=per_step, rows=rows),
        grid_spec=gs, out_shape=jax.ShapeDtypeStruct((b, n_pages * rows, c), F32),
        compiler_params=_params(("parallel", "arbitrary")), name="gather_pages",
    )(pt, *([pool] * per_step))


def _combine_body(oa_ref, oc_ref, os_ref, ow_ref, misc_ref, o_ref):
    o_ref[:, :H_A * HD] = oa_ref[...]
    gate = misc_ref[...]
    for h in range(H_B):
        c0 = MISC_G + 3 * h
        sl = slice(h * HD, (h + 1) * HD)
        o_ref[:, H_A * HD + h * HD:H_A * HD + (h + 1) * HD] = (
            gate[:, c0:c0 + 1] * oc_ref[:, sl] + gate[:, c0 + 1:c0 + 2] * os_ref[:, sl]
            + gate[:, c0 + 2:c0 + 3] * ow_ref[:, sl])


def _combine(o_a, o_c, o_s, o_w, misc):
    m = o_a.shape[0]
    tm = _tile(m, 512, 8)
    wa, wb = H_A * HD, H_B * HD
    return pl.pallas_call(
        _combine_body,
        grid=(m // tm,),
        in_specs=[pl.BlockSpec((tm, wa), lambda i: (i, 0))]
        + [pl.BlockSpec((tm, wb), lambda i: (i, 0))] * 3
        + [pl.BlockSpec((tm, LANES), lambda i: (i, 0))],
        out_specs=pl.BlockSpec((tm, wa + wb), lambda i: (i, 0)),
        out_shape=jax.ShapeDtypeStruct((m, wa + wb), F32),
        compiler_params=_params(("parallel",)),
        name="combine",
    )(o_a, o_c, o_s, o_w, misc)


def _sb_body(*refs, tq, tk, nj, n_pages, pos0, paged):
    refs = list(refs)
    if paged:
        refs.pop(0)
        q_ref, kv_ref, tail_ref, o_ref, q_scr, c_scr, acc_scr = refs
    else:
        q_ref, kv_ref, o_ref, q_scr, c_scr, acc_scr = refs
        tail_ref = None
    rep = H_C // KV_C
    qb = pl.program_id(1)
    j = pl.program_id(2)
    if paged:
        kb = nj - 1 - j
    else:
        kb = (pos0 + (qb + 1) * tq - 1) // tk - j

    @pl.when(j == 0)
    def _():
        c_scr[...] = jnp.zeros(c_scr.shape, F32)
        acc_scr[...] = jnp.zeros(acc_scr.shape, F32)
        for g in range(KV_C):
            for r in range(rep):
                h = g * rep + r
                q_scr[g, r * tq:(r + 1) * tq, :] = q_ref[0, :, h * HD:(h + 1) * HD].astype(MXU)

    @pl.when(kb >= 0)
    def _():
        if paged:
            kv = jnp.where(kb < n_pages, kv_ref[0], tail_ref[0])
        else:
            kv = kv_ref[0]
        qpos = pos0 + qb * tq + lax.broadcasted_iota(I32, (tq, tk), 0)
        kpos = kb * tk + lax.broadcasted_iota(I32, (tq, tk), 1)
        before = jnp.concatenate([jnp.where(kpos < qpos, 1.0, 0.0)] * rep, axis=0) > 0.5
        r_i = lax.broadcasted_iota(I32, (tk, tk), 0)
        c_i = lax.broadcasted_iota(I32, (tk, tk), 1)
        later = jnp.where(r_i >= c_i, 1.0, 0.0).astype(BF16)
        for g in range(KV_C):
            k_g = kv[:, g * HD:(g + 1) * HD]
            v_g = kv[:, (KV_C + g) * HD:(KV_C + g + 1) * HD]
            z = _dot_nt(q_scr[g], k_g) * SCALE
            log_keep = jnp.where(
                before, -(jnp.maximum(z, 0.0) + jnp.log1p(jnp.exp(-jnp.abs(z)))), 0.0)
            hi = log_keep.astype(BF16)
            rem = log_keep - hi.astype(F32)
            mid = rem.astype(BF16)
            lo = (rem - mid.astype(F32)).astype(BF16)
            incl = (jnp.dot(hi, later, preferred_element_type=F32)
                    + jnp.dot(mid, later, preferred_element_type=F32)
                    + jnp.dot(lo, later, preferred_element_type=F32))
            carry = c_scr[g][:, :1]
            a = jnp.where(before, jnp.exp(z + incl + carry), 0.0)
            acc_scr[g] = acc_scr[g] + _dot(a, v_g)
            c_scr[g] = jnp.broadcast_to(carry + jnp.sum(log_keep, axis=1, keepdims=True),
                                        c_scr.shape[1:])

    @pl.when(j == nj - 1)
    def _():
        for g in range(KV_C):
            for r in range(rep):
                h = g * rep + r
                o_ref[0, :, h * HD:(h + 1) * HD] = acc_scr[g][r * tq:(r + 1) * tq, :]


def _sb_attn(q, kv, *, tq, tk, pos0, page=None):
    b, t, qw = q.shape
    cw = kv.shape[2]
    paged = page is not None
    nq = t // tq
    if paged:
        pt, off, tail = page
        n_pages = pt.shape[1]
        nj = n_pages + 1
        ix = lambda f: (lambda bi, qb, j, p: f(bi, qb, j))
        kv_spec = pl.BlockSpec(
            (1, tk, cw), lambda bi, qb, j, p: (off + p[bi, jnp.clip(nj - 1 - j, 0, n_pages - 1)], 0, 0))
        extra = [pl.BlockSpec((1, tk, cw), ix(lambda bi, qb, j: (bi, 0, 0)))]
        args = [pt, q, kv, tail]
    else:
        n_pages = 0
        nj = kv.shape[1] // tk
        ix = lambda f: f
        kv_spec = pl.BlockSpec(
            (1, tk, cw),
            lambda bi, qb, j: (bi, jnp.maximum((pos0 + (qb + 1) * tq - 1) // tk - j, 0), 0))
        extra = []
        args = [q, kv]
    in_specs = [pl.BlockSpec((1, tq, qw), ix(lambda bi, qb, j: (bi, qb, 0))), kv_spec] + extra
    out_spec = pl.BlockSpec((1, tq, qw), ix(lambda bi, qb, j: (bi, qb, 0)))
    rows = (H_C // KV_C) * tq
    scratch = [pltpu.VMEM((KV_C, rows, HD), MXU),
               pltpu.VMEM((KV_C, rows, LANES), F32),
               pltpu.VMEM((KV_C, rows, HD), F32)]
    body = functools.partial(_sb_body, tq=tq, tk=tk, nj=nj, n_pages=n_pages, pos0=pos0,
                             paged=paged)
    grid = (b, nq, nj)
    out_shape = jax.ShapeDtypeStruct((b, t, qw), F32)
    sem = _params(("parallel", "parallel", "arbitrary"))
    if paged:
        gs = pltpu.PrefetchScalarGridSpec(num_scalar_prefetch=1, grid=grid, in_specs=in_specs,
                                          out_specs=out_spec, scratch_shapes=scratch)
        return pl.pallas_call(body, grid_spec=gs, out_shape=out_shape, compiler_params=sem,
                              name="sb_attn_paged")(*args)
    return pl.pallas_call(body, grid=grid, in_specs=in_specs, out_specs=out_spec,
                          out_shape=out_shape, scratch_shapes=scratch, compiler_params=sem,
                          name="sb_attn")(*args)


def _mem_body(x_ref, kv_ref, wq_ref, wo_ref, g_ref, b_ref, o_ref):
    x = x_ref[0]
    kv = kv_ref[0]
    q = _dot(x, wq_ref[...])
    outs = []
    for h in range(MEM_HEADS):
        s = _dot_nt(q[:, h * HD:(h + 1) * HD], kv[:, h * HD:(h + 1) * HD]) * SCALE
        p = jnp.exp(s - jnp.max(s, axis=1, keepdims=True))
        p = p / jnp.sum(p, axis=1, keepdims=True)
        outs.append(_dot(p, kv[:, (MEM_HEADS + h) * HD:(MEM_HEADS + h + 1) * HD]))
    y = _dot(jnp.concatenate(outs, axis=1), wo_ref[...])
    o_ref[0] = _ln(ALPHA * x + y, g_ref[...], b_ref[...])


def _mem_layer(x, mem_kv, wq, wo, g, b, *, tq):
    bsz, t, d = x.shape
    nm, cw = mem_kv.shape[1:]
    return pl.pallas_call(
        _mem_body,
        grid=(bsz, t // tq),
        in_specs=[pl.BlockSpec((1, tq, d), lambda bi, qi: (bi, qi, 0)),
                  pl.BlockSpec((1, nm, cw), lambda bi, qi: (bi, 0, 0)),
                  pl.BlockSpec(wq.shape, lambda bi, qi: (0, 0)),
                  pl.BlockSpec(wo.shape, lambda bi, qi: (0, 0)),
                  pl.BlockSpec((1, d), lambda bi, qi: (0, 0)),
                  pl.BlockSpec((1, d), lambda bi, qi: (0, 0))],
        out_specs=pl.BlockSpec((1, tq, d), lambda bi, qi: (bi, qi, 0)),
        out_shape=jax.ShapeDtypeStruct((bsz, t, d), F32),
        compiler_params=_params(("parallel", "parallel")),
        name="mem_layer",
    )(x, mem_kv, wq, wo, g.reshape(1, d), b.reshape(1, d))


def _ffn_body(x_ref, wg_ref, wu_ref, wd_ref, g_ref, b_ref, o_ref, xb_ref, acc_ref, *, nf):
    f = pl.program_id(1)

    @pl.when(f == 0)
    def _():
        xb_ref[...] = x_ref[...].astype(MXU)
        acc_ref[...] = jnp.zeros(acc_ref.shape, F32)

    xb = xb_ref[...]
    gate = jnp.dot(xb, wg_ref[...], preferred_element_type=F32)
    up = jnp.dot(xb, wu_ref[...], preferred_element_type=F32)
    hidden = gate / (1.0 + jnp.exp(-gate)) * up
    acc_ref[...] += _dot(hidden, wd_ref[...])

    @pl.when(f == nf - 1)
    def _():
        o_ref[...] = _ln(ALPHA * x_ref[...] + acc_ref[...], g_ref[...], b_ref[...])


def _ffn_layer(x, w_gu, w_d, g, b, *, tm=512, tf=512):
    m, d = x.shape
    dff = w_d.shape[0]
    tm = _tile(m, tm, 8)
    tf = _tile(dff, tf, LANES)
    nf = dff // tf
    return pl.pallas_call(
        functools.partial(_ffn_body, nf=nf),
        grid=(m // tm, nf),
        in_specs=[pl.BlockSpec((tm, d), lambda i, f: (i, 0)),
                  pl.BlockSpec((d, tf), lambda i, f: (0, f)),
                  pl.BlockSpec((d, tf), lambda i, f: (0, nf + f)),
                  pl.BlockSpec((tf, d), lambda i, f: (f, 0)),
                  pl.BlockSpec((1, d), lambda i, f: (0, 0)),
                  pl.BlockSpec((1, d), lambda i, f: (0, 0))],
        out_specs=pl.BlockSpec((tm, d), lambda i, f: (i, 0)),
        out_shape=jax.ShapeDtypeStruct((m, d), F32),
        scratch_shapes=[pltpu.VMEM((tm, d), MXU), pltpu.VMEM((tm, d), F32)],
        compiler_params=_params(("parallel", "arbitrary")),
        name="ffn_layer",
    )(x, w_gu, w_gu, w_d, g.reshape(1, d), b.reshape(1, d))


def _even_odd_rows(a):
    b, n2, c = a.shape
    return a.reshape(b, n2 // 2, 2, c).transpose(0, 2, 1, 3).reshape(b, n2, c)


def _pad_rows(a, rows):
    return jnp.pad(a, ((0, 0), (0, rows - a.shape[1]), (0, 0)))


def _ab_mixer(x, w, layer, past, *, pos0):
    b, t, d = x.shape
    m = b * t
    pos = pos0 + jnp.arange(t, dtype=I32)
    tabs = _rope_tables(pos, HD) + _rope_tables(pos, D_IDX)
    h = _mm(x.reshape(m, d), w["in_ab"][layer], tn=640)
    q_a, kv_a, q_i, q_b, cmp_kv, sel_kv, win_kv, misc = _post_ab(h, tabs, t)
    r3 = lambda a: a.reshape(b, t, a.shape[1])
    q_a, kv_a, q_i, q_b, cmp_kv, sel_kv, win_kv, misc = map(
        r3, (q_a, kv_a, q_i, q_b, cmp_kv, sel_kv, win_kv, misc))
    cw = 2 * KV_B * HD
    if past is None:
        s_len = t
        tq = _tile(t, 256, 8)
        tk = _tile(t, 512, LANES)
        topk = min(TOPK_MAX, s_len // 4)
        mask_a = _dsa_index(q_i, misc, misc, topk=topk, pos0=0, tq=tq, tk=tk)
        o_a = _attn(q_a, kv_a, groups=KV_A, rep=H_A // KV_A, tq=tq, tk=tk, pos0=0, mask=mask_a)
        nc = s_len // CMP_BLOCK
        kc = _mm(cmp_kv.reshape(m // CMP_BLOCK, CMP_BLOCK * cw), w["phi"][layer], tn=cw)
        kc = _even_odd_rows(kc.reshape(b, nc, cw))
        s_pad = s_len
        o_w = _attn(q_b, win_kv, groups=KV_B, rep=H_B // KV_B, tq=tq, tk=tq, pos0=0)
        sel_attn = lambda msk: _attn(q_b, sel_kv, groups=KV_B, rep=H_B // KV_B, tq=tq, tk=tk,
                                     pos0=0, mask=msk)
        win_state = win_kv[:, t - min(WINDOW, t):]
    else:
        pt, caches, win_prev = past
        n_pages = pt.shape[1]
        page = caches["a_kv"].shape[1]
        n_pool = caches["a_kv"].shape[0] // caches["n_layers"]
        off = layer * n_pool
        s_len = n_pages * page + t
        s_pad = (n_pages + 1) * page
        tq, tk = t, page
        topk = min(TOPK_MAX, s_len // 4)
        mask_a = _dsa_index(q_i, misc, caches["a_kidx"], topk=topk, pos0=pos0, tq=tq, tk=tk,
                            page=(pt, off, _pad_rows(misc[:, :, :D_IDX], page)))
        o_a = _attn(q_a, caches["a_kv"], groups=KV_A, rep=H_A // KV_A, tq=tq, tk=tk, pos0=pos0,
                    mask=mask_a, page=(pt, off, _pad_rows(kv_a, page)))
        nc = s_len // CMP_BLOCK
        per_page = page // CMP_BLOCK
        assert nc == n_pages * per_page, "new rows must not complete a compressed block"
        kc_pool = _mm(caches["cmp"], w["phi"][layer], tn=cw,
                      rows=(off * per_page, n_pool * per_page))
        kc = _gather_pages(kc_pool.reshape(n_pool, per_page, cw), pt, 0)
        kc = _even_odd_rows(kc)
        win_all = jnp.concatenate([win_prev, win_kv], axis=1)
        k_start = n_pages * page - win_prev.shape[1]
        o_w = _attn(q_b, _pad_rows(win_all, -(-win_all.shape[1] // page) * page),
                    groups=KV_B, rep=H_B // KV_B, tq=tq, tk=tk, pos0=pos0, k_start=k_start)
        sel_attn = lambda msk: _attn(q_b, caches["sel"], groups=KV_B, rep=H_B // KV_B, tq=tq,
                                     tk=tk, pos0=pos0, mask=msk,
                                     page=(pt, off, _pad_rows(sel_kv, page)))
        win_state = win_all[:, win_all.shape[1] - win_prev.shape[1]:]
    assert nc % 2 == 0
    c_pos = (jnp.arange(nc, dtype=I32) + 1) * CMP_BLOCK - 1
    ctab = tuple(_even_odd_rows(tb[None])[0] for tb in _rope_tables(c_pos, HD))
    o_c, mask_s = _nsa_cmp(q_b, kc, ctab, s_len=s_len, s_pad=s_pad, pos0=pos0, tq=tq)
    o_s = sel_attn(mask_s)
    f2 = lambda a: a.reshape(m, a.shape[2])
    o = _combine(f2(o_a), f2(o_c), f2(o_s), f2(o_w), f2(misc))
    states = (kv_a, misc[:, :, :D_IDX], cmp_kv, sel_kv, win_state)
    return o, states


def _sb_mixer(x, w, layer, past, *, pos0):
    b, t, d = x.shape
    m = b * t
    x2 = x.reshape(m, d)
    q = _mm(x2, w["in_c_q"][layer]).reshape(b, t, H_C * HD)
    kv = _mm(x2, w["in_c_kv"][layer]).reshape(b, t, 2 * KV_C * HD)
    if past is None:
        tq = _tile(t, 256, 8)
        o = _sb_attn(q, kv, tq=tq, tk=tq, pos0=0)
    else:
        pt, caches = past
        page = caches["c_kv"].shape[1]
        n_pool = caches["c_kv"].shape[0] // caches["n_layers"]
        o = _sb_attn(q, caches["c_kv"], tq=t, tk=page, pos0=pos0,
                     page=(pt, layer * n_pool, _pad_rows(kv, page)))
    return o.reshape(m, H_C * HD), kv


def _run_group(x, w, mem_kvs, pasts, ln_g, ln_b, *, pos0):
    b, t, d = x.shape
    m = b * t
    even, odd = [], []
    for i in range(DEPTH):
        if i % 2 == 0:
            mix, st = _ab_mixer(x, w, i // 2, pasts[i], pos0=pos0)
            even.append(st)
        else:
            mix, st = _sb_mixer(x, w, i // 2, pasts[i], pos0=pos0)
            odd.append(st)
        x2 = _mm_ln(mix, w["out"][i], x.reshape(m, d), ln_g[i, 0], ln_b[i, 0])
        x3 = _mem_layer(x2.reshape(b, t, d), mem_kvs[i], w["mem_q"][i], w["mem_o"][i],
                        ln_g[i, 1], ln_b[i, 1], tq=_tile(t, 256, 8))
        x = _ffn_layer(x3.reshape(m, d), w["gate_up"][i], w["down"][i],
                       ln_g[i, 2], ln_b[i, 2]).reshape(b, t, d)
    return x, even, odd


def _reorder_in_ab(w):
    sizes = (H_A * HD, 2 * KV_A * HD, H_IDX * D_IDX, D_IDX, H_IDX,
             H_B * HD, 2 * KV_B * HD, 2 * KV_B * HD, 2 * KV_B * HD, 3 * H_B)
    offs = np.concatenate([[0], np.cumsum(sizes)])
    piece = lambda i: w[..., offs[i]:offs[i + 1]]
    order = [0, 1, 2, 5, 6, 7, 8, 3, 4, 9]
    cols = [piece(i) for i in order]
    pad = C_END - int(offs[-1])
    cols.append(jnp.zeros(w.shape[:-1] + (pad,), w.dtype))
    return jnp.concatenate(cols, axis=-1)


def _phi_matrix(w_phi):
    phi = w_phi.reshape(2, CMP_BLOCK, HD, HD)
    nb = 2 * KV_B
    out = jnp.zeros((CMP_BLOCK, nb, HD, nb, HD), w_phi.dtype)
    for c in range(nb):
        out = out.at[:, c, :, c, :].set(phi[c // KV_B])
    return out.reshape(CMP_BLOCK * nb * HD, nb * HD)


def kernel(x_prompt, x_sample, cache_a_kv, cache_a_kidx, cache_b_cmp_kv, cache_b_sel_kv,
           state_b_win_kv, cache_c_kv, cache_mem_kv, page_table, mem_prompt, w_in_ab, w_cmp_phi,
           w_in_c, w_out, w_mem_q, w_mem_kv, w_mem_o, w_gate_up, w_down, ln_g, ln_b):
    b_p, t_p, d = x_prompt.shape
    b_s, t_s, _ = x_sample.shape
    n_even, n_pool, page = cache_a_kv.shape[:3]
    n_odd = cache_c_kv.shape[0]
    n_mem = mem_prompt.shape[1]
    past_len = page_table.shape[1] * page

    def per_layer(a):
        return [a[i].astype(MXU) for i in range(a.shape[0])]

    w = {
        "in_ab": [_reorder_in_ab(wl).astype(MXU) for wl in w_in_ab],
        "phi": [_phi_matrix(wl).astype(MXU) for wl in w_cmp_phi],
        "in_c_q": per_layer(w_in_c[:, :, :H_C * HD]),
        "in_c_kv": per_layer(w_in_c[:, :, H_C * HD:]),
        "out": per_layer(w_out),
        "mem_q": per_layer(w_mem_q),
        "mem_o": per_layer(w_mem_o),
        "gate_up": per_layer(w_gate_up),
        "down": per_layer(w_down),
    }
    w_mkv = per_layer(w_mem_kv)

    mem_flat = mem_prompt.reshape(b_p * n_mem, d)
    mem_p = [_mm(mem_flat, w_mkv[i]).reshape(b_p, n_mem, 2 * MEM_HEADS * HD) for i in range(DEPTH)]
    y_p, ev_p, od_p = _run_group(x_prompt, w, mem_p, [None] * DEPTH, ln_g, ln_b, pos0=0)

    cw_a, cw_b, cw_c = 2 * KV_A * HD, 2 * KV_B * HD, 2 * KV_C * HD
    caches = {
        "n_layers": n_even,
        "a_kv": cache_a_kv.reshape(n_even * n_pool, page, cw_a),
        "a_kidx": cache_a_kidx.reshape(n_even * n_pool, page, D_IDX),
        "cmp": cache_b_cmp_kv.reshape(n_even * n_pool * (page // CMP_BLOCK), CMP_BLOCK * cw_b),
        "sel": cache_b_sel_kv.reshape(n_even * n_pool, page, cw_b),
    }
    caches_c = {"n_layers": n_odd, "c_kv": cache_c_kv.reshape(n_odd * n_pool, page, cw_c)}
    win_prev = state_b_win_kv.reshape(n_even, b_s, state_b_win_kv.shape[2], cw_b)
    pasts = []
    for i in range(DEPTH):
        if i % 2 == 0:
            pasts.append((page_table, caches, win_prev[i // 2]))
        else:
            pasts.append((page_table, caches_c))
    mem_s = [cache_mem_kv[i].reshape(b_s, n_mem, 2 * MEM_HEADS * HD) for i in range(DEPTH)]
    y_s, ev_s, od_s = _run_group(x_sample, w, mem_s, pasts, ln_g, ln_b, pos0=past_len)

    def pack(states, bsz):
        def kv5(a, g):
            return a.reshape(bsz, a.shape[1], 2, g, HD)
        return (jnp.stack([kv5(s[0], KV_A) for s in states]),
                jnp.stack([s[1] for s in states]),
                jnp.stack([kv5(s[2], KV_B) for s in states]),
                jnp.stack([kv5(s[3], KV_B) for s in states]),
                jnp.stack([kv5(s[4], KV_B) for s in states]))

    a_p, ki_p, cmp_p, sel_p, win_p = pack(ev_p, b_p)
    a_s, ki_s, cmp_s, sel_s, win_s = pack(ev_s, b_s)
    c_p = jnp.stack([s.reshape(b_p, t_p, 2, KV_C, HD) for s in od_p])
    c_s = jnp.stack([s.reshape(b_s, t_s, 2, KV_C, HD) for s in od_s])
    mem_out = jnp.stack([mk.reshape(b_p, n_mem, 2, MEM_HEADS, HD) for mk in mem_p])
    return (y_p, y_s, a_p, ki_p, cmp_p, sel_p, win_p, c_p, mem_out,
            a_s, ki_s, cmp_s, sel_s, win_s, c_s)
```

```python
import functools
import math

import jax
import jax.numpy as jnp
import numpy as np
from jax import lax
from jax.experimental import pallas as pl
from jax.experimental.pallas import tpu as pltpu

F32 = jnp.float32
BF16 = jnp.bfloat16
I32 = jnp.int32

HD = 128
LANES = 128
H_A, KV_A, H_IDX, D_IDX = 8, 2, 4, 64
H_B, KV_B = 8, 2
H_C, KV_C = 16, 4
MEM_HEADS = 4
TOPK_MAX = 256
CMP_BLOCK, SEL_BLOCK, N_SEL, WINDOW = 32, 64, 16, 512
ROPE_THETA = 10000.0
LN_EPS = 1e-5
DEPTH = 4
ALPHA = (2 * DEPTH) ** 0.25
SCALE = HD ** -0.5
QSCALE = SCALE * math.log2(math.e)
NEG = -1e30
INT_MIN = -2 ** 31
VMEM_LIMIT = 56 * 1024 * 1024
PAGES_PER_STEP = 16

C_QA, C_KVA, C_QI, C_QB, C_CMP, C_SEL, C_WIN, C_MISC, C_END = (
    0, 1024, 1536, 1792, 2816, 3328, 3840, 4352, 4480)
MISC_WI = D_IDX
MISC_G = D_IDX + H_IDX


def _params(sem):
    return pltpu.CompilerParams(dimension_semantics=sem, vmem_limit_bytes=VMEM_LIMIT)


def _tile(n, pref, align):
    t = (min(pref, n) // align) * align
    while t >= align:
        if n % t == 0:
            return t
        t -= align
    return n


def _ln(y, g, b):
    mu = jnp.mean(y, axis=-1, keepdims=True)
    d = y - mu
    var = jnp.mean(d * d, axis=-1, keepdims=True)
    return d * lax.rsqrt(var + LN_EPS) * g + b


MXU = BF16


def _dot(a, b):
    return jnp.dot(a.astype(MXU), b.astype(MXU), preferred_element_type=F32)


def _dot_nt(a, b):
    return lax.dot_general(a.astype(MXU), b.astype(MXU), (((1,), (1,)), ((), ())),
                           preferred_element_type=F32)


def _call(body, *, grid, in_specs, out_specs, out_shape, scratch=(), sem, name, prefetch=None):
    if prefetch is None:
        return pl.pallas_call(body, grid=grid, in_specs=in_specs, out_specs=out_specs,
                              out_shape=out_shape, scratch_shapes=list(scratch),
                              compiler_params=_params(sem), name=name)
    gs = pltpu.PrefetchScalarGridSpec(num_scalar_prefetch=1, grid=grid, in_specs=in_specs,
                                      out_specs=out_specs, scratch_shapes=list(scratch))
    return functools.partial(
        pl.pallas_call(body, grid_spec=gs, out_shape=out_shape, compiler_params=_params(sem),
                       name=name), prefetch)


def _mm_body(x_ref, w_ref, o_ref, acc_ref, *, nk):
    part = _dot(x_ref[...], w_ref[...])
    if nk == 1:
        o_ref[...] = part
        return
    k = pl.program_id(2)

    @pl.when(k == 0)
    def _():
        acc_ref[...] = part

    @pl.when(k > 0)
    def _():
        acc_ref[...] += part

    @pl.when(k == nk - 1)
    def _():
        o_ref[...] = acc_ref[...]


def _mm(x, w, *, tm=1024, tn=512, tk=2048, rows=None):
    row0, m = rows if rows is not None else (0, x.shape[0])
    kd = x.shape[1]
    n = w.shape[1]
    tm = _tile(math.gcd(m, row0), tm, 8)
    tn = _tile(n, tn, LANES)
    tk = _tile(kd, tk, LANES)
    nk = kd // tk
    blk0 = row0 // tm
    return pl.pallas_call(
        functools.partial(_mm_body, nk=nk),
        grid=(m // tm, n // tn, nk),
        in_specs=[pl.BlockSpec((tm, tk), lambda i, j, k: (blk0 + i, k)),
                  pl.BlockSpec((tk, tn), lambda i, j, k: (k, j))],
        out_specs=pl.BlockSpec((tm, tn), lambda i, j, k: (i, j)),
        out_shape=jax.ShapeDtypeStruct((m, n), F32),
        scratch_shapes=[pltpu.VMEM((tm, tn), F32)],
        compiler_params=_params(("parallel", "parallel", "arbitrary")),
        name="mm",
    )(x, w)


def _mm_ln_body(x_ref, w_ref, r_ref, g_ref, b_ref, o_ref, acc_ref, *, nk):
    k = pl.program_id(1)
    part = _dot(x_ref[...], w_ref[...])

    @pl.when(k == 0)
    def _():
        acc_ref[...] = part

    @pl.when(k > 0)
    def _():
        acc_ref[...] += part

    @pl.when(k == nk - 1)
    def _():
        o_ref[...] = _ln(ALPHA * r_ref[...] + acc_ref[...], g_ref[...], b_ref[...])


def _mm_ln(x, w, res, g, b, *, tm=512, tk=512):
    m, kd = x.shape
    d = w.shape[1]
    tm = _tile(m, tm, 8)
    tk = _tile(kd, tk, LANES)
    nk = kd // tk
    return pl.pallas_call(
        functools.partial(_mm_ln_body, nk=nk),
        grid=(m // tm, nk),
        in_specs=[pl.BlockSpec((tm, tk), lambda i, k: (i, k)),
                  pl.BlockSpec((tk, d), lambda i, k: (k, 0)),
                  pl.BlockSpec((tm, d), lambda i, k: (i, 0)),
                  pl.BlockSpec((1, d), lambda i, k: (0, 0)),
                  pl.BlockSpec((1, d), lambda i, k: (0, 0))],
        out_specs=pl.BlockSpec((tm, d), lambda i, k: (i, 0)),
        out_shape=jax.ShapeDtypeStruct((m, d), F32),
        scratch_shapes=[pltpu.VMEM((tm, d), F32)],
        compiler_params=_params(("parallel", "arbitrary")),
        name="mm_ln",
    )(x, w, res, g.reshape(1, d), b.reshape(1, d))


def _post_ab_body(h_ref, c128_ref, s128_ref, c64_ref, s64_ref,
                  qa_ref, kva_ref, qi_ref, qb_ref, cmp_ref, sel_ref, win_ref, misc_ref):
    cos = c128_ref[...]
    sin = s128_ref[...]
    ci = c64_ref[...]
    si = s64_ref[...]
    lane = lax.broadcasted_iota(I32, cos.shape, 1)
    first_half = (lane & (D_IDX - 1)) < (D_IDX // 2)

    def rope128(x):
        return x * cos + pltpu.roll(x, HD // 2, 1) * sin

    def rope64(x):
        partner = jnp.where(first_half, pltpu.roll(x, LANES - D_IDX // 2, 1),
                            pltpu.roll(x, D_IDX // 2, 1))
        return x * ci + partner * si

    def head(c0, i):
        return h_ref[:, c0 + i * HD:c0 + (i + 1) * HD]

    for i in range(H_A):
        qa_ref[:, i * HD:(i + 1) * HD] = rope128(head(C_QA, i))
    for i in range(H_B):
        qb_ref[:, i * HD:(i + 1) * HD] = rope128(head(C_QB, i))
    for src, dst, kv in ((C_KVA, kva_ref, KV_A), (C_SEL, sel_ref, KV_B), (C_WIN, win_ref, KV_B)):
        for i in range(kv):
            dst[:, i * HD:(i + 1) * HD] = rope128(head(src, i))
            dst[:, (kv + i) * HD:(kv + i + 1) * HD] = head(src, kv + i)
    cmp_ref[...] = h_ref[:, C_CMP:C_SEL]
    for i in range(H_IDX * D_IDX // LANES):
        qi_ref[:, i * LANES:(i + 1) * LANES] = rope64(head(C_QI, i))
    raw = h_ref[:, C_MISC:C_END]
    misc_ref[...] = jnp.where(
        lane < MISC_WI, rope64(raw),
        jnp.where(lane < MISC_G, raw * (H_IDX ** -0.5),
                  jnp.where(lane < MISC_G + 3 * H_B, 1.0 / (1.0 + jnp.exp(-raw)), 0.0)))


def _post_ab(h, tabs, t):
    m = h.shape[0]
    tm = _tile(t, 256, 8)
    nt = t // tm
    widths = (H_A * HD, 2 * KV_A * HD, H_IDX * D_IDX, H_B * HD, 2 * KV_B * HD,
              2 * KV_B * HD, 2 * KV_B * HD, LANES)
    tab_spec = pl.BlockSpec((tm, LANES), lambda i: (i % nt, 0))
    return pl.pallas_call(
        _post_ab_body,
        grid=(m // tm,),
        in_specs=[pl.BlockSpec((tm, C_END), lambda i: (i, 0))] + [tab_spec] * 4,
        out_specs=[pl.BlockSpec((tm, w), lambda i: (i, 0)) for w in widths],
        out_shape=[jax.ShapeDtypeStruct((m, w), F32) for w in widths],
        compiler_params=_params(("parallel",)),
        name="post_ab",
    )(h, *tabs)


def _rope_tables(pos, d):
    half = d // 2
    inv = ROPE_THETA ** (-(jnp.arange(half, dtype=F32) * 2.0 / d))
    ang = pos.astype(F32)[:, None] * inv[None, :]
    cos, sin = jnp.cos(ang), jnp.sin(ang)
    reps = LANES // d
    return (jnp.tile(jnp.concatenate([cos, cos], axis=1), (1, reps)),
            jnp.tile(jnp.concatenate([-sin, sin], axis=1), (1, reps)))


def _sortable(x):
    bits = lax.bitcast_convert_type(jnp.where(x == 0.0, 0.0, x), I32)
    return bits ^ ((bits >> 31) & 0x7FFFFFFF)


def _dsa_index_body(*refs, tq, wk, n_kx, n_main, tail_w, topk, pos0, causal_loop):
    refs = list(refs)
    if tail_w:
        refs.pop(0)
    qi_ref, wi_ref = refs[0], refs[1]
    kx_refs = refs[2:2 + n_kx]
    i = 2 + n_kx
    tail_ref = None
    if tail_w:
        tail_ref = refs[i]
        i += 1
    bias_ref, key_ref = refs[i], refs[i + 1]
    qb = pl.program_id(1)
    j = pl.program_id(2)
    nch = key_ref.shape[0]
    nj = n_main + (1 if tail_w else 0)
    qi = qi_ref[0]
    wi = wi_ref[0]

    def put_keys(kx, chunk0, width, kpos0):
        score = jnp.zeros((tq, width), F32)
        for h in range(H_IDX):
            dots = _dot_nt(qi[:, h * D_IDX:(h + 1) * D_IDX], kx) * (D_IDX ** -0.5)
            score = score + wi[:, MISC_WI + h:MISC_WI + h + 1] * jnp.maximum(dots, 0.0)
        qpos = pos0 + qb * tq + lax.broadcasted_iota(I32, (tq, width), 0)
        kpos = kpos0 + lax.broadcasted_iota(I32, (tq, width), 1)
        keys = jnp.where(kpos <= qpos, _sortable(score), INT_MIN)
        for u in range(width // LANES):
            key_ref[chunk0 + u] = keys[:, u * LANES:(u + 1) * LANES]

    def main_step():
        kx = jnp.concatenate([r[0][:, :D_IDX] for r in kx_refs], axis=0)
        put_keys(kx, j * (wk // LANES), wk, j * wk)

    if tail_w:
        pl.when(j < n_main)(main_step)

        @pl.when(j == n_main)
        def _():
            put_keys(tail_ref[0][:, :D_IDX], n_main * (wk // LANES), tail_w, n_main * wk)
    else:
        main_step()

    @pl.when(j == nj - 1)
    def _():
        if causal_loop:
            n_live = jnp.minimum(nch, (pos0 + (qb + 1) * tq - 1) // LANES + 1)

            def count(pred):
                def chunk(c, acc):
                    return acc + pred(key_ref[c]).astype(I32)
                acc = lax.fori_loop(0, n_live, chunk, jnp.zeros((tq, LANES), I32))
                return jnp.sum(acc, axis=1, keepdims=True)
        else:
            def count(pred):
                hit = pred(key_ref[...]).astype(I32)
                return jnp.sum(jnp.sum(hit, axis=0), axis=1, keepdims=True)

        ans0 = jnp.where(count(lambda k: k >= 0) >= topk, 0, INT_MIN)

        def bit_step(b, ans):
            cand = ans | jnp.left_shift(jnp.int32(1), 30 - b)
            return jnp.where(count(lambda k: k >= cand) >= topk, cand, ans)

        kth = lax.fori_loop(0, 31, bit_step, ans0)
        need = (topk - count(lambda k: k > kth)).astype(F32)
        r_i = lax.broadcasted_iota(I32, (LANES, LANES), 0)
        c_i = lax.broadcasted_iota(I32, (LANES, LANES), 1)
        strict = jnp.where(r_i < c_i, 1.0, 0.0).astype(BF16)
        seen = jnp.zeros((tq, 1), F32)
        for c in range(nch):
            kc = key_ref[c]
            tie = jnp.where(kc == kth, 1.0, 0.0)
            rank = seen + jnp.dot(tie.astype(BF16), strict, preferred_element_type=F32)
            take = jnp.where(kc > kth, 1.0, jnp.where(rank < need, tie, 0.0))
            keep = jnp.where(kc > INT_MIN, take, 0.0)
            bias_ref[0, 0, :, c * LANES:(c + 1) * LANES] = (
                jnp.where(keep > 0.5, 0.0, NEG).astype(bias_ref.dtype))
            seen = seen + jnp.sum(tie, axis=1, keepdims=True)


def _dsa_index(qi, misc, kx, *, topk, pos0, tq, tk, page=None):
    b, t, _ = qi.shape
    nq = t // tq
    if page is not None:
        pt, off, tail = page
        n_pages = pt.shape[1]
        n_kx = _tile(n_pages, PAGES_PER_STEP, 1)
        n_main = n_pages // n_kx
        tail_w = tk
        wk = n_kx * tk
        ix = lambda f: (lambda bi, qb, j, p: f(bi, qb, j))
        kx_specs = [pl.BlockSpec(
            (1, tk, kx.shape[2]),
            functools.partial(lambda u, bi, qb, j, p:
                              (off + p[bi, jnp.minimum(j, n_main - 1) * n_kx + u], 0, 0), u))
            for u in range(n_kx)]
        extra = [pl.BlockSpec((1, tk, tail.shape[2]), ix(lambda bi, qb, j: (bi, 0, 0)))]
        args = [qi, misc] + [kx] * n_kx + [tail]
        prefetch = pt
        s_pad = n_pages * tk + tail_w
    else:
        n_kx, n_main, tail_w, wk = 1, kx.shape[1] // tk, 0, tk
        ix = lambda f: f
        kx_specs = [pl.BlockSpec((1, tk, kx.shape[2]), lambda bi, qb, j: (bi, j, 0))]
        extra = []
        args = [qi, misc, kx]
        prefetch = None
        s_pad = kx.shape[1]
    nj = n_main + (1 if tail_w else 0)
    dtype = BF16 if tq % 16 == 0 else F32
    in_specs = [pl.BlockSpec((1, tq, qi.shape[2]), ix(lambda bi, qb, j: (bi, qb, 0))),
                pl.BlockSpec((1, tq, LANES), ix(lambda bi, qb, j: (bi, qb, 0)))] + kx_specs + extra
    body = functools.partial(_dsa_index_body, tq=tq, wk=wk, n_kx=n_kx, n_main=n_main,
                             tail_w=tail_w, topk=topk, pos0=pos0, causal_loop=nq > 1)
    return _call(body, grid=(b, nq, nj), in_specs=in_specs,
                 out_specs=pl.BlockSpec((1, 1, tq, s_pad), ix(lambda bi, qb, j: (bi, 0, qb, 0))),
                 out_shape=jax.ShapeDtypeStruct((b, 1, t, s_pad), dtype),
                 scratch=[pltpu.VMEM((s_pad // LANES, tq, LANES), I32)],
                 sem=("parallel", "parallel", "arbitrary"), name="dsa_index",
                 prefetch=prefetch)(*args)


def _flash_init(q_ref, q_scr, m_scr, l_scr, acc_scr, groups, rep, tq):
    m_scr[...] = jnp.full(m_scr.shape, NEG, F32)
    l_scr[...] = jnp.zeros(l_scr.shape, F32)
    acc_scr[...] = jnp.zeros(acc_scr.shape, F32)
    for g in range(groups):
        for r in range(rep):
            h = g * rep + r
            q_scr[g, r * tq:(r + 1) * tq, :] = (q_ref[0, :, h * HD:(h + 1) * HD] * QSCALE).astype(MXU)


def _flash_step(g, k, v, bias, q_scr, m_scr, l_scr, acc_scr, rep):
    s = _dot_nt(q_scr[g], k) + jnp.concatenate([bias.astype(F32)] * rep, axis=0)
    m_prev = m_scr[g][:, :1]
    l_prev = l_scr[g][:, :1]
    m_new = jnp.maximum(m_prev, jnp.max(s, axis=1, keepdims=True))
    p = jnp.exp2(s - m_new)
    a = jnp.exp2(m_prev - m_new)
    l_new = a * l_prev + jnp.sum(p, axis=1, keepdims=True)
    acc_scr[g] = a * acc_scr[g] + _dot(p, v)
    m_scr[g] = jnp.broadcast_to(m_new, m_scr.shape[1:])
    l_scr[g] = jnp.broadcast_to(l_new, l_scr.shape[1:])


def _flash_final(o_ref, m_scr, l_scr, acc_scr, groups, rep, tq):
    for g in range(groups):
        for r in range(rep):
            h = g * rep + r
            rows = slice(r * tq, (r + 1) * tq)
            seen = m_scr[g][rows, :1] > 0.5 * NEG
            o_ref[0, :, h * HD:(h + 1) * HD] = jnp.where(
                seen, acc_scr[g][rows, :] / l_scr[g][rows, :1], 0.0)


def _flash_scratch(groups, rep, tq):
    rows = rep * tq
    return [pltpu.VMEM((groups, rows, HD), MXU),
            pltpu.VMEM((groups, rows, LANES), F32),
            pltpu.VMEM((groups, rows, LANES), F32),
            pltpu.VMEM((groups, rows, HD), F32)]


def _attn_body(*refs, groups, rep, tq, tk, nj, pos0, k_start, band, lead, gm):
    q_ref, kv_ref = refs[0], refs[1]
    bias_ref = None if band else refs[2]
    o_ref, q_scr, m_scr, l_scr, acc_scr = refs[2 if band else 3:]
    qb = pl.program_id(1)
    j = pl.program_id(2)
    kb = qb - lead + j if band else j

    @pl.when(j == 0)
    def _():
        _flash_init(q_ref, q_scr, m_scr, l_scr, acc_scr, groups, rep, tq)

    last_q = pos0 + qb * tq + tq - 1

    @pl.when((kb >= 0) & (k_start + kb * tk <= last_q))
    def _():
        kv = kv_ref[0]
        if band:
            qpos = pos0 + qb * tq + lax.broadcasted_iota(I32, (tq, tk), 0)
            kpos = k_start + kb * tk + lax.broadcasted_iota(I32, (tq, tk), 1)
            bias = jnp.where(kpos <= qpos, jnp.where(kpos > qpos - WINDOW, 0.0, NEG), NEG)
        for g in range(groups):
            if not band:
                bias = bias_ref[0, g if gm > 1 else 0]
            _flash_step(g, kv[:, g * HD:(g + 1) * HD],
                        kv[:, (groups + g) * HD:(groups + g + 1) * HD],
                        bias, q_scr, m_scr, l_scr, acc_scr, rep)

    @pl.when(j == nj - 1)
    def _():
        _flash_final(o_ref, m_scr, l_scr, acc_scr, groups, rep, tq)


def _attn(q, kv, *, groups, rep, tq, tk, pos0, bias=None, k_start=0):
    b, t, qw = q.shape
    band = bias is None
    cw = kv.shape[2]
    nq = t // tq
    lead = WINDOW // tk if (band and nq > 1) else 0
    nj = (lead + 1) if (band and nq > 1) else kv.shape[1] // tk

    def kblock(qb, j):
        if band:
            return jnp.maximum(qb - lead + j, 0)
        return jnp.minimum(j, (pos0 + (qb + 1) * tq - 1) // tk)

    in_specs = [pl.BlockSpec((1, tq, qw), lambda bi, qb, j: (bi, qb, 0)),
                pl.BlockSpec((1, tk, cw), lambda bi, qb, j: (bi, kblock(qb, j), 0))]
    args = [q, kv]
    gm = 1
    if not band:
        gm = bias.shape[1]
        in_specs.append(pl.BlockSpec((1, gm, tq, tk), lambda bi, qb, j: (bi, 0, qb, kblock(qb, j))))
        args.append(bias)
    body = functools.partial(_attn_body, groups=groups, rep=rep, tq=tq, tk=tk, nj=nj, pos0=pos0,
                             k_start=k_start, band=band, lead=lead, gm=gm)
    return _call(body, grid=(b, nq, nj), in_specs=in_specs,
                 out_specs=pl.BlockSpec((1, tq, qw), lambda bi, qb, j: (bi, qb, 0)),
                 out_shape=jax.ShapeDtypeStruct((b, t, qw), F32),
                 scratch=_flash_scratch(groups, rep, tq),
                 sem=("parallel", "parallel", "arbitrary"),
                 name="attn_band" if band else "attn_bias")(*args)


def _attn_paged_body(_, q_ref, *refs, groups, rep, tq, page, n_pg, n_main, gm):
    pages = refs[:n_pg]
    tail_ref, bias_ref, bias_tail_ref, o_ref, q_scr, m_scr, l_scr, acc_scr = refs[n_pg:]
    j = pl.program_id(2)
    rpt = 2 * groups

    @pl.when(j == 0)
    def _():
        _flash_init(q_ref, q_scr, m_scr, l_scr, acc_scr, groups, rep, tq)

    @pl.when(j < n_main)
    def _():
        for g in range(groups):
            k = jnp.concatenate([pg[pl.ds(g, page, stride=rpt), :] for pg in pages], axis=0)
            v = jnp.concatenate([pg[pl.ds(groups + g, page, stride=rpt), :] for pg in pages], axis=0)
            _flash_step(g, k, v, bias_ref[0, g if gm > 1 else 0], q_scr, m_scr, l_scr, acc_scr, rep)

    @pl.when(j == n_main)
    def _():
        kv = tail_ref[0]
        for g in range(groups):
            _flash_step(g, kv[:, g * HD:(g + 1) * HD],
                        kv[:, (groups + g) * HD:(groups + g + 1) * HD],
                        bias_tail_ref[0, g if gm > 1 else 0], q_scr, m_scr, l_scr, acc_scr, rep)
        _flash_final(o_ref, m_scr, l_scr, acc_scr, groups, rep, tq)


def _attn_paged(q, pool, pt, off, tail, bias, *, groups, rep, page):
    b, t, qw = q.shape
    n_pages = pt.shape[1]
    n_pg = _tile(n_pages, PAGES_PER_STEP, 1)
    n_main = n_pages // n_pg
    rpt = 2 * groups
    gm = bias.shape[1]
    ix = lambda f: (lambda bi, qb, j, p: f(bi, qb, j))
    page_specs = [pl.BlockSpec(
        (page * rpt, LANES),
        functools.partial(lambda u, bi, qb, j, p:
                          (off + p[bi, jnp.minimum(j, n_main - 1) * n_pg + u], 0), u))
        for u in range(n_pg)]
    in_specs = ([pl.BlockSpec((1, t, qw), ix(lambda bi, qb, j: (bi, 0, 0)))] + page_specs + [
        pl.BlockSpec((1, page, tail.shape[2]), ix(lambda bi, qb, j: (bi, 0, 0))),
        pl.BlockSpec((1, gm, t, n_pg * page),
                     ix(lambda bi, qb, j: (bi, 0, 0, jnp.minimum(j, n_main - 1)))),
        pl.BlockSpec((1, gm, t, page), ix(lambda bi, qb, j: (bi, 0, 0, n_pages)))])
    body = functools.partial(_attn_paged_body, groups=groups, rep=rep, tq=t, page=page,
                             n_pg=n_pg, n_main=n_main, gm=gm)
    return _call(body, grid=(b, 1, n_main + 1), in_specs=in_specs,
                 out_specs=pl.BlockSpec((1, t, qw), ix(lambda bi, qb, j: (bi, 0, 0))),
                 out_shape=jax.ShapeDtypeStruct((b, t, qw), F32),
                 scratch=_flash_scratch(groups, rep, t),
                 sem=("parallel", "parallel", "arbitrary"), name="attn_paged",
                 prefetch=pt)(q, *([pool] * n_pg), tail, bias, bias)


def _nsa_cmp_body(q_ref, kc_ref, cc_ref, cs_ref, oc_ref, bias_ref, *, tq, nc, ns, nsp, s_pad,
                  chunk, pos0):
    qb = pl.program_id(1)
    half = nc // 2
    rep = H_B // KV_B
    kc = kc_ref[0]
    cos = cc_ref[...]
    sin = cs_ref[...]
    qpos1 = pos0 + qb * tq + lax.broadcasted_iota(I32, (tq, 1), 0)
    n_perm = lax.broadcasted_iota(I32, (tq, nc), 1)
    n_orig = jnp.where(n_perm < half, 2 * n_perm, 2 * (n_perm - half) + 1)
    c_ok = jnp.where(((n_orig + 1) * CMP_BLOCK - 1) <= qpos1, 1.0, 0.0)
    c_okr = jnp.concatenate([c_ok] * rep, axis=0) > 0.5
    blk = lax.broadcasted_iota(I32, (tq, nsp), 1)
    sel_shift = SEL_BLOCK.bit_length() - 1
    cur = qpos1 >> sel_shift
    forced = (blk == 0) | (blk == cur) | (blk == cur - 1)
    b_causal = blk * SEL_BLOCK <= qpos1
    for g in range(KV_B):
        k_raw = kc[:, g * HD:(g + 1) * HD]
        k_g = k_raw * cos + pltpu.roll(k_raw, HD // 2, 1) * sin
        v_g = kc[:, (KV_B + g) * HD:(KV_B + g + 1) * HD]
        qg = jnp.concatenate(
            [q_ref[0, :, (g * rep + r) * HD:(g * rep + r + 1) * HD] for r in range(rep)], axis=0)
        s = _dot_nt(qg, k_g) * SCALE
        s = jnp.where(c_okr, s, NEG)
        m = jnp.max(s, axis=1, keepdims=True)
        p = jnp.where(c_okr, jnp.exp(s - m), 0.0)
        den = jnp.sum(p, axis=1, keepdims=True)
        p = p / jnp.where(den > 0.0, den, 1.0)
        o = _dot(p, v_g)
        imp_c = p[0:tq]
        for r in range(rep):
            oc_ref[0, :, (g * rep + r) * HD:(g * rep + r + 1) * HD] = o[r * tq:(r + 1) * tq]
            if r > 0:
                imp_c = imp_c + p[r * tq:(r + 1) * tq]
        imp = imp_c[:, :half] + imp_c[:, half:]
        if nsp > half:
            imp = jnp.concatenate([imp, jnp.zeros((tq, nsp - half), F32)], axis=1)
        val = jnp.where(forced, jnp.inf, jnp.where(b_causal, imp, -jnp.inf))
        rank = jnp.zeros((tq, nsp), F32)
        for i in range(ns):
            col = val[:, i:i + 1]
            low = jnp.where(blk > i, 1.0, 0.0)
            rank = rank + jnp.where(col > val, 1.0, jnp.where(col == val, low, 0.0))
        chosen = jnp.where(rank < min(N_SEL, ns), 1.0, 0.0).astype(BF16)
        row_blk = lax.broadcasted_iota(I32, (nsp, chunk), 0)
        for c in range(s_pad // chunk):
            tok_b = c * chunk + lax.broadcasted_iota(I32, (nsp, chunk), 1)
            expand = jnp.where((tok_b >> sel_shift) == row_blk, 1.0, 0.0).astype(BF16)
            hit = jnp.dot(chosen, expand, preferred_element_type=F32)
            tok = c * chunk + lax.broadcasted_iota(I32, (tq, chunk), 1)
            bias_ref[0, g, :, c * chunk:(c + 1) * chunk] = jnp.where(
                tok <= qpos1, jnp.where(hit > 0.5, 0.0, NEG), NEG).astype(bias_ref.dtype)


def _nsa_cmp(qb, kc, ctab, *, s_len, s_pad, pos0, tq):
    b, t, qw = qb.shape
    nc = kc.shape[1]
    ns = -(-s_len // SEL_BLOCK)
    nsp = -(-ns // LANES) * LANES
    chunk = next(c for c in (512, 384, 256, 128) if s_pad % c == 0)
    dtype = BF16 if tq % 16 == 0 else F32
    body = functools.partial(_nsa_cmp_body, tq=tq, nc=nc, ns=ns, nsp=nsp, s_pad=s_pad,
                             chunk=chunk, pos0=pos0)
    return pl.pallas_call(
        body,
        grid=(b, t // tq),
        in_specs=[pl.BlockSpec((1, tq, qw), lambda bi, qi: (bi, qi, 0)),
                  pl.BlockSpec((1, nc, kc.shape[2]), lambda bi, qi: (bi, 0, 0)),
                  pl.BlockSpec((nc, HD), lambda bi, qi: (0, 0)),
                  pl.BlockSpec((nc, HD), lambda bi, qi: (0, 0))],
        out_specs=[pl.BlockSpec((1, tq, qw), lambda bi, qi: (bi, qi, 0)),
                   pl.BlockSpec((1, KV_B, tq, s_pad), lambda bi, qi: (bi, 0, qi, 0))],
        out_shape=[jax.ShapeDtypeStruct((b, t, qw), F32),
                   jax.ShapeDtypeStruct((b, KV_B, t, s_pad), dtype)],
        compiler_params=_params(("parallel", "parallel")),
        name="nsa_cmp",
    )(qb, kc, *ctab)


def _gather_pages_body(_, *refs, per_step, rows):
    o_ref = refs[per_step]
    for u in range(per_step):
        o_ref[0, u * rows:(u + 1) * rows, :] = refs[u][0]


def _gather_pages(pool, pt, off, per_step=PAGES_PER_STEP):
    b, n_pages = pt.shape
    _, rows, c = pool.shape
    per_step = _tile(n_pages, per_step, 1)

    def spec(u):
        return pl.BlockSpec((1, rows, c), lambda bi, i, p: (off + p[bi, i * per_step + u], 0, 0))

    return _call(functools.partial(_gather_pages_body, per_step=per_step, rows=rows),
                 grid=(b, n_pages // per_step), in_specs=[spec(u) for u in range(per_step)],
                 out_specs=pl.BlockSpec((1, per_step * rows, c), lambda bi, i, p: (bi, i, 0)),
                 out_shape=jax.ShapeDtypeStruct((b, n_pages * rows, c), F32),
                 sem=("parallel", "arbitrary"), name="gather_pages",
                 prefetch=pt)(*([pool] * per_step))


def _combine_body(oa_ref, oc_ref, os_ref, ow_ref, misc_ref, o_ref):
    o_ref[:, :H_A * HD] = oa_ref[...]
    gate = misc_ref[...]
    for h in range(H_B):
        c0 = MISC_G + 3 * h
        sl = slice(h * HD, (h + 1) * HD)
        o_ref[:, H_A * HD + h * HD:H_A * HD + (h + 1) * HD] = (
            gate[:, c0:c0 + 1] * oc_ref[:, sl] + gate[:, c0 + 1:c0 + 2] * os_ref[:, sl]
            + gate[:, c0 + 2:c0 + 3] * ow_ref[:, sl])


def _combine(o_a, o_c, o_s, o_w, misc):
    m = o_a.shape[0]
    tm = _tile(m, 512, 8)
    wa, wb = H_A * HD, H_B * HD
    return pl.pallas_call(
        _combine_body,
        grid=(m // tm,),
        in_specs=[pl.BlockSpec((tm, wa), lambda i: (i, 0))]
        + [pl.BlockSpec((tm, wb), lambda i: (i, 0))] * 3
        + [pl.BlockSpec((tm, LANES), lambda i: (i, 0))],
        out_specs=pl.BlockSpec((tm, wa + wb), lambda i: (i, 0)),
        out_shape=jax.ShapeDtypeStruct((m, wa + wb), F32),
        compiler_params=_params(("parallel",)),
        name="combine",
    )(o_a, o_c, o_s, o_w, misc)


def _sb_init(q_ref, q_scr, c_scr, acc_scr, tq):
    rep = H_C // KV_C
    c_scr[...] = jnp.zeros(c_scr.shape, F32)
    acc_scr[...] = jnp.zeros(acc_scr.shape, F32)
    for g in range(KV_C):
        for r in range(rep):
            h = g * rep + r
            q_scr[g, r * tq:(r + 1) * tq, :] = q_ref[0, :, h * HD:(h + 1) * HD].astype(MXU)


def _sb_block(ks, vs, before, q_scr, c_scr, acc_scr):
    rep = H_C // KV_C
    rows = q_scr.shape[1]
    nchunk = ks[0].shape[0] // LANES
    older = None if before is None else jnp.concatenate([before] * rep, axis=0) > 0.5
    zs, pieces = [], []
    for g in range(KV_C):
        z = _dot_nt(q_scr[g], ks[g]) * SCALE
        log_keep = -(jnp.maximum(z, 0.0) + jnp.log(1.0 + jnp.exp(-jnp.abs(z))))
        if older is not None:
            log_keep = jnp.where(older, log_keep, 0.0)
        zs.append(z)
        pieces += [log_keep[:, c * LANES:(c + 1) * LANES] for c in range(nchunk)]
    stacked = jnp.concatenate(pieces, axis=0)
    hi = stacked.astype(BF16)
    lo = (stacked - hi.astype(F32)).astype(BF16)
    r_i = lax.broadcasted_iota(I32, (LANES, LANES), 0)
    c_i = lax.broadcasted_iota(I32, (LANES, LANES), 1)
    later = jnp.where(r_i >= c_i, 1.0, 0.0).astype(BF16)
    incl = (jnp.dot(hi, later, preferred_element_type=F32)
            + jnp.dot(lo, later, preferred_element_type=F32))
    for g in range(KV_C):
        run = c_scr[g][:, :1]
        cols = [None] * nchunk
        for c in reversed(range(nchunk)):
            part = incl[(g * nchunk + c) * rows:(g * nchunk + c + 1) * rows]
            cols[c] = part + run
            run = run + part[:, :1]
        a = jnp.exp(zs[g] + jnp.concatenate(cols, axis=1))
        if older is not None:
            a = jnp.where(older, a, 0.0)
        acc_scr[g] = acc_scr[g] + _dot(a, vs[g])
        c_scr[g] = jnp.broadcast_to(run, c_scr.shape[1:])


def _sb_final(o_ref, acc_scr, tq):
    rep = H_C // KV_C
    for g in range(KV_C):
        for r in range(rep):
            h = g * rep + r
            o_ref[0, :, h * HD:(h + 1) * HD] = acc_scr[g][r * tq:(r + 1) * tq, :]


def _sb_scratch(tq):
    rows = (H_C // KV_C) * tq
    return [pltpu.VMEM((KV_C, rows, HD), MXU),
            pltpu.VMEM((KV_C, rows, LANES), F32),
            pltpu.VMEM((KV_C, rows, HD), F32)]


def _split_kv(kv):
    ks = [kv[:, g * HD:(g + 1) * HD] for g in range(KV_C)]
    vs = [kv[:, (KV_C + g) * HD:(KV_C + g + 1) * HD] for g in range(KV_C)]
    return ks, vs


def _older_mask(tq, tk, qpos0, kpos0):
    qpos = qpos0 + lax.broadcasted_iota(I32, (tq, tk), 0)
    kpos = kpos0 + lax.broadcasted_iota(I32, (tq, tk), 1)
    return jnp.where(kpos < qpos, 1.0, 0.0)


def _sb_body(q_ref, kv_ref, o_ref, q_scr, c_scr, acc_scr, *, tq, tk, nj, pos0):
    qb = pl.program_id(1)
    j = pl.program_id(2)
    kb = (pos0 + (qb + 1) * tq - 1) // tk - j
    q0 = pos0 + qb * tq

    @pl.when(j == 0)
    def _():
        _sb_init(q_ref, q_scr, c_scr, acc_scr, tq)

    @pl.when((kb >= 0) & ((kb + 1) * tk > q0))
    def _():
        ks, vs = _split_kv(kv_ref[0])
        _sb_block(ks, vs, _older_mask(tq, tk, q0, kb * tk), q_scr, c_scr, acc_scr)

    @pl.when((kb >= 0) & ((kb + 1) * tk <= q0))
    def _():
        ks, vs = _split_kv(kv_ref[0])
        _sb_block(ks, vs, None, q_scr, c_scr, acc_scr)

    @pl.when(j == nj - 1)
    def _():
        _sb_final(o_ref, acc_scr, tq)


def _sb_attn(q, kv, *, tq, tk, pos0):
    b, t, qw = q.shape
    cw = kv.shape[2]
    nj = kv.shape[1] // tk
    body = functools.partial(_sb_body, tq=tq, tk=tk, nj=nj, pos0=pos0)
    return _call(
        body, grid=(b, t // tq, nj),
        in_specs=[pl.BlockSpec((1, tq, qw), lambda bi, qb, j: (bi, qb, 0)),
                  pl.BlockSpec((1, tk, cw), lambda bi, qb, j: (
                      bi, jnp.maximum((pos0 + (qb + 1) * tq - 1) // tk - j, 0), 0))],
        out_specs=pl.BlockSpec((1, tq, qw), lambda bi, qb, j: (bi, qb, 0)),
        out_shape=jax.ShapeDtypeStruct((b, t, qw), F32), scratch=_sb_scratch(tq),
        sem=("parallel", "parallel", "arbitrary"), name="sb_attn")(q, kv)


def _sb_paged_body(_, q_ref, *refs, tq, page, n_pg, n_main, pos0):
    pages = refs[:n_pg]
    tail_ref, o_ref, q_scr, c_scr, acc_scr = refs[n_pg:]
    j = pl.program_id(2)
    rpt = 2 * KV_C

    @pl.when(j == 0)
    def _():
        _sb_init(q_ref, q_scr, c_scr, acc_scr, tq)
        ks, vs = _split_kv(tail_ref[0])
        _sb_block(ks, vs, _older_mask(tq, page, pos0, n_main * n_pg * page), q_scr, c_scr, acc_scr)

    @pl.when(j > 0)
    def _():
        ks = [jnp.concatenate([pg[pl.ds(g, page, stride=rpt), :] for pg in pages], axis=0)
              for g in range(KV_C)]
        vs = [jnp.concatenate([pg[pl.ds(KV_C + g, page, stride=rpt), :] for pg in pages], axis=0)
              for g in range(KV_C)]
        _sb_block(ks, vs, None, q_scr, c_scr, acc_scr)

    @pl.when(j == n_main)
    def _():
        _sb_final(o_ref, acc_scr, tq)


def _sb_attn_paged(q, pool, pt, off, tail, *, page, pos0, pages_per_step=8):
    b, t, qw = q.shape
    n_pages = pt.shape[1]
    n_pg = _tile(n_pages, pages_per_step, 1)
    n_main = n_pages // n_pg
    rpt = 2 * KV_C
    ix = lambda f: (lambda bi, qb, j, p: f(bi, qb, j))
    page_specs = [pl.BlockSpec(
        (page * rpt, LANES),
        functools.partial(lambda u, bi, qb, j, p:
                          (off + p[bi, jnp.clip(n_main - j, 0, n_main - 1) * n_pg + u], 0), u))
        for u in range(n_pg)]
    in_specs = ([pl.BlockSpec((1, t, qw), ix(lambda bi, qb, j: (bi, 0, 0)))] + page_specs
                + [pl.BlockSpec((1, page, tail.shape[2]), ix(lambda bi, qb, j: (bi, 0, 0)))])
    body = functools.partial(_sb_paged_body, tq=t, page=page, n_pg=n_pg, n_main=n_main, pos0=pos0)
    return _call(body, grid=(b, 1, n_main + 1), in_specs=in_specs,
                 out_specs=pl.BlockSpec((1, t, qw), ix(lambda bi, qb, j: (bi, 0, 0))),
                 out_shape=jax.ShapeDtypeStruct((b, t, qw), F32), scratch=_sb_scratch(t),
                 sem=("parallel", "parallel", "arbitrary"), name="sb_attn_paged",
                 prefetch=pt)(q, *([pool] * n_pg), tail)


def _mem_body(x_ref, kv_ref, wq_ref, wo_ref, g_ref, b_ref, o_ref):
    x = x_ref[0]
    kv = kv_ref[0]
    q = _dot(x, wq_ref[...])
    outs = []
    for h in range(MEM_HEADS):
        s = _dot_nt(q[:, h * HD:(h + 1) * HD], kv[:, h * HD:(h + 1) * HD]) * SCALE
        p = jnp.exp(s - jnp.max(s, axis=1, keepdims=True))
        p = p / jnp.sum(p, axis=1, keepdims=True)
        outs.append(_dot(p, kv[:, (MEM_HEADS + h) * HD:(MEM_HEADS + h + 1) * HD]))
    y = _dot(jnp.concatenate(outs, axis=1), wo_ref[...])
    o_ref[0] = _ln(ALPHA * x + y, g_ref[...], b_ref[...])


def _mem_layer(x, mem_kv, wq, wo, g, b, *, tq):
    bsz, t, d = x.shape
    nm, cw = mem_kv.shape[1:]
    return pl.pallas_call(
        _mem_body,
        grid=(bsz, t // tq),
        in_specs=[pl.BlockSpec((1, tq, d), lambda bi, qi: (bi, qi, 0)),
                  pl.BlockSpec((1, nm, cw), lambda bi, qi: (bi, 0, 0)),
                  pl.BlockSpec(wq.shape, lambda bi, qi: (0, 0)),
                  pl.BlockSpec(wo.shape, lambda bi, qi: (0, 0)),
                  pl.BlockSpec((1, d), lambda bi, qi: (0, 0)),
                  pl.BlockSpec((1, d), lambda bi, qi: (0, 0))],
        out_specs=pl.BlockSpec((1, tq, d), lambda bi, qi: (bi, qi, 0)),
        out_shape=jax.ShapeDtypeStruct((bsz, t, d), F32),
        compiler_params=_params(("parallel", "parallel")),
        name="mem_layer",
    )(x, mem_kv, wq, wo, g.reshape(1, d), b.reshape(1, d))


def _ffn_body(x_ref, wg_ref, wu_ref, wd_ref, g_ref, b_ref, o_ref, xb_ref, acc_ref, *, nf):
    f = pl.program_id(1)

    @pl.when(f == 0)
    def _():
        xb_ref[...] = x_ref[...].astype(MXU)
        acc_ref[...] = jnp.zeros(acc_ref.shape, F32)

    xb = xb_ref[...]
    gate = jnp.dot(xb, wg_ref[...], preferred_element_type=F32)
    up = jnp.dot(xb, wu_ref[...], preferred_element_type=F32)
    hidden = gate / (1.0 + jnp.exp(-gate)) * up
    acc_ref[...] += _dot(hidden, wd_ref[...])

    @pl.when(f == nf - 1)
    def _():
        o_ref[...] = _ln(ALPHA * x_ref[...] + acc_ref[...], g_ref[...], b_ref[...])


def _ffn_layer(x, w_gu, w_d, g, b, *, tm=512, tf=512):
    m, d = x.shape
    dff = w_d.shape[0]
    tm = _tile(m, tm, 8)
    tf = _tile(dff, tf, LANES)
    nf = dff // tf
    return pl.pallas_call(
        functools.partial(_ffn_body, nf=nf),
        grid=(m // tm, nf),
        in_specs=[pl.BlockSpec((tm, d), lambda i, f: (i, 0)),
                  pl.BlockSpec((d, tf), lambda i, f: (0, f)),
                  pl.BlockSpec((d, tf), lambda i, f: (0, nf + f)),
                  pl.BlockSpec((tf, d), lambda i, f: (f, 0)),
                  pl.BlockSpec((1, d), lambda i, f: (0, 0)),
                  pl.BlockSpec((1, d), lambda i, f: (0, 0))],
        out_specs=pl.BlockSpec((tm, d), lambda i, f: (i, 0)),
        out_shape=jax.ShapeDtypeStruct((m, d), F32),
        scratch_shapes=[pltpu.VMEM((tm, d), MXU), pltpu.VMEM((tm, d), F32)],
        compiler_params=_params(("parallel", "arbitrary")),
        name="ffn_layer",
    )(x, w_gu, w_gu, w_d, g.reshape(1, d), b.reshape(1, d))


def _even_odd_rows(a):
    b, n2, c = a.shape
    return a.reshape(b, n2 // 2, 2, c).transpose(0, 2, 1, 3).reshape(b, n2, c)


def _pad_rows(a, rows):
    return jnp.pad(a, ((0, 0), (0, rows - a.shape[1]), (0, 0)))


def _ab_mixer(x, w, layer, past, *, pos0):
    b, t, d = x.shape
    m = b * t
    pos = pos0 + jnp.arange(t, dtype=I32)
    tabs = _rope_tables(pos, HD) + _rope_tables(pos, D_IDX)
    h = _mm(x.reshape(m, d), w["in_ab"][layer], tn=640)
    q_a, kv_a, q_i, q_b, cmp_kv, sel_kv, win_kv, misc = _post_ab(h, tabs, t)
    r3 = lambda a: a.reshape(b, t, a.shape[1])
    q_a, kv_a, q_i, q_b, cmp_kv, sel_kv, win_kv, misc = map(
        r3, (q_a, kv_a, q_i, q_b, cmp_kv, sel_kv, win_kv, misc))
    cw = 2 * KV_B * HD
    rep_a, rep_b = H_A // KV_A, H_B // KV_B
    if past is None:
        s_len = t
        tq = _tile(t, 256, 8)
        tk = _tile(t, 512, LANES)
        topk = min(TOPK_MAX, s_len // 4)
        bias_a = _dsa_index(q_i, misc, misc, topk=topk, pos0=0, tq=tq, tk=tk)
        o_a = _attn(q_a, kv_a, groups=KV_A, rep=rep_a, tq=tq, tk=tk, pos0=0, bias=bias_a)
        nc = s_len // CMP_BLOCK
        kc = _mm(cmp_kv.reshape(m // CMP_BLOCK, CMP_BLOCK * cw), w["phi"][layer], tn=cw)
        kc = _even_odd_rows(kc.reshape(b, nc, cw))
        s_pad = s_len
        o_w = _attn(q_b, win_kv, groups=KV_B, rep=rep_b, tq=tq, tk=tq, pos0=0)
        sel_attn = lambda bias: _attn(q_b, sel_kv, groups=KV_B, rep=rep_b, tq=tq, tk=tk,
                                      pos0=0, bias=bias)
        win_state = win_kv[:, t - min(WINDOW, t):]
    else:
        pt, caches, win_prev = past
        n_pages = pt.shape[1]
        page = caches["page"]
        n_pool = caches["n_pool"]
        off = layer * n_pool
        s_len = n_pages * page + t
        s_pad = (n_pages + 1) * page
        tq = t
        topk = min(TOPK_MAX, s_len // 4)
        bias_a = _dsa_index(q_i, misc, caches["a_kidx"], topk=topk, pos0=pos0, tq=tq, tk=page,
                            page=(pt, off, _pad_rows(misc[:, :, :D_IDX], page)))
        o_a = _attn_paged(q_a, caches["a_kv"], pt, off, _pad_rows(kv_a, page), bias_a,
                          groups=KV_A, rep=rep_a, page=page)
        nc = s_len // CMP_BLOCK
        per_page = page // CMP_BLOCK
        assert nc == n_pages * per_page, "new rows must not complete a compressed block"
        kc_pool = _mm(caches["cmp"], w["phi"][layer], tn=cw,
                      rows=(off * per_page, n_pool * per_page))
        kc = _gather_pages(kc_pool.reshape(n_pool, per_page, cw), pt, 0)
        kc = _even_odd_rows(kc)
        win_all = jnp.concatenate([win_prev, win_kv], axis=1)
        k_start = n_pages * page - win_prev.shape[1]
        o_w = _attn(q_b, _pad_rows(win_all, -(-win_all.shape[1] // page) * page),
                    groups=KV_B, rep=rep_b, tq=tq, tk=page, pos0=pos0, k_start=k_start)
        sel_attn = lambda bias: _attn_paged(q_b, caches["sel"], pt, off, _pad_rows(sel_kv, page),
                                            bias, groups=KV_B, rep=rep_b, page=page)
        win_state = win_all[:, win_all.shape[1] - win_prev.shape[1]:]
    assert nc % 2 == 0
    c_pos = (jnp.arange(nc, dtype=I32) + 1) * CMP_BLOCK - 1
    ctab = tuple(_even_odd_rows(tb[None])[0] for tb in _rope_tables(c_pos, HD))
    o_c, bias_s = _nsa_cmp(q_b, kc, ctab, s_len=s_len, s_pad=s_pad, pos0=pos0, tq=tq)
    o_s = sel_attn(bias_s)
    f2 = lambda a: a.reshape(m, a.shape[2])
    o = _combine(f2(o_a), f2(o_c), f2(o_s), f2(o_w), f2(misc))
    states = (kv_a, misc[:, :, :D_IDX], cmp_kv, sel_kv, win_state)
    return o, states


def _sb_mixer(x, w, layer, past, *, pos0):
    b, t, d = x.shape
    m = b * t
    x2 = x.reshape(m, d)
    q = _mm(x2, w["in_c_q"][layer]).reshape(b, t, H_C * HD)
    kv = _mm(x2, w["in_c_kv"][layer]).reshape(b, t, 2 * KV_C * HD)
    if past is None:
        tq = _tile(t, 256, 8)
        o = _sb_attn(q, kv, tq=tq, tk=tq, pos0=0)
    else:
        pt, caches = past
        page = caches["page"]
        o = _sb_attn_paged(q, caches["c_kv"], pt, layer * caches["n_pool"], _pad_rows(kv, page),
                           page=page, pos0=pos0)
    return o.reshape(m, H_C * HD), kv


def _run_group(x, w, mem_kvs, pasts, ln_g, ln_b, *, pos0):
    b, t, d = x.shape
    m = b * t
    even, odd = [], []
    for i in range(DEPTH):
        if i % 2 == 0:
            mix, st = _ab_mixer(x, w, i // 2, pasts[i], pos0=pos0)
            even.append(st)
        else:
            mix, st = _sb_mixer(x, w, i // 2, pasts[i], pos0=pos0)
            odd.append(st)
        x2 = _mm_ln(mix, w["out"][i], x.reshape(m, d), ln_g[i, 0], ln_b[i, 0])
        x3 = _mem_layer(x2.reshape(b, t, d), mem_kvs[i], w["mem_q"][i], w["mem_o"][i],
                        ln_g[i, 1], ln_b[i, 1], tq=_tile(t, 256, 8))
        x = _ffn_layer(x3.reshape(m, d), w["gate_up"][i], w["down"][i],
                       ln_g[i, 2], ln_b[i, 2]).reshape(b, t, d)
    return x, even, odd


def _reorder_in_ab(w):
    sizes = (H_A * HD, 2 * KV_A * HD, H_IDX * D_IDX, D_IDX, H_IDX,
             H_B * HD, 2 * KV_B * HD, 2 * KV_B * HD, 2 * KV_B * HD, 3 * H_B)
    offs = np.concatenate([[0], np.cumsum(sizes)])
    piece = lambda i: w[..., offs[i]:offs[i + 1]]
    order = [0, 1, 2, 5, 6, 7, 8, 3, 4, 9]
    cols = [piece(i) for i in order]
    pad = C_END - int(offs[-1])
    cols.append(jnp.zeros(w.shape[:-1] + (pad,), w.dtype))
    return jnp.concatenate(cols, axis=-1)


def _phi_matrix(w_phi):
    phi = w_phi.reshape(2, CMP_BLOCK, HD, HD)
    nb = 2 * KV_B
    out = jnp.zeros((CMP_BLOCK, nb, HD, nb, HD), w_phi.dtype)
    for c in range(nb):
        out = out.at[:, c, :, c, :].set(phi[c // KV_B])
    return out.reshape(CMP_BLOCK * nb * HD, nb * HD)


def kernel(x_prompt, x_sample, cache_a_kv, cache_a_kidx, cache_b_cmp_kv, cache_b_sel_kv,
           state_b_win_kv, cache_c_kv, cache_mem_kv, page_table, mem_prompt, w_in_ab, w_cmp_phi,
           w_in_c, w_out, w_mem_q, w_mem_kv, w_mem_o, w_gate_up, w_down, ln_g, ln_b):
    b_p, t_p, d = x_prompt.shape
    b_s, t_s, _ = x_sample.shape
    n_even, n_pool, page = cache_a_kv.shape[:3]
    n_odd = cache_c_kv.shape[0]
    n_mem = mem_prompt.shape[1]
    past_len = page_table.shape[1] * page

    def per_layer(a):
        return [a[i].astype(MXU) for i in range(a.shape[0])]

    w = {
        "in_ab": [_reorder_in_ab(wl).astype(MXU) for wl in w_in_ab],
        "phi": [_phi_matrix(wl).astype(MXU) for wl in w_cmp_phi],
        "in_c_q": per_layer(w_in_c[:, :, :H_C * HD]),
        "in_c_kv": per_layer(w_in_c[:, :, H_C * HD:]),
        "out": per_layer(w_out),
        "mem_q": per_layer(w_mem_q),
        "mem_o": per_layer(w_mem_o),
        "gate_up": per_layer(w_gate_up),
        "down": per_layer(w_down),
    }
    w_mkv = per_layer(w_mem_kv)

    mem_flat = mem_prompt.reshape(b_p * n_mem, d)
    mem_p = [_mm(mem_flat, w_mkv[i]).reshape(b_p, n_mem, 2 * MEM_HEADS * HD) for i in range(DEPTH)]
    y_p, ev_p, od_p = _run_group(x_prompt, w, mem_p, [None] * DEPTH, ln_g, ln_b, pos0=0)

    cw_b = 2 * KV_B * HD
    caches = {
        "page": page, "n_pool": n_pool,
        "a_kv": cache_a_kv.reshape(-1, LANES),
        "a_kidx": cache_a_kidx.reshape(n_even * n_pool, page, D_IDX),
        "cmp": cache_b_cmp_kv.reshape(n_even * n_pool * (page // CMP_BLOCK), CMP_BLOCK * cw_b),
        "sel": cache_b_sel_kv.reshape(-1, LANES),
    }
    caches_c = {"page": page, "n_pool": n_pool, "c_kv": cache_c_kv.reshape(-1, LANES)}
    win_prev = state_b_win_kv.reshape(n_even, b_s, state_b_win_kv.shape[2], cw_b)
    pasts = []
    for i in range(DEPTH):
        if i % 2 == 0:
            pasts.append((page_table, caches, win_prev[i // 2]))
        else:
            pasts.append((page_table, caches_c))
    mem_s = [cache_mem_kv[i].reshape(b_s, n_mem, 2 * MEM_HEADS * HD) for i in range(DEPTH)]
    y_s, ev_s, od_s = _run_group(x_sample, w, mem_s, pasts, ln_g, ln_b, pos0=past_len)

    def pack(states, bsz):
        def kv5(a, g):
            return a.reshape(bsz, a.shape[1], 2, g, HD)
        return (jnp.stack([kv5(s[0], KV_A) for s in states]),
                jnp.stack([s[1] for s in states]),
                jnp.stack([kv5(s[2], KV_B) for s in states]),
                jnp.stack([kv5(s[3], KV_B) for s in states]),
                jnp.stack([kv5(s[4], KV_B) for s in states]))

    a_p, ki_p, cmp_p, sel_p, win_p = pack(ev_p, b_p)
    a_s, ki_s, cmp_s, sel_s, win_s = pack(ev_s, b_s)
    c_p = jnp.stack([s.reshape(b_p, t_p, 2, KV_C, HD) for s in od_p])
    c_s = jnp.stack([s.reshape(b_s, t_s, 2, KV_C, HD) for s in od_s])
    mem_out = jnp.stack([mk.reshape(b_p, n_mem, 2, MEM_HEADS, HD) for mk in mem_p])
    return (y_p, y_s, a_p, ki_p, cmp_p, sel_p, win_p, c_p, mem_out,
            a_s, ki_s, cmp_s, sel_s, win_s, c_s)
```

```python
import functools
import math

import jax
import jax.numpy as jnp
import numpy as np
from jax import lax
from jax.experimental import pallas as pl
from jax.experimental.pallas import tpu as pltpu

F32 = jnp.float32
BF16 = jnp.bfloat16
I32 = jnp.int32

HD = 128
LANES = 128
H_A, KV_A, H_IDX, D_IDX = 8, 2, 4, 64
H_B, KV_B = 8, 2
H_C, KV_C = 16, 4
MEM_HEADS = 4
TOPK_MAX = 256
CMP_BLOCK, SEL_BLOCK, N_SEL, WINDOW = 32, 64, 16, 512
ROPE_THETA = 10000.0
LN_EPS = 1e-5
DEPTH = 4
ALPHA = (2 * DEPTH) ** 0.25
SCALE = HD ** -0.5
QSCALE = SCALE * math.log2(math.e)
NEG = -1e30
INT_MIN = -2 ** 31
VMEM_LIMIT = 56 * 1024 * 1024
PAGES_PER_STEP = 16

C_QA, C_KVA, C_QI, C_QB, C_CMP, C_SEL, C_WIN, C_MISC, C_END = (
    0, 1024, 1536, 1792, 2816, 3328, 3840, 4352, 4480)
MISC_WI = D_IDX
MISC_G = D_IDX + H_IDX


def _params(sem):
    return pltpu.CompilerParams(dimension_semantics=sem, vmem_limit_bytes=VMEM_LIMIT)


def _tile(n, pref, align):
    t = (min(pref, n) // align) * align
    while t >= align:
        if n % t == 0:
            return t
        t -= align
    return n


def _ln(y, g, b):
    mu = jnp.mean(y, axis=-1, keepdims=True)
    d = y - mu
    var = jnp.mean(d * d, axis=-1, keepdims=True)
    return d * lax.rsqrt(var + LN_EPS) * g + b


MXU = BF16


def _dot(a, b):
    return jnp.dot(a.astype(MXU), b.astype(MXU), preferred_element_type=F32)


def _dot_nt(a, b):
    return lax.dot_general(a.astype(MXU), b.astype(MXU), (((1,), (1,)), ((), ())),
                           preferred_element_type=F32)


def _call(body, *, grid, in_specs, out_specs, out_shape, scratch=(), sem, name, prefetch=None):
    if prefetch is None:
        return pl.pallas_call(body, grid=grid, in_specs=in_specs, out_specs=out_specs,
                              out_shape=out_shape, scratch_shapes=list(scratch),
                              compiler_params=_params(sem), name=name)
    gs = pltpu.PrefetchScalarGridSpec(num_scalar_prefetch=1, grid=grid, in_specs=in_specs,
                                      out_specs=out_specs, scratch_shapes=list(scratch))
    return functools.partial(
        pl.pallas_call(body, grid_spec=gs, out_shape=out_shape, compiler_params=_params(sem),
                       name=name), prefetch)


def _mm_body(x_ref, w_ref, o_ref, acc_ref, *, nk):
    part = _dot(x_ref[...], w_ref[...])
    if nk == 1:
        o_ref[...] = part
        return
    k = pl.program_id(2)

    @pl.when(k == 0)
    def _():
        acc_ref[...] = part

    @pl.when(k > 0)
    def _():
        acc_ref[...] += part

    @pl.when(k == nk - 1)
    def _():
        o_ref[...] = acc_ref[...]


def _mm(x, w, *, tm=1024, tn=512, tk=2048, rows=None):
    row0, m = rows if rows is not None else (0, x.shape[0])
    kd = x.shape[1]
    n = w.shape[1]
    tm = _tile(math.gcd(m, row0), tm, 8)
    tn = _tile(n, tn, LANES)
    tk = _tile(kd, tk, LANES)
    nk = kd // tk
    blk0 = row0 // tm
    return pl.pallas_call(
        functools.partial(_mm_body, nk=nk),
        grid=(m // tm, n // tn, nk),
        in_specs=[pl.BlockSpec((tm, tk), lambda i, j, k: (blk0 + i, k)),
                  pl.BlockSpec((tk, tn), lambda i, j, k: (k, j))],
        out_specs=pl.BlockSpec((tm, tn), lambda i, j, k: (i, j)),
        out_shape=jax.ShapeDtypeStruct((m, n), F32),
        scratch_shapes=[pltpu.VMEM((tm, tn), F32)],
        compiler_params=_params(("parallel", "parallel", "arbitrary")),
        name="mm",
    )(x, w)


def _mm_ln_body(x_ref, w_ref, r_ref, g_ref, b_ref, o_ref, acc_ref, *, nk):
    k = pl.program_id(1)
    part = _dot(x_ref[...], w_ref[...])

    @pl.when(k == 0)
    def _():
        acc_ref[...] = part

    @pl.when(k > 0)
    def _():
        acc_ref[...] += part

    @pl.when(k == nk - 1)
    def _():
        o_ref[...] = _ln(ALPHA * r_ref[...] + acc_ref[...], g_ref[...], b_ref[...])


def _mm_ln(x, w, res, g, b, *, tm=512, tk=512):
    m, kd = x.shape
    d = w.shape[1]
    tm = _tile(m, tm, 8)
    tk = _tile(kd, tk, LANES)
    nk = kd // tk
    return pl.pallas_call(
        functools.partial(_mm_ln_body, nk=nk),
        grid=(m // tm, nk),
        in_specs=[pl.BlockSpec((tm, tk), lambda i, k: (i, k)),
                  pl.BlockSpec((tk, d), lambda i, k: (k, 0)),
                  pl.BlockSpec((tm, d), lambda i, k: (i, 0)),
                  pl.BlockSpec((1, d), lambda i, k: (0, 0)),
                  pl.BlockSpec((1, d), lambda i, k: (0, 0))],
        out_specs=pl.BlockSpec((tm, d), lambda i, k: (i, 0)),
        out_shape=jax.ShapeDtypeStruct((m, d), F32),
        scratch_shapes=[pltpu.VMEM((tm, d), F32)],
        compiler_params=_params(("parallel", "arbitrary")),
        name="mm_ln",
    )(x, w, res, g.reshape(1, d), b.reshape(1, d))


def _post_ab_body(h_ref, c128_ref, s128_ref, c64_ref, s64_ref,
                  qa_ref, kva_ref, qi_ref, qb_ref, cmp_ref, sel_ref, win_ref, misc_ref):
    cos = c128_ref[...]
    sin = s128_ref[...]
    ci = c64_ref[...]
    si = s64_ref[...]
    lane = lax.broadcasted_iota(I32, cos.shape, 1)
    first_half = (lane & (D_IDX - 1)) < (D_IDX // 2)

    def rope128(x):
        return x * cos + pltpu.roll(x, HD // 2, 1) * sin

    def rope64(x):
        partner = jnp.where(first_half, pltpu.roll(x, LANES - D_IDX // 2, 1),
                            pltpu.roll(x, D_IDX // 2, 1))
        return x * ci + partner * si

    def head(c0, i):
        return h_ref[:, c0 + i * HD:c0 + (i + 1) * HD]

    for i in range(H_A):
        qa_ref[:, i * HD:(i + 1) * HD] = rope128(head(C_QA, i))
    for i in range(H_B):
        qb_ref[:, i * HD:(i + 1) * HD] = rope128(head(C_QB, i))
    for src, dst, kv in ((C_KVA, kva_ref, KV_A), (C_SEL, sel_ref, KV_B), (C_WIN, win_ref, KV_B)):
        for i in range(kv):
            dst[:, i * HD:(i + 1) * HD] = rope128(head(src, i))
            dst[:, (kv + i) * HD:(kv + i + 1) * HD] = head(src, kv + i)
    cmp_ref[...] = h_ref[:, C_CMP:C_SEL]
    for i in range(H_IDX * D_IDX // LANES):
        qi_ref[:, i * LANES:(i + 1) * LANES] = rope64(head(C_QI, i))
    raw = h_ref[:, C_MISC:C_END]
    misc_ref[...] = jnp.where(
        lane < MISC_WI, rope64(raw),
        jnp.where(lane < MISC_G, raw * (H_IDX ** -0.5),
                  jnp.where(lane < MISC_G + 3 * H_B, 1.0 / (1.0 + jnp.exp(-raw)), 0.0)))


def _post_ab(h, tabs, t):
    m = h.shape[0]
    tm = _tile(t, 256, 8)
    nt = t // tm
    widths = (H_A * HD, 2 * KV_A * HD, H_IDX * D_IDX, H_B * HD, 2 * KV_B * HD,
              2 * KV_B * HD, 2 * KV_B * HD, LANES)
    tab_spec = pl.BlockSpec((tm, LANES), lambda i: (i % nt, 0))
    return pl.pallas_call(
        _post_ab_body,
        grid=(m // tm,),
        in_specs=[pl.BlockSpec((tm, C_END), lambda i: (i, 0))] + [tab_spec] * 4,
        out_specs=[pl.BlockSpec((tm, w), lambda i: (i, 0)) for w in widths],
        out_shape=[jax.ShapeDtypeStruct((m, w), F32) for w in widths],
        compiler_params=_params(("parallel",)),
        name="post_ab",
    )(h, *tabs)


def _rope_tables(pos, d):
    half = d // 2
    inv = ROPE_THETA ** (-(jnp.arange(half, dtype=F32) * 2.0 / d))
    ang = pos.astype(F32)[:, None] * inv[None, :]
    cos, sin = jnp.cos(ang), jnp.sin(ang)
    reps = LANES // d
    return (jnp.tile(jnp.concatenate([cos, cos], axis=1), (1, reps)),
            jnp.tile(jnp.concatenate([-sin, sin], axis=1), (1, reps)))


def _sortable(x):
    bits = lax.bitcast_convert_type(jnp.where(x == 0.0, 0.0, x), I32)
    return bits ^ ((bits >> 31) & 0x7FFFFFFF)


def _dsa_index_body(*refs, tq, wk, n_kx, n_main, tail_w, topk, pos0, n_bucket):
    refs = list(refs)
    if tail_w:
        refs.pop(0)
    qi_ref, wi_ref = refs[0], refs[1]
    kx_refs = refs[2:2 + n_kx]
    i = 2 + n_kx
    tail_ref = None
    if tail_w:
        tail_ref = refs[i]
        i += 1
    bias_ref, key_ref, kth_ref, need_ref = refs[i:i + 4]
    qb = pl.program_id(1)
    j = pl.program_id(2)
    nch = key_ref.shape[0]
    nj = n_main + (1 if tail_w else 0)
    last_q = pos0 + (qb + 1) * tq - 1

    def put_keys(kx, chunk0, width, kpos0):
        qi = qi_ref[0]
        wi = wi_ref[0]
        score = jnp.zeros((tq, width), F32)
        for h in range(H_IDX):
            dots = _dot_nt(qi[:, h * D_IDX:(h + 1) * D_IDX], kx) * (D_IDX ** -0.5)
            score = score + wi[:, MISC_WI + h:MISC_WI + h + 1] * jnp.maximum(dots, 0.0)
        qpos = pos0 + qb * tq + lax.broadcasted_iota(I32, (tq, width), 0)
        kpos = kpos0 + lax.broadcasted_iota(I32, (tq, width), 1)
        keys = jnp.where(kpos <= qpos, _sortable(score), INT_MIN)
        for u in range(width // LANES):
            key_ref[chunk0 + u] = keys[:, u * LANES:(u + 1) * LANES]

    def main_step():
        kx = jnp.concatenate([r[0][:, :D_IDX] for r in kx_refs], axis=0)
        put_keys(kx, j * (wk // LANES), wk, j * wk)

    if tail_w:
        pl.when(j < n_main)(main_step)

        @pl.when(j == n_main)
        def _():
            put_keys(tail_ref[0][:, :D_IDX], n_main * (wk // LANES), tail_w, n_main * wk)
    else:
        pl.when(j * wk <= last_q)(main_step)

        @pl.when(j * wk > last_q)
        def _():
            for u in range(wk // LANES):
                key_ref[j * (wk // LANES) + u] = jnp.full((tq, LANES), INT_MIN, I32)

    def search(lim):
        n_part = max(1, min(4, tq // 64))
        pr = tq // n_part

        def count(pred, p):
            hit = pred(key_ref[0:lim, p * pr:(p + 1) * pr, :]).astype(I32)
            return jnp.sum(jnp.sum(hit, axis=0), axis=1, keepdims=True)

        ans0 = tuple(jnp.where(count(lambda k: k >= 0, p) >= topk, 0, INT_MIN)
                     for p in range(n_part))

        def bit_step(b, ans):
            bit = jnp.left_shift(jnp.int32(1), 30 - b)
            out = []
            for p in range(n_part):
                cand = ans[p] | bit
                out.append(jnp.where(count(lambda k: k >= cand, p) >= topk, cand, ans[p]))
            return tuple(out)

        kth = lax.fori_loop(0, 31, bit_step, ans0)
        for p in range(n_part):
            rows = slice(p * pr, (p + 1) * pr)
            kth_ref[rows, :] = jnp.broadcast_to(kth[p], (pr, LANES))
            need_ref[rows, :] = jnp.broadcast_to(
                (topk - count(lambda k: k > kth[p], p)).astype(F32), (pr, LANES))

    @pl.when(j == nj - 1)
    def _():
        n_live = last_q // LANES + 1
        lims = [nch * (i + 1) // n_bucket for i in range(n_bucket)]
        for i, lim in enumerate(lims):
            lo = lims[i - 1] if i else 0
            pl.when((n_live > lo) & ((n_live <= lim) | (i == n_bucket - 1)))(
                functools.partial(search, lim))
        kth = kth_ref[...]
        need = need_ref[...]
        r_i = lax.broadcasted_iota(I32, (LANES, LANES), 0)
        c_i = lax.broadcasted_iota(I32, (LANES, LANES), 1)
        strict = jnp.where(r_i < c_i, 1.0, 0.0).astype(BF16)
        ones = jnp.ones((LANES, LANES), BF16)
        seen = jnp.zeros((tq, LANES), F32)
        for c in range(nch):
            kc = key_ref[c]
            tie = jnp.where(kc == kth, 1.0, 0.0).astype(BF16)
            rank = seen + jnp.dot(tie, strict, preferred_element_type=F32)
            tie_bias = jnp.where(kc == kth, jnp.where(rank < need, 0.0, NEG), NEG)
            take = jnp.where(kc > kth, 0.0, tie_bias)
            bias_ref[0, 0, :, c * LANES:(c + 1) * LANES] = (
                jnp.where(kc > INT_MIN, take, NEG).astype(bias_ref.dtype))
            seen = seen + jnp.dot(tie, ones, preferred_element_type=F32)


def _dsa_index(qi, misc, kx, *, topk, pos0, tq, tk, page=None):
    b, t, _ = qi.shape
    nq = t // tq
    if page is not None:
        pt, off, tail = page
        n_pages = pt.shape[1]
        n_kx = _tile(n_pages, PAGES_PER_STEP, 1)
        n_main = n_pages // n_kx
        tail_w = tk
        wk = n_kx * tk
        ix = lambda f: (lambda bi, qb, j, p: f(bi, qb, j))
        kx_specs = [pl.BlockSpec(
            (1, tk, kx.shape[2]),
            functools.partial(lambda u, bi, qb, j, p:
                              (off + p[bi, jnp.minimum(j, n_main - 1) * n_kx + u], 0, 0), u))
            for u in range(n_kx)]
        extra = [pl.BlockSpec((1, tk, tail.shape[2]), ix(lambda bi, qb, j: (bi, 0, 0)))]
        args = [qi, misc] + [kx] * n_kx + [tail]
        prefetch = pt
        s_pad = n_pages * tk + tail_w
    else:
        n_kx, n_main, tail_w, wk = 1, kx.shape[1] // tk, 0, tk
        ix = lambda f: f
        kx_specs = [pl.BlockSpec((1, tk, kx.shape[2]), lambda bi, qb, j: (bi, j, 0))]
        extra = []
        args = [qi, misc, kx]
        prefetch = None
        s_pad = kx.shape[1]
    nj = n_main + (1 if tail_w else 0)
    dtype = BF16 if tq % 16 == 0 else F32
    in_specs = [pl.BlockSpec((1, tq, qi.shape[2]), ix(lambda bi, qb, j: (bi, qb, 0))),
                pl.BlockSpec((1, tq, LANES), ix(lambda bi, qb, j: (bi, qb, 0)))] + kx_specs + extra
    body = functools.partial(_dsa_index_body, tq=tq, wk=wk, n_kx=n_kx, n_main=n_main,
                             tail_w=tail_w, topk=topk, pos0=pos0,
                             n_bucket=min(4, nq) if not tail_w else 1)
    return _call(body, grid=(b, nq, nj), in_specs=in_specs,
                 out_specs=pl.BlockSpec((1, 1, tq, s_pad), ix(lambda bi, qb, j: (bi, 0, qb, 0))),
                 out_shape=jax.ShapeDtypeStruct((b, 1, t, s_pad), dtype),
                 scratch=[pltpu.VMEM((s_pad // LANES, tq, LANES), I32),
                          pltpu.VMEM((tq, LANES), I32), pltpu.VMEM((tq, LANES), F32)],
                 sem=("parallel", "parallel", "arbitrary"), name="dsa_index",
                 prefetch=prefetch)(*args)


def _flash_init(q_ref, q_scr, m_scr, l_scr, acc_scr, groups, rep, tq):
    m_scr[...] = jnp.full(m_scr.shape, NEG, F32)
    l_scr[...] = jnp.zeros(l_scr.shape, F32)
    acc_scr[...] = jnp.zeros(acc_scr.shape, F32)
    for g in range(groups):
        for r in range(rep):
            h = g * rep + r
            q_scr[g, r * tq:(r + 1) * tq, :] = (q_ref[0, :, h * HD:(h + 1) * HD] * QSCALE).astype(MXU)


def _flash_step(g, k, v, bias, q_scr, m_scr, l_scr, acc_scr, rep):
    s = _dot_nt(q_scr[g], k) + jnp.concatenate([bias.astype(F32)] * rep, axis=0)
    m_prev = m_scr[g]
    m_new = jnp.maximum(m_prev, jnp.max(s, axis=1, keepdims=True))
    p = jnp.exp2(s - pltpu.repeat(m_new, k.shape[0] // LANES, axis=1))
    a = jnp.exp2(m_prev - m_new)
    l_scr[g] = a * l_scr[g] + jnp.sum(p, axis=1, keepdims=True)
    acc_scr[g] = a * acc_scr[g] + _dot(p, v)
    m_scr[g] = m_new


def _flash_final(o_ref, m_scr, l_scr, acc_scr, groups, rep, tq):
    for g in range(groups):
        for r in range(rep):
            h = g * rep + r
            rows = slice(r * tq, (r + 1) * tq)
            seen = m_scr[g][rows, :] > 0.5 * NEG
            o_ref[0, :, h * HD:(h + 1) * HD] = jnp.where(
                seen, acc_scr[g][rows, :] / l_scr[g][rows, :], 0.0)


def _flash_scratch(groups, rep, tq):
    rows = rep * tq
    return [pltpu.VMEM((groups, rows, HD), MXU),
            pltpu.VMEM((groups, rows, LANES), F32),
            pltpu.VMEM((groups, rows, LANES), F32),
            pltpu.VMEM((groups, rows, HD), F32)]


def _attn_body(*refs, groups, rep, tq, tk, nj, pos0, k_start, band, lead, gm):
    q_ref, kv_ref = refs[0], refs[1]
    bias_ref = None if band else refs[2]
    o_ref, q_scr, m_scr, l_scr, acc_scr = refs[2 if band else 3:]
    qb = pl.program_id(1)
    j = pl.program_id(2)
    kb = qb - lead + j if band else j

    @pl.when(j == 0)
    def _():
        _flash_init(q_ref, q_scr, m_scr, l_scr, acc_scr, groups, rep, tq)

    last_q = pos0 + qb * tq + tq - 1

    @pl.when((kb >= 0) & (k_start + kb * tk <= last_q))
    def _():
        kv = kv_ref[0]
        if band:
            qpos = pos0 + qb * tq + lax.broadcasted_iota(I32, (tq, tk), 0)
            kpos = k_start + kb * tk + lax.broadcasted_iota(I32, (tq, tk), 1)
            bias = jnp.where(kpos <= qpos, jnp.where(kpos > qpos - WINDOW, 0.0, NEG), NEG)
        for g in range(groups):
            if not band:
                bias = bias_ref[0, g if gm > 1 else 0]
            _flash_step(g, kv[:, g * HD:(g + 1) * HD],
                        kv[:, (groups + g) * HD:(groups + g + 1) * HD],
                        bias, q_scr, m_scr, l_scr, acc_scr, rep)

    @pl.when(j == nj - 1)
    def _():
        _flash_final(o_ref, m_scr, l_scr, acc_scr, groups, rep, tq)


def _attn(q, kv, *, groups, rep, tq, tk, pos0, bias=None, k_start=0):
    b, t, qw = q.shape
    band = bias is None
    cw = kv.shape[2]
    nq = t // tq
    lead = WINDOW // tk if (band and nq > 1) else 0
    nj = (lead + 1) if (band and nq > 1) else kv.shape[1] // tk

    def kblock(qb, j):
        if band:
            return jnp.maximum(qb - lead + j, 0)
        return jnp.minimum(j, (pos0 + (qb + 1) * tq - 1) // tk)

    in_specs = [pl.BlockSpec((1, tq, qw), lambda bi, qb, j: (bi, qb, 0)),
                pl.BlockSpec((1, tk, cw), lambda bi, qb, j: (bi, kblock(qb, j), 0))]
    args = [q, kv]
    gm = 1
    if not band:
        gm = bias.shape[1]
        in_specs.append(pl.BlockSpec((1, gm, tq, tk), lambda bi, qb, j: (bi, 0, qb, kblock(qb, j))))
        args.append(bias)
    body = functools.partial(_attn_body, groups=groups, rep=rep, tq=tq, tk=tk, nj=nj, pos0=pos0,
                             k_start=k_start, band=band, lead=lead, gm=gm)
    return _call(body, grid=(b, nq, nj), in_specs=in_specs,
                 out_specs=pl.BlockSpec((1, tq, qw), lambda bi, qb, j: (bi, qb, 0)),
                 out_shape=jax.ShapeDtypeStruct((b, t, qw), F32),
                 scratch=_flash_scratch(groups, rep, tq),
                 sem=("parallel", "parallel", "arbitrary"),
                 name="attn_band" if band else "attn_bias")(*args)


def _attn_paged_body(_, q_ref, *refs, groups, rep, tq, page, n_pg, n_main, gm):
    pages = refs[:n_pg]
    tail_ref, bias_ref, bias_tail_ref, o_ref, q_scr, m_scr, l_scr, acc_scr = refs[n_pg:]
    j = pl.program_id(2)
    rpt = 2 * groups

    @pl.when(j == 0)
    def _():
        _flash_init(q_ref, q_scr, m_scr, l_scr, acc_scr, groups, rep, tq)

    @pl.when(j < n_main)
    def _():
        for g in range(groups):
            k = jnp.concatenate([pg[pl.ds(g, page, stride=rpt), :] for pg in pages], axis=0)
            v = jnp.concatenate([pg[pl.ds(groups + g, page, stride=rpt), :] for pg in pages], axis=0)
            _flash_step(g, k, v, bias_ref[0, g if gm > 1 else 0], q_scr, m_scr, l_scr, acc_scr, rep)

    @pl.when(j == n_main)
    def _():
        kv = tail_ref[0]
        for g in range(groups):
            _flash_step(g, kv[:, g * HD:(g + 1) * HD],
                        kv[:, (groups + g) * HD:(groups + g + 1) * HD],
                        bias_tail_ref[0, g if gm > 1 else 0], q_scr, m_scr, l_scr, acc_scr, rep)
        _flash_final(o_ref, m_scr, l_scr, acc_scr, groups, rep, tq)


def _attn_paged(q, pool, pt, off, tail, bias, *, groups, rep, page):
    b, t, qw = q.shape
    n_pages = pt.shape[1]
    n_pg = _tile(n_pages, PAGES_PER_STEP, 1)
    n_main = n_pages // n_pg
    rpt = 2 * groups
    gm = bias.shape[1]
    ix = lambda f: (lambda bi, qb, j, p: f(bi, qb, j))
    page_specs = [pl.BlockSpec(
        (page * rpt, LANES),
        functools.partial(lambda u, bi, qb, j, p:
                          (off + p[bi, jnp.minimum(j, n_main - 1) * n_pg + u], 0), u))
        for u in range(n_pg)]
    in_specs = ([pl.BlockSpec((1, t, qw), ix(lambda bi, qb, j: (bi, 0, 0)))] + page_specs + [
        pl.BlockSpec((1, page, tail.shape[2]), ix(lambda bi, qb, j: (bi, 0, 0))),
        pl.BlockSpec((1, gm, t, n_pg * page),
                     ix(lambda bi, qb, j: (bi, 0, 0, jnp.minimum(j, n_main - 1)))),
        pl.BlockSpec((1, gm, t, page), ix(lambda bi, qb, j: (bi, 0, 0, n_pages)))])
    body = functools.partial(_attn_paged_body, groups=groups, rep=rep, tq=t, page=page,
                             n_pg=n_pg, n_main=n_main, gm=gm)
    return _call(body, grid=(b, 1, n_main + 1), in_specs=in_specs,
                 out_specs=pl.BlockSpec((1, t, qw), ix(lambda bi, qb, j: (bi, 0, 0))),
                 out_shape=jax.ShapeDtypeStruct((b, t, qw), F32),
                 scratch=_flash_scratch(groups, rep, t),
                 sem=("parallel", "parallel", "arbitrary"), name="attn_paged",
                 prefetch=pt)(q, *([pool] * n_pg), tail, bias, bias)


def _nsa_cmp_body(q_ref, kc_ref, cc_ref, cs_ref, oc_ref, bias_ref, *, tq, nc, ns, nsp, s_pad,
                  chunk, pos0):
    qb = pl.program_id(1)
    half = nc // 2
    rep = H_B // KV_B
    kc = kc_ref[0]
    cos = cc_ref[...]
    sin = cs_ref[...]
    qpos1 = pos0 + qb * tq + lax.broadcasted_iota(I32, (tq, 1), 0)
    n_perm = lax.broadcasted_iota(I32, (tq, nc), 1)
    n_orig = jnp.where(n_perm < half, 2 * n_perm, 2 * (n_perm - half) + 1)
    c_ok = jnp.where(((n_orig + 1) * CMP_BLOCK - 1) <= qpos1, 1.0, 0.0)
    c_okr = jnp.concatenate([c_ok] * rep, axis=0) > 0.5
    blk = lax.broadcasted_iota(I32, (tq, nsp), 1)
    sel_shift = SEL_BLOCK.bit_length() - 1
    cur = qpos1 >> sel_shift
    forced = (blk == 0) | (blk == cur) | (blk == cur - 1)
    b_causal = blk * SEL_BLOCK <= qpos1
    for g in range(KV_B):
        k_raw = kc[:, g * HD:(g + 1) * HD]
        k_g = k_raw * cos + pltpu.roll(k_raw, HD // 2, 1) * sin
        v_g = kc[:, (KV_B + g) * HD:(KV_B + g + 1) * HD]
        qg = jnp.concatenate(
            [q_ref[0, :, (g * rep + r) * HD:(g * rep + r + 1) * HD] for r in range(rep)], axis=0)
        s = _dot_nt(qg, k_g) * SCALE
        s = jnp.where(c_okr, s, NEG)
        m = jnp.max(s, axis=1, keepdims=True)
        p = jnp.where(c_okr, jnp.exp(s - m), 0.0)
        den = jnp.sum(p, axis=1, keepdims=True)
        p = p / jnp.where(den > 0.0, den, 1.0)
        o = _dot(p, v_g)
        imp_c = p[0:tq]
        for r in range(rep):
            oc_ref[0, :, (g * rep + r) * HD:(g * rep + r + 1) * HD] = o[r * tq:(r + 1) * tq]
            if r > 0:
                imp_c = imp_c + p[r * tq:(r + 1) * tq]
        imp = imp_c[:, :half] + imp_c[:, half:]
        if nsp > half:
            imp = jnp.concatenate([imp, jnp.zeros((tq, nsp - half), F32)], axis=1)
        val = jnp.where(forced, jnp.inf, jnp.where(b_causal, imp, -jnp.inf))
        rank = jnp.zeros((tq, nsp), F32)
        for i in range(ns):
            col = val[:, i:i + 1]
            low = jnp.where(blk > i, 1.0, 0.0)
            rank = rank + jnp.where(col > val, 1.0, jnp.where(col == val, low, 0.0))
        chosen = jnp.where(rank < min(N_SEL, ns), 1.0, 0.0).astype(BF16)
        row_blk = lax.broadcasted_iota(I32, (nsp, chunk), 0)
        for c in range(s_pad // chunk):
            tok_b = c * chunk + lax.broadcasted_iota(I32, (nsp, chunk), 1)
            expand = jnp.where((tok_b >> sel_shift) == row_blk, 1.0, 0.0).astype(BF16)
            hit = jnp.dot(chosen, expand, preferred_element_type=F32)
            tok = c * chunk + lax.broadcasted_iota(I32, (tq, chunk), 1)
            bias_ref[0, g, :, c * chunk:(c + 1) * chunk] = jnp.where(
                tok <= qpos1, jnp.where(hit > 0.5, 0.0, NEG), NEG).astype(bias_ref.dtype)


def _nsa_cmp(qb, kc, ctab, *, s_len, s_pad, pos0, tq):
    b, t, qw = qb.shape
    nc = kc.shape[1]
    ns = -(-s_len // SEL_BLOCK)
    nsp = -(-ns // LANES) * LANES
    chunk = next(c for c in (512, 384, 256, 128) if s_pad % c == 0)
    dtype = BF16 if tq % 16 == 0 else F32
    body = functools.partial(_nsa_cmp_body, tq=tq, nc=nc, ns=ns, nsp=nsp, s_pad=s_pad,
                             chunk=chunk, pos0=pos0)
    return pl.pallas_call(
        body,
        grid=(b, t // tq),
        in_specs=[pl.BlockSpec((1, tq, qw), lambda bi, qi: (bi, qi, 0)),
                  pl.BlockSpec((1, nc, kc.shape[2]), lambda bi, qi: (bi, 0, 0)),
                  pl.BlockSpec((nc, HD), lambda bi, qi: (0, 0)),
                  pl.BlockSpec((nc, HD), lambda bi, qi: (0, 0))],
        out_specs=[pl.BlockSpec((1, tq, qw), lambda bi, qi: (bi, qi, 0)),
                   pl.BlockSpec((1, KV_B, tq, s_pad), lambda bi, qi: (bi, 0, qi, 0))],
        out_shape=[jax.ShapeDtypeStruct((b, t, qw), F32),
                   jax.ShapeDtypeStruct((b, KV_B, t, s_pad), dtype)],
        compiler_params=_params(("parallel", "parallel")),
        name="nsa_cmp",
    )(qb, kc, *ctab)


def _cmp_rows_body(_, *refs, per_step, per_page, rpt):
    o_ref = refs[per_step]
    stride = CMP_BLOCK * rpt
    for i in range(stride):
        piece = jnp.concatenate([pg[pl.ds(i, per_page, stride=stride), :] for pg in refs[:per_step]],
                                axis=0)
        o_ref[0, :, i * LANES:(i + 1) * LANES] = piece


def _cmp_rows(pool, pt, off, *, page, rpt, per_step=PAGES_PER_STEP):
    b, n_pages = pt.shape
    per_step = _tile(n_pages, per_step, 1)
    per_page = page // CMP_BLOCK
    width = CMP_BLOCK * rpt * LANES

    def spec(u):
        return pl.BlockSpec((page * rpt, LANES),
                            lambda bi, i, p: (off + p[bi, i * per_step + u], 0))

    return _call(functools.partial(_cmp_rows_body, per_step=per_step, per_page=per_page, rpt=rpt),
                 grid=(b, n_pages // per_step), in_specs=[spec(u) for u in range(per_step)],
                 out_specs=pl.BlockSpec((1, per_step * per_page, width), lambda bi, i, p: (bi, i, 0)),
                 out_shape=jax.ShapeDtypeStruct((b, n_pages * per_page, width), F32),
                 sem=("parallel", "arbitrary"), name="cmp_rows",
                 prefetch=pt)(*([pool] * per_step))


def _combine_body(oa_ref, oc_ref, os_ref, ow_ref, misc_ref, o_ref):
    o_ref[:, :H_A * HD] = oa_ref[...]
    gate = misc_ref[...]
    for h in range(H_B):
        c0 = MISC_G + 3 * h
        sl = slice(h * HD, (h + 1) * HD)
        o_ref[:, H_A * HD + h * HD:H_A * HD + (h + 1) * HD] = (
            gate[:, c0:c0 + 1] * oc_ref[:, sl] + gate[:, c0 + 1:c0 + 2] * os_ref[:, sl]
            + gate[:, c0 + 2:c0 + 3] * ow_ref[:, sl])


def _combine(o_a, o_c, o_s, o_w, misc):
    m = o_a.shape[0]
    tm = _tile(m, 512, 8)
    wa, wb = H_A * HD, H_B * HD
    return pl.pallas_call(
        _combine_body,
        grid=(m // tm,),
        in_specs=[pl.BlockSpec((tm, wa), lambda i: (i, 0))]
        + [pl.BlockSpec((tm, wb), lambda i: (i, 0))] * 3
        + [pl.BlockSpec((tm, LANES), lambda i: (i, 0))],
        out_specs=pl.BlockSpec((tm, wa + wb), lambda i: (i, 0)),
        out_shape=jax.ShapeDtypeStruct((m, wa + wb), F32),
        compiler_params=_params(("parallel",)),
        name="combine",
    )(o_a, o_c, o_s, o_w, misc)


def _sb_init(q_ref, q_scr, c_scr, acc_scr, tq):
    rep = H_C // KV_C
    c_scr[...] = jnp.zeros(c_scr.shape, F32)
    acc_scr[...] = jnp.zeros(acc_scr.shape, F32)
    for g in range(KV_C):
        for r in range(rep):
            h = g * rep + r
            q_scr[g, r * tq:(r + 1) * tq, :] = (q_ref[0, :, h * HD:(h + 1) * HD] * SCALE).astype(MXU)


def _sb_block(ks, vs, before, q_scr, c_scr, acc_scr):
    rep = H_C // KV_C
    rows = q_scr.shape[1]
    nchunk = ks[0].shape[0] // LANES
    older = None if before is None else jnp.concatenate([before] * rep, axis=0) > 0.5
    zs, pieces = [], []
    for g in range(KV_C):
        z = _dot_nt(q_scr[g], ks[g])
        log_keep = -(jnp.maximum(z, 0.0) + jnp.log(1.0 + jnp.exp(-jnp.abs(z))))
        if older is not None:
            log_keep = jnp.where(older, log_keep, 0.0)
        zs.append(z)
        pieces += [log_keep[:, c * LANES:(c + 1) * LANES] for c in range(nchunk)]
    stacked = jnp.concatenate(pieces, axis=0)
    hi = stacked.astype(BF16)
    lo = (stacked - hi.astype(F32)).astype(BF16)
    r_i = lax.broadcasted_iota(I32, (LANES, LANES), 0)
    c_i = lax.broadcasted_iota(I32, (LANES, LANES), 1)
    later = jnp.where(r_i >= c_i, 1.0, 0.0).astype(BF16)
    incl = (jnp.dot(hi, later, preferred_element_type=F32)
            + jnp.dot(lo, later, preferred_element_type=F32))
    for g in range(KV_C):
        run = c_scr[g]
        cols = [None] * nchunk
        for c in reversed(range(nchunk)):
            part = incl[(g * nchunk + c) * rows:(g * nchunk + c + 1) * rows]
            cols[c] = part + run
            run = run + jnp.broadcast_to(part[:, :1], part.shape)
        a = jnp.exp(zs[g] + jnp.concatenate(cols, axis=1))
        if older is not None:
            a = jnp.where(older, a, 0.0)
        acc_scr[g] = acc_scr[g] + _dot(a, vs[g])
        c_scr[g] = run


def _sb_final(o_ref, acc_scr, tq):
    rep = H_C // KV_C
    for g in range(KV_C):
        for r in range(rep):
            h = g * rep + r
            o_ref[0, :, h * HD:(h + 1) * HD] = acc_scr[g][r * tq:(r + 1) * tq, :]


def _sb_scratch(tq):
    rows = (H_C // KV_C) * tq
    return [pltpu.VMEM((KV_C, rows, HD), MXU),
            pltpu.VMEM((KV_C, rows, LANES), F32),
            pltpu.VMEM((KV_C, rows, HD), F32)]


def _split_kv(kv):
    ks = [kv[:, g * HD:(g + 1) * HD] for g in range(KV_C)]
    vs = [kv[:, (KV_C + g) * HD:(KV_C + g + 1) * HD] for g in range(KV_C)]
    return ks, vs


def _older_mask(tq, tk, qpos0, kpos0):
    qpos = qpos0 + lax.broadcasted_iota(I32, (tq, tk), 0)
    kpos = kpos0 + lax.broadcasted_iota(I32, (tq, tk), 1)
    return jnp.where(kpos < qpos, 1.0, 0.0)


def _sb_body(q_ref, kv_ref, o_ref, q_scr, c_scr, acc_scr, *, tq, tk, nj, pos0):
    qb = pl.program_id(1)
    j = pl.program_id(2)
    kb = (pos0 + (qb + 1) * tq - 1) // tk - j
    q0 = pos0 + qb * tq

    @pl.when(j == 0)
    def _():
        _sb_init(q_ref, q_scr, c_scr, acc_scr, tq)

    @pl.when((kb >= 0) & ((kb + 1) * tk > q0))
    def _():
        ks, vs = _split_kv(kv_ref[0])
        _sb_block(ks, vs, _older_mask(tq, tk, q0, kb * tk), q_scr, c_scr, acc_scr)

    @pl.when((kb >= 0) & ((kb + 1) * tk <= q0))
    def _():
        ks, vs = _split_kv(kv_ref[0])
        _sb_block(ks, vs, None, q_scr, c_scr, acc_scr)

    @pl.when(j == nj - 1)
    def _():
        _sb_final(o_ref, acc_scr, tq)


def _sb_attn(q, kv, *, tq, tk, pos0):
    b, t, qw = q.shape
    cw = kv.shape[2]
    nj = kv.shape[1] // tk
    body = functools.partial(_sb_body, tq=tq, tk=tk, nj=nj, pos0=pos0)
    return _call(
        body, grid=(b, t // tq, nj),
        in_specs=[pl.BlockSpec((1, tq, qw), lambda bi, qb, j: (bi, qb, 0)),
                  pl.BlockSpec((1, tk, cw), lambda bi, qb, j: (
                      bi, jnp.maximum((pos0 + (qb + 1) * tq - 1) // tk - j, 0), 0))],
        out_specs=pl.BlockSpec((1, tq, qw), lambda bi, qb, j: (bi, qb, 0)),
        out_shape=jax.ShapeDtypeStruct((b, t, qw), F32), scratch=_sb_scratch(tq),
        sem=("parallel", "parallel", "arbitrary"), name="sb_attn")(q, kv)


def _sb_paged_body(_, q_ref, *refs, tq, page, n_pg, n_main, pos0):
    pages = refs[:n_pg]
    tail_ref, o_ref, q_scr, c_scr, acc_scr = refs[n_pg:]
    j = pl.program_id(2)
    rpt = 2 * KV_C

    @pl.when(j == 0)
    def _():
        _sb_init(q_ref, q_scr, c_scr, acc_scr, tq)
        ks, vs = _split_kv(tail_ref[0])
        _sb_block(ks, vs, _older_mask(tq, page, pos0, n_main * n_pg * page), q_scr, c_scr, acc_scr)

    @pl.when(j > 0)
    def _():
        ks = [jnp.concatenate([pg[pl.ds(g, page, stride=rpt), :] for pg in pages], axis=0)
              for g in range(KV_C)]
        vs = [jnp.concatenate([pg[pl.ds(KV_C + g, page, stride=rpt), :] for pg in pages], axis=0)
              for g in range(KV_C)]
        _sb_block(ks, vs, None, q_scr, c_scr, acc_scr)

    @pl.when(j == n_main)
    def _():
        _sb_final(o_ref, acc_scr, tq)


def _sb_attn_paged(q, pool, pt, off, tail, *, page, pos0, pages_per_step=8):
    b, t, qw = q.shape
    n_pages = pt.shape[1]
    n_pg = _tile(n_pages, pages_per_step, 1)
    n_main = n_pages // n_pg
    rpt = 2 * KV_C
    ix = lambda f: (lambda bi, qb, j, p: f(bi, qb, j))
    page_specs = [pl.BlockSpec(
        (page * rpt, LANES),
        functools.partial(lambda u, bi, qb, j, p:
                          (off + p[bi, jnp.clip(n_main - j, 0, n_main - 1) * n_pg + u], 0), u))
        for u in range(n_pg)]
    in_specs = ([pl.BlockSpec((1, t, qw), ix(lambda bi, qb, j: (bi, 0, 0)))] + page_specs
                + [pl.BlockSpec((1, page, tail.shape[2]), ix(lambda bi, qb, j: (bi, 0, 0)))])
    body = functools.partial(_sb_paged_body, tq=t, page=page, n_pg=n_pg, n_main=n_main, pos0=pos0)
    return _call(body, grid=(b, 1, n_main + 1), in_specs=in_specs,
                 out_specs=pl.BlockSpec((1, t, qw), ix(lambda bi, qb, j: (bi, 0, 0))),
                 out_shape=jax.ShapeDtypeStruct((b, t, qw), F32), scratch=_sb_scratch(t),
                 sem=("parallel", "parallel", "arbitrary"), name="sb_attn_paged",
                 prefetch=pt)(q, *([pool] * n_pg), tail)


def _mem_body(x_ref, kv_ref, wq_ref, wo_ref, g_ref, b_ref, o_ref):
    x = x_ref[0]
    kv = kv_ref[0]
    q = _dot(x, wq_ref[...])
    outs = []
    for h in range(MEM_HEADS):
        s = _dot_nt(q[:, h * HD:(h + 1) * HD], kv[:, h * HD:(h + 1) * HD]) * SCALE
        p = jnp.exp(s - jnp.max(s, axis=1, keepdims=True))
        p = p / jnp.sum(p, axis=1, keepdims=True)
        outs.append(_dot(p, kv[:, (MEM_HEADS + h) * HD:(MEM_HEADS + h + 1) * HD]))
    y = _dot(jnp.concatenate(outs, axis=1), wo_ref[...])
    o_ref[0] = _ln(ALPHA * x + y, g_ref[...], b_ref[...])


def _mem_layer(x, mem_kv, wq, wo, g, b, *, tq):
    bsz, t, d = x.shape
    nm, cw = mem_kv.shape[1:]
    return pl.pallas_call(
        _mem_body,
        grid=(bsz, t // tq),
        in_specs=[pl.BlockSpec((1, tq, d), lambda bi, qi: (bi, qi, 0)),
                  pl.BlockSpec((1, nm, cw), lambda bi, qi: (bi, 0, 0)),
                  pl.BlockSpec(wq.shape, lambda bi, qi: (0, 0)),
                  pl.BlockSpec(wo.shape, lambda bi, qi: (0, 0)),
                  pl.BlockSpec((1, d), lambda bi, qi: (0, 0)),
                  pl.BlockSpec((1, d), lambda bi, qi: (0, 0))],
        out_specs=pl.BlockSpec((1, tq, d), lambda bi, qi: (bi, qi, 0)),
        out_shape=jax.ShapeDtypeStruct((bsz, t, d), F32),
        compiler_params=_params(("parallel", "parallel")),
        name="mem_layer",
    )(x, mem_kv, wq, wo, g.reshape(1, d), b.reshape(1, d))


def _ffn_body(x_ref, wg_ref, wu_ref, wd_ref, g_ref, b_ref, o_ref, xb_ref, acc_ref, *, nf):
    f = pl.program_id(1)

    @pl.when(f == 0)
    def _():
        xb_ref[...] = x_ref[...].astype(MXU)
        acc_ref[...] = jnp.zeros(acc_ref.shape, F32)

    xb = xb_ref[...]
    gate = jnp.dot(xb, wg_ref[...], preferred_element_type=F32)
    up = jnp.dot(xb, wu_ref[...], preferred_element_type=F32)
    hidden = gate / (1.0 + jnp.exp(-gate)) * up
    acc_ref[...] += _dot(hidden, wd_ref[...])

    @pl.when(f == nf - 1)
    def _():
        o_ref[...] = _ln(ALPHA * x_ref[...] + acc_ref[...], g_ref[...], b_ref[...])


def _ffn_layer(x, w_gu, w_d, g, b, *, tm=512, tf=512):
    m, d = x.shape
    dff = w_d.shape[0]
    tm = _tile(m, tm, 8)
    tf = _tile(dff, tf, LANES)
    nf = dff // tf
    return pl.pallas_call(
        functools.partial(_ffn_body, nf=nf),
        grid=(m // tm, nf),
        in_specs=[pl.BlockSpec((tm, d), lambda i, f: (i, 0)),
                  pl.BlockSpec((d, tf), lambda i, f: (0, f)),
                  pl.BlockSpec((d, tf), lambda i, f: (0, nf + f)),
                  pl.BlockSpec((tf, d), lambda i, f: (f, 0)),
                  pl.BlockSpec((1, d), lambda i, f: (0, 0)),
                  pl.BlockSpec((1, d), lambda i, f: (0, 0))],
        out_specs=pl.BlockSpec((tm, d), lambda i, f: (i, 0)),
        out_shape=jax.ShapeDtypeStruct((m, d), F32),
        scratch_shapes=[pltpu.VMEM((tm, d), MXU), pltpu.VMEM((tm, d), F32)],
        compiler_params=_params(("parallel", "arbitrary")),
        name="ffn_layer",
    )(x, w_gu, w_gu, w_d, g.reshape(1, d), b.reshape(1, d))


def _even_odd_rows(a):
    b, n2, c = a.shape
    return a.reshape(b, n2 // 2, 2, c).transpose(0, 2, 1, 3).reshape(b, n2, c)


def _pad_rows(a, rows):
    return jnp.pad(a, ((0, 0), (0, rows - a.shape[1]), (0, 0)))


def _ab_mixer(x, w, layer, past, *, pos0):
    b, t, d = x.shape
    m = b * t
    pos = pos0 + jnp.arange(t, dtype=I32)
    tabs = _rope_tables(pos, HD) + _rope_tables(pos, D_IDX)
    h = _mm(x.reshape(m, d), w["in_ab"][layer], tn=640)
    q_a, kv_a, q_i, q_b, cmp_kv, sel_kv, win_kv, misc = _post_ab(h, tabs, t)
    r3 = lambda a: a.reshape(b, t, a.shape[1])
    q_a, kv_a, q_i, q_b, cmp_kv, sel_kv, win_kv, misc = map(
        r3, (q_a, kv_a, q_i, q_b, cmp_kv, sel_kv, win_kv, misc))
    cw = 2 * KV_B * HD
    rep_a, rep_b = H_A // KV_A, H_B // KV_B
    if past is None:
        s_len = t
        tq = _tile(t, 256, 8)
        tk = _tile(t, 1024, LANES)
        topk = min(TOPK_MAX, s_len // 4)
        bias_a = _dsa_index(q_i, misc, misc, topk=topk, pos0=0, tq=tq, tk=_tile(t, 512, LANES))
        o_a = _attn(q_a, kv_a, groups=KV_A, rep=rep_a, tq=tq, tk=tk, pos0=0, bias=bias_a)
        nc = s_len // CMP_BLOCK
        kc = _mm(cmp_kv.reshape(m // CMP_BLOCK, CMP_BLOCK * cw), w["phi"][layer], tn=cw)
        kc = _even_odd_rows(kc.reshape(b, nc, cw))
        s_pad = s_len
        o_w = _attn(q_b, win_kv, groups=KV_B, rep=rep_b, tq=tq, tk=tq, pos0=0)
        sel_attn = lambda bias: _attn(q_b, sel_kv, groups=KV_B, rep=rep_b, tq=tq, tk=tk,
                                      pos0=0, bias=bias)
        win_state = win_kv[:, t - min(WINDOW, t):]
    else:
        pt, caches, win_prev = past
        n_pages = pt.shape[1]
        page = caches["page"]
        n_pool = caches["n_pool"]
        off = layer * n_pool
        s_len = n_pages * page + t
        s_pad = (n_pages + 1) * page
        tq = t
        topk = min(TOPK_MAX, s_len // 4)
        bias_a = _dsa_index(q_i, misc, caches["a_kidx"], topk=topk, pos0=pos0, tq=tq, tk=page,
                            page=(pt, off, _pad_rows(misc[:, :, :D_IDX], page)))
        o_a = _attn_paged(q_a, caches["a_kv"], pt, off, _pad_rows(kv_a, page), bias_a,
                          groups=KV_A, rep=rep_a, page=page)
        nc = s_len // CMP_BLOCK
        per_page = page // CMP_BLOCK
        assert nc == n_pages * per_page, "new rows must not complete a compressed block"
        raw = _cmp_rows(caches["cmp"], pt, off, page=page, rpt=cw // LANES)
        kc = _mm(raw.reshape(b * nc, raw.shape[2]), w["phi"][layer], tn=cw)
        kc = _even_odd_rows(kc.reshape(b, nc, cw))
        win_all = jnp.concatenate([win_prev, win_kv], axis=1)
        k_start = n_pages * page - win_prev.shape[1]
        o_w = _attn(q_b, _pad_rows(win_all, -(-win_all.shape[1] // page) * page),
                    groups=KV_B, rep=rep_b, tq=tq, tk=page, pos0=pos0, k_start=k_start)
        sel_attn = lambda bias: _attn_paged(q_b, caches["sel"], pt, off, _pad_rows(sel_kv, page),
                                            bias, groups=KV_B, rep=rep_b, page=page)
        win_state = win_all[:, win_all.shape[1] - win_prev.shape[1]:]
    assert nc % 2 == 0
    c_pos = (jnp.arange(nc, dtype=I32) + 1) * CMP_BLOCK - 1
    ctab = tuple(_even_odd_rows(tb[None])[0] for tb in _rope_tables(c_pos, HD))
    o_c, bias_s = _nsa_cmp(q_b, kc, ctab, s_len=s_len, s_pad=s_pad, pos0=pos0, tq=tq)
    o_s = sel_attn(bias_s)
    f2 = lambda a: a.reshape(m, a.shape[2])
    o = _combine(f2(o_a), f2(o_c), f2(o_s), f2(o_w), f2(misc))
    states = (kv_a, misc[:, :, :D_IDX], cmp_kv, sel_kv, win_state)
    return o, states


def _sb_mixer(x, w, layer, past, *, pos0):
    b, t, d = x.shape
    m = b * t
    x2 = x.reshape(m, d)
    q = _mm(x2, w["in_c_q"][layer]).reshape(b, t, H_C * HD)
    kv = _mm(x2, w["in_c_kv"][layer]).reshape(b, t, 2 * KV_C * HD)
    if past is None:
        tq = _tile(t, 256, 8)
        o = _sb_attn(q, kv, tq=tq, tk=tq, pos0=0)
    else:
        pt, caches = past
        page = caches["page"]
        o = _sb_attn_paged(q, caches["c_kv"], pt, layer * caches["n_pool"], _pad_rows(kv, page),
                           page=page, pos0=pos0)
    return o.reshape(m, H_C * HD), kv


def _run_group(x, w, mem_kvs, pasts, ln_g, ln_b, *, pos0):
    b, t, d = x.shape
    m = b * t
    even, odd = [], []
    for i in range(DEPTH):
        if i % 2 == 0:
            mix, st = _ab_mixer(x, w, i // 2, pasts[i], pos0=pos0)
            even.append(st)
        else:
            mix, st = _sb_mixer(x, w, i // 2, pasts[i], pos0=pos0)
            odd.append(st)
        x2 = _mm_ln(mix, w["out"][i], x.reshape(m, d), ln_g[i, 0], ln_b[i, 0])
        x3 = _mem_layer(x2.reshape(b, t, d), mem_kvs[i], w["mem_q"][i], w["mem_o"][i],
                        ln_g[i, 1], ln_b[i, 1], tq=_tile(t, 256, 8))
        x = _ffn_layer(x3.reshape(m, d), w["gate_up"][i], w["down"][i],
                       ln_g[i, 2], ln_b[i, 2]).reshape(b, t, d)
    return x, even, odd


def _reorder_in_ab(w):
    sizes = (H_A * HD, 2 * KV_A * HD, H_IDX * D_IDX, D_IDX, H_IDX,
             H_B * HD, 2 * KV_B * HD, 2 * KV_B * HD, 2 * KV_B * HD, 3 * H_B)
    offs = np.concatenate([[0], np.cumsum(sizes)])
    piece = lambda i: w[..., offs[i]:offs[i + 1]]
    order = [0, 1, 2, 5, 6, 7, 8, 3, 4, 9]
    cols = [piece(i) for i in order]
    pad = C_END - int(offs[-1])
    cols.append(jnp.zeros(w.shape[:-1] + (pad,), w.dtype))
    return jnp.concatenate(cols, axis=-1)


def _phi_matrix(w_phi):
    phi = w_phi.reshape(2, CMP_BLOCK, HD, HD)
    nb = 2 * KV_B
    out = jnp.zeros((CMP_BLOCK, nb, HD, nb, HD), w_phi.dtype)
    for c in range(nb):
        out = out.at[:, c, :, c, :].set(phi[c // KV_B])
    return out.reshape(CMP_BLOCK * nb * HD, nb * HD)


def kernel(x_prompt, x_sample, cache_a_kv, cache_a_kidx, cache_b_cmp_kv, cache_b_sel_kv,
           state_b_win_kv, cache_c_kv, cache_mem_kv, page_table, mem_prompt, w_in_ab, w_cmp_phi,
           w_in_c, w_out, w_mem_q, w_mem_kv, w_mem_o, w_gate_up, w_down, ln_g, ln_b):
    b_p, t_p, d = x_prompt.shape
    b_s, t_s, _ = x_sample.shape
    n_even, n_pool, page = cache_a_kv.shape[:3]
    n_odd = cache_c_kv.shape[0]
    n_mem = mem_prompt.shape[1]
    past_len = page_table.shape[1] * page

    def per_layer(a):
        return [a[i].astype(MXU) for i in range(a.shape[0])]

    w = {
        "in_ab": [_reorder_in_ab(wl).astype(MXU) for wl in w_in_ab],
        "phi": [_phi_matrix(wl).astype(MXU) for wl in w_cmp_phi],
        "in_c_q": per_layer(w_in_c[:, :, :H_C * HD]),
        "in_c_kv": per_layer(w_in_c[:, :, H_C * HD:]),
        "out": per_layer(w_out),
        "mem_q": per_layer(w_mem_q),
        "mem_o": per_layer(w_mem_o),
        "gate_up": per_layer(w_gate_up),
        "down": per_layer(w_down),
    }
    w_mkv = per_layer(w_mem_kv)

    mem_flat = mem_prompt.reshape(b_p * n_mem, d)
    mem_p = [_mm(mem_flat, w_mkv[i]).reshape(b_p, n_mem, 2 * MEM_HEADS * HD) for i in range(DEPTH)]
    y_p, ev_p, od_p = _run_group(x_prompt, w, mem_p, [None] * DEPTH, ln_g, ln_b, pos0=0)

    cw_b = 2 * KV_B * HD
    caches = {
        "page": page, "n_pool": n_pool,
        "a_kv": cache_a_kv.reshape(-1, LANES),
        "a_kidx": cache_a_kidx.reshape(n_even * n_pool, page, D_IDX),
        "cmp": cache_b_cmp_kv.reshape(-1, LANES),
        "sel": cache_b_sel_kv.reshape(-1, LANES),
    }
    caches_c = {"page": page, "n_pool": n_pool, "c_kv": cache_c_kv.reshape(-1, LANES)}
    win_prev = state_b_win_kv.reshape(n_even, b_s, state_b_win_kv.shape[2], cw_b)
    pasts = []
    for i in range(DEPTH):
        if i % 2 == 0:
            pasts.append((page_table, caches, win_prev[i // 2]))
        else:
            pasts.append((page_table, caches_c))
    mem_s = [cache_mem_kv[i].reshape(b_s, n_mem, 2 * MEM_HEADS * HD) for i in range(DEPTH)]
    y_s, ev_s, od_s = _run_group(x_sample, w, mem_s, pasts, ln_g, ln_b, pos0=past_len)

    def pack(states, bsz):
        def kv5(a, g):
            return a.reshape(bsz, a.shape[1], 2, g, HD)
        return (jnp.stack([kv5(s[0], KV_A) for s in states]),
                jnp.stack([s[1] for s in states]),
                jnp.stack([kv5(s[2], KV_B) for s in states]),
                jnp.stack([kv5(s[3], KV_B) for s in states]),
                jnp.stack([kv5(s[4], KV_B) for s in states]))

    a_p, ki_p, cmp_p, sel_p, win_p = pack(ev_p, b_p)
    a_s, ki_s, cmp_s, sel_s, win_s = pack(ev_s, b_s)
    c_p = jnp.stack([s.reshape(b_p, t_p, 2, KV_C, HD) for s in od_p])
    c_s = jnp.stack([s.reshape(b_s, t_s, 2, KV_C, HD) for s in od_s])
    mem_out = jnp.stack([mk.reshape(b_p, n_mem, 2, MEM_HEADS, HD) for mk in mem_p])
    return (y_p, y_s, a_p, ki_p, cmp_p, sel_p, win_p, c_p, mem_out,
            a_s, ki_s, cmp_s, sel_s, win_s, c_s)
```

```python
import functools
import math

import jax
import jax.numpy as jnp
import numpy as np
from jax import lax
from jax.experimental import pallas as pl
from jax.experimental.pallas import tpu as pltpu

F32 = jnp.float32
BF16 = jnp.bfloat16
I32 = jnp.int32

HD = 128
LANES = 128
H_A, KV_A, H_IDX, D_IDX = 8, 2, 4, 64
H_B, KV_B = 8, 2
H_C, KV_C = 16, 4
MEM_HEADS = 4
TOPK_MAX = 256
CMP_BLOCK, SEL_BLOCK, N_SEL, WINDOW = 32, 64, 16, 512
ROPE_THETA = 10000.0
LN_EPS = 1e-5
DEPTH = 4
ALPHA = (2 * DEPTH) ** 0.25
SCALE = HD ** -0.5
QSCALE = SCALE * math.log2(math.e)
NEG = -1e30
INT_MIN = -2 ** 31
VMEM_LIMIT = 56 * 1024 * 1024
PAGES_PER_STEP = 16

C_QA, C_KVA, C_QI, C_QB, C_CMP, C_SEL, C_WIN, C_MISC, C_END = (
    0, 1024, 1536, 1792, 2816, 3328, 3840, 4352, 4480)
MISC_WI = D_IDX
MISC_G = D_IDX + H_IDX


def _params(sem):
    return pltpu.CompilerParams(dimension_semantics=sem, vmem_limit_bytes=VMEM_LIMIT)


def _tile(n, pref, align):
    t = (min(pref, n) // align) * align
    while t >= align:
        if n % t == 0:
            return t
        t -= align
    return n


def _ln(y, g, b):
    mu = jnp.mean(y, axis=-1, keepdims=True)
    d = y - mu
    var = jnp.mean(d * d, axis=-1, keepdims=True)
    return d * lax.rsqrt(var + LN_EPS) * g + b


MXU = BF16


def _dot(a, b):
    return jnp.dot(a.astype(MXU), b.astype(MXU), preferred_element_type=F32)


def _dot_nt(a, b):
    return lax.dot_general(a.astype(MXU), b.astype(MXU), (((1,), (1,)), ((), ())),
                           preferred_element_type=F32)


def _call(body, *, grid, in_specs, out_specs, out_shape, scratch=(), sem, name, prefetch=None):
    if prefetch is None:
        return pl.pallas_call(body, grid=grid, in_specs=in_specs, out_specs=out_specs,
                              out_shape=out_shape, scratch_shapes=list(scratch),
                              compiler_params=_params(sem), name=name)
    gs = pltpu.PrefetchScalarGridSpec(num_scalar_prefetch=1, grid=grid, in_specs=in_specs,
                                      out_specs=out_specs, scratch_shapes=list(scratch))
    return functools.partial(
        pl.pallas_call(body, grid_spec=gs, out_shape=out_shape, compiler_params=_params(sem),
                       name=name), prefetch)


def _mm_body(x_ref, w_ref, o_ref, acc_ref, *, nk):
    part = _dot(x_ref[...], w_ref[...])
    if nk == 1:
        o_ref[...] = part
        return
    k = pl.program_id(2)

    @pl.when(k == 0)
    def _():
        acc_ref[...] = part

    @pl.when(k > 0)
    def _():
        acc_ref[...] += part

    @pl.when(k == nk - 1)
    def _():
        o_ref[...] = acc_ref[...]


def _wspec(w, layer, block, index):
    if w.ndim == 2:
        return pl.BlockSpec(block, index)
    return pl.BlockSpec((None,) + block, lambda *a: (layer,) + index(*a))


def _mm(x, w, *, layer=None, cols=None, tm=1024, tn=512, tk=2048):
    m, kd = x.shape
    col0, n = cols if cols is not None else (0, w.shape[-1])
    tm = _tile(m, tm, 8)
    tn = _tile(math.gcd(n, col0), tn, LANES)
    tk = _tile(kd, tk, LANES)
    nk = kd // tk
    cb0 = col0 // tn
    return pl.pallas_call(
        functools.partial(_mm_body, nk=nk),
        grid=(m // tm, n // tn, nk),
        in_specs=[pl.BlockSpec((tm, tk), lambda i, j, k: (i, k)),
                  _wspec(w, layer, (tk, tn), lambda i, j, k: (k, cb0 + j))],
        out_specs=pl.BlockSpec((tm, tn), lambda i, j, k: (i, j)),
        out_shape=jax.ShapeDtypeStruct((m, n), F32),
        scratch_shapes=[pltpu.VMEM((tm, tn) if nk > 1 else (8, LANES), F32)],
        compiler_params=_params(("parallel", "parallel", "arbitrary")),
        name="mm",
    )(x, w)


def _mm_ln_body(x_ref, w_ref, r_ref, g_ref, b_ref, o_ref, acc_ref, *, nk):
    k = pl.program_id(1)
    part = _dot(x_ref[...], w_ref[...])
    if nk == 1:
        o_ref[...] = _ln(ALPHA * r_ref[...] + part, g_ref[...], b_ref[...])
        return

    @pl.when(k == 0)
    def _():
        acc_ref[...] = part

    @pl.when(k > 0)
    def _():
        acc_ref[...] += part

    @pl.when(k == nk - 1)
    def _():
        o_ref[...] = _ln(ALPHA * r_ref[...] + acc_ref[...], g_ref[...], b_ref[...])


def _mm_ln(x, w, res, g, b, *, layer=None, tm=512, tk=2048):
    m, kd = x.shape
    d = w.shape[-1]
    tm = _tile(m, tm, 8)
    tk = _tile(kd, tk, LANES)
    nk = kd // tk
    return pl.pallas_call(
        functools.partial(_mm_ln_body, nk=nk),
        grid=(m // tm, nk),
        in_specs=[pl.BlockSpec((tm, tk), lambda i, k: (i, k)),
                  _wspec(w, layer, (tk, d), lambda i, k: (k, 0)),
                  pl.BlockSpec((tm, d), lambda i, k: (i, 0)),
                  pl.BlockSpec((1, d), lambda i, k: (0, 0)),
                  pl.BlockSpec((1, d), lambda i, k: (0, 0))],
        out_specs=pl.BlockSpec((tm, d), lambda i, k: (i, 0)),
        out_shape=jax.ShapeDtypeStruct((m, d), F32),
        scratch_shapes=[pltpu.VMEM((tm, d) if nk > 1 else (8, LANES), F32)],
        compiler_params=_params(("parallel", "arbitrary")),
        name="mm_ln",
    )(x, w, res, g.reshape(1, d), b.reshape(1, d))


def _post_ab_body(h_ref, c128_ref, s128_ref, c64_ref, s64_ref,
                  qa_ref, kva_ref, qi_ref, qb_ref, cmp_ref, sel_ref, win_ref, misc_ref):
    cos = c128_ref[...]
    sin = s128_ref[...]
    ci = c64_ref[...]
    si = s64_ref[...]
    lane = lax.broadcasted_iota(I32, cos.shape, 1)
    first_half = (lane & (D_IDX - 1)) < (D_IDX // 2)

    def rope128(x):
        return x * cos + pltpu.roll(x, HD // 2, 1) * sin

    def rope64(x):
        partner = jnp.where(first_half, pltpu.roll(x, LANES - D_IDX // 2, 1),
                            pltpu.roll(x, D_IDX // 2, 1))
        return x * ci + partner * si

    def head(c0, i):
        return h_ref[:, c0 + i * HD:c0 + (i + 1) * HD]

    for i in range(H_A):
        qa_ref[:, i * HD:(i + 1) * HD] = rope128(head(C_QA, i))
    for i in range(H_B):
        qb_ref[:, i * HD:(i + 1) * HD] = rope128(head(C_QB, i))
    for src, dst, kv in ((C_KVA, kva_ref, KV_A), (C_SEL, sel_ref, KV_B), (C_WIN, win_ref, KV_B)):
        for i in range(kv):
            dst[:, i * HD:(i + 1) * HD] = rope128(head(src, i))
            dst[:, (kv + i) * HD:(kv + i + 1) * HD] = head(src, kv + i)
    cmp_ref[...] = h_ref[:, C_CMP:C_SEL]
    for i in range(H_IDX * D_IDX // LANES):
        qi_ref[:, i * LANES:(i + 1) * LANES] = rope64(head(C_QI, i))
    raw = h_ref[:, C_MISC:C_END]
    misc_ref[...] = jnp.where(
        lane < MISC_WI, rope64(raw),
        jnp.where(lane < MISC_G, raw * (H_IDX ** -0.5),
                  jnp.where(lane < MISC_G + 3 * H_B, 1.0 / (1.0 + jnp.exp(-raw)), 0.0)))


def _post_ab(h, tabs, t):
    m = h.shape[0]
    tm = _tile(t, 256, 8)
    nt = t // tm
    widths = (H_A * HD, 2 * KV_A * HD, H_IDX * D_IDX, H_B * HD, 2 * KV_B * HD,
              2 * KV_B * HD, 2 * KV_B * HD, LANES)
    tab_spec = pl.BlockSpec((tm, LANES), lambda i: (i % nt, 0))
    return pl.pallas_call(
        _post_ab_body,
        grid=(m // tm,),
        in_specs=[pl.BlockSpec((tm, C_END), lambda i: (i, 0))] + [tab_spec] * 4,
        out_specs=[pl.BlockSpec((tm, w), lambda i: (i, 0)) for w in widths],
        out_shape=[jax.ShapeDtypeStruct((m, w), F32) for w in widths],
        compiler_params=_params(("parallel",)),
        name="post_ab",
    )(h, *tabs)


def _rope_tables(pos, d):
    half = d // 2
    inv = ROPE_THETA ** (-(jnp.arange(half, dtype=F32) * 2.0 / d))
    ang = pos.astype(F32)[:, None] * inv[None, :]
    cos, sin = jnp.cos(ang), jnp.sin(ang)
    reps = LANES // d
    return (jnp.tile(jnp.concatenate([cos, cos], axis=1), (1, reps)),
            jnp.tile(jnp.concatenate([-sin, sin], axis=1), (1, reps)))


def _sortable(x):
    bits = lax.bitcast_convert_type(jnp.where(x == 0.0, 0.0, x), I32)
    return bits ^ ((bits >> 31) & 0x7FFFFFFF)


def _dsa_index_body(*refs, tq, wk, n_kx, n_main, tail_w, topk, pos0, n_bucket):
    refs = list(refs)
    if tail_w:
        refs.pop(0)
    qi_ref, wi_ref = refs[0], refs[1]
    kx_refs = refs[2:2 + n_kx]
    i = 2 + n_kx
    tail_ref = None
    if tail_w:
        tail_ref = refs[i]
        i += 1
    bias_ref, key_ref, kth_ref, need_ref = refs[i:i + 4]
    qb = pl.program_id(1)
    j = pl.program_id(2)
    nch = key_ref.shape[0]
    nj = n_main + (1 if tail_w else 0)
    last_q = pos0 + (qb + 1) * tq - 1

    def put_keys(kx, chunk0, width, kpos0):
        qi = qi_ref[0]
        wi = wi_ref[0]
        score = jnp.zeros((tq, width), F32)
        for h in range(H_IDX):
            dots = _dot_nt(qi[:, h * D_IDX:(h + 1) * D_IDX], kx) * (D_IDX ** -0.5)
            score = score + wi[:, MISC_WI + h:MISC_WI + h + 1] * jnp.maximum(dots, 0.0)
        qpos = pos0 + qb * tq + lax.broadcasted_iota(I32, (tq, width), 0)
        kpos = kpos0 + lax.broadcasted_iota(I32, (tq, width), 1)
        keys = jnp.where(kpos <= qpos, _sortable(score), INT_MIN)
        for u in range(width // LANES):
            key_ref[chunk0 + u] = keys[:, u * LANES:(u + 1) * LANES]

    def main_step():
        kx = jnp.concatenate([r[0][:, :D_IDX] for r in kx_refs], axis=0)
        put_keys(kx, j * (wk // LANES), wk, j * wk)

    if tail_w:
        pl.when(j < n_main)(main_step)

        @pl.when(j == n_main)
        def _():
            put_keys(tail_ref[0][:, :D_IDX], n_main * (wk // LANES), tail_w, n_main * wk)
    else:
        pl.when(j * wk <= last_q)(main_step)

        @pl.when(j * wk > last_q)
        def _():
            for u in range(wk // LANES):
                key_ref[j * (wk // LANES) + u] = jnp.full((tq, LANES), INT_MIN, I32)

    def search(lim):
        n_part = max(1, min(4, tq // 64))
        pr = tq // n_part

        def count(pred, p):
            hit = pred(key_ref[0:lim, p * pr:(p + 1) * pr, :]).astype(I32)
            return jnp.sum(jnp.sum(hit, axis=0), axis=1, keepdims=True)

        ans0 = tuple(jnp.where(count(lambda k: k >= 0, p) >= topk, 0, INT_MIN)
                     for p in range(n_part))

        def bit_step(b, ans):
            bit = jnp.left_shift(jnp.int32(1), 30 - b)
            out = []
            for p in range(n_part):
                cand = ans[p] | bit
                out.append(jnp.where(count(lambda k: k >= cand, p) >= topk, cand, ans[p]))
            return tuple(out)

        kth = lax.fori_loop(0, 31, bit_step, ans0)
        for p in range(n_part):
            rows = slice(p * pr, (p + 1) * pr)
            kth_ref[rows, :] = jnp.broadcast_to(kth[p], (pr, LANES))
            need_ref[rows, :] = jnp.broadcast_to(
                (topk - count(lambda k: k > kth[p], p)).astype(F32), (pr, LANES))

    @pl.when(j == nj - 1)
    def _():
        n_live = last_q // LANES + 1
        lims = [nch * (i + 1) // n_bucket for i in range(n_bucket)]
        for i, lim in enumerate(lims):
            lo = lims[i - 1] if i else 0
            pl.when((n_live > lo) & ((n_live <= lim) | (i == n_bucket - 1)))(
                functools.partial(search, lim))
        kth = kth_ref[...]
        need = need_ref[...]
        r_i = lax.broadcasted_iota(I32, (LANES, LANES), 0)
        c_i = lax.broadcasted_iota(I32, (LANES, LANES), 1)
        strict = jnp.where(r_i < c_i, 1.0, 0.0).astype(BF16)
        ones = jnp.ones((LANES, LANES), BF16)
        seen = jnp.zeros((tq, LANES), F32)
        for c in range(nch):
            kc = key_ref[c]
            tie = jnp.where(kc == kth, 1.0, 0.0).astype(BF16)
            rank = seen + jnp.dot(tie, strict, preferred_element_type=F32)
            tie_bias = jnp.where(kc == kth, jnp.where(rank < need, 0.0, NEG), NEG)
            take = jnp.where(kc > kth, 0.0, tie_bias)
            bias_ref[0, 0, :, c * LANES:(c + 1) * LANES] = (
                jnp.where(kc > INT_MIN, take, NEG).astype(bias_ref.dtype))
            seen = seen + jnp.dot(tie, ones, preferred_element_type=F32)


def _dsa_index(qi, misc, kx, *, topk, pos0, tq, tk, page=None):
    b, t, _ = qi.shape
    nq = t // tq
    if page is not None:
        pt, off, tail = page
        n_pages = pt.shape[1]
        n_kx = _tile(n_pages, PAGES_PER_STEP, 1)
        n_main = n_pages // n_kx
        tail_w = tk
        wk = n_kx * tk
        ix = lambda f: (lambda bi, qb, j, p: f(bi, qb, j))
        kx_specs = [pl.BlockSpec(
            (1, tk, kx.shape[2]),
            functools.partial(lambda u, bi, qb, j, p:
                              (off + p[bi, jnp.minimum(j, n_main - 1) * n_kx + u], 0, 0), u))
            for u in range(n_kx)]
        extra = [pl.BlockSpec((1, tk, tail.shape[2]), ix(lambda bi, qb, j: (bi, 0, 0)))]
        args = [qi, misc] + [kx] * n_kx + [tail]
        prefetch = pt
        s_pad = n_pages * tk + tail_w
    else:
        n_kx, n_main, tail_w, wk = 1, kx.shape[1] // tk, 0, tk
        ix = lambda f: f
        kx_specs = [pl.BlockSpec((1, tk, kx.shape[2]), lambda bi, qb, j: (bi, j, 0))]
        extra = []
        args = [qi, misc, kx]
        prefetch = None
        s_pad = kx.shape[1]
    nj = n_main + (1 if tail_w else 0)
    dtype = BF16 if tq % 16 == 0 else F32
    in_specs = [pl.BlockSpec((1, tq, qi.shape[2]), ix(lambda bi, qb, j: (bi, qb, 0))),
                pl.BlockSpec((1, tq, LANES), ix(lambda bi, qb, j: (bi, qb, 0)))] + kx_specs + extra
    body = functools.partial(_dsa_index_body, tq=tq, wk=wk, n_kx=n_kx, n_main=n_main,
                             tail_w=tail_w, topk=topk, pos0=pos0,
                             n_bucket=min(4, nq) if not tail_w else 1)
    return _call(body, grid=(b, nq, nj), in_specs=in_specs,
                 out_specs=pl.BlockSpec((1, 1, tq, s_pad), ix(lambda bi, qb, j: (bi, 0, qb, 0))),
                 out_shape=jax.ShapeDtypeStruct((b, 1, t, s_pad), dtype),
                 scratch=[pltpu.VMEM((s_pad // LANES, tq, LANES), I32),
                          pltpu.VMEM((tq, LANES), I32), pltpu.VMEM((tq, LANES), F32)],
                 sem=("parallel", "parallel", "arbitrary"), name="dsa_index",
                 prefetch=prefetch)(*args)


def _flash_init(q_ref, q_scr, m_scr, l_scr, acc_scr, groups, rep, tq):
    m_scr[...] = jnp.full(m_scr.shape, NEG, F32)
    l_scr[...] = jnp.zeros(l_scr.shape, F32)
    acc_scr[...] = jnp.zeros(acc_scr.shape, F32)
    for g in range(groups):
        for r in range(rep):
            h = g * rep + r
            q_scr[g, r * tq:(r + 1) * tq, :] = (q_ref[0, :, h * HD:(h + 1) * HD] * QSCALE).astype(MXU)


def _flash_step(g, k, v, bias, q_scr, m_scr, l_scr, acc_scr, rep):
    s = _dot_nt(q_scr[g], k) + jnp.concatenate([bias.astype(F32)] * rep, axis=0)
    m_prev = m_scr[g]
    m_new = jnp.maximum(m_prev, jnp.max(s, axis=1, keepdims=True))
    p = jnp.exp2(s - jnp.concatenate([m_new] * (k.shape[0] // LANES), axis=1))
    a = jnp.exp2(m_prev - m_new)
    l_scr[g] = a * l_scr[g] + jnp.sum(p, axis=1, keepdims=True)
    acc_scr[g] = a * acc_scr[g] + _dot(p, v)
    m_scr[g] = m_new


def _flash_final(o_ref, m_scr, l_scr, acc_scr, groups, rep, tq):
    for g in range(groups):
        for r in range(rep):
            h = g * rep + r
            rows = slice(r * tq, (r + 1) * tq)
            seen = m_scr[g][rows, :] > 0.5 * NEG
            o_ref[0, :, h * HD:(h + 1) * HD] = jnp.where(
                seen, acc_scr[g][rows, :] / l_scr[g][rows, :], 0.0)


def _flash_scratch(groups, rep, tq):
    rows = rep * tq
    return [pltpu.VMEM((groups, rows, HD), MXU),
            pltpu.VMEM((groups, rows, LANES), F32),
            pltpu.VMEM((groups, rows, LANES), F32),
            pltpu.VMEM((groups, rows, HD), F32)]


def _attn_body(*refs, groups, rep, tq, tk, nj, pos0, k_start, band, lead, gm):
    q_ref, kv_ref = refs[0], refs[1]
    bias_ref = None if band else refs[2]
    o_ref, q_scr, m_scr, l_scr, acc_scr = refs[2 if band else 3:]
    qb = pl.program_id(1)
    j = pl.program_id(2)
    kb = qb - lead + j if band else j

    @pl.when(j == 0)
    def _():
        _flash_init(q_ref, q_scr, m_scr, l_scr, acc_scr, groups, rep, tq)

    last_q = pos0 + qb * tq + tq - 1

    @pl.when((kb >= 0) & (k_start + kb * tk <= last_q))
    def _():
        kv = kv_ref[0]
        if band:
            qpos = pos0 + qb * tq + lax.broadcasted_iota(I32, (tq, tk), 0)
            kpos = k_start + kb * tk + lax.broadcasted_iota(I32, (tq, tk), 1)
            bias = jnp.where(kpos <= qpos, jnp.where(kpos > qpos - WINDOW, 0.0, NEG), NEG)
        for g in range(groups):
            if not band:
                bias = bias_ref[0, g if gm > 1 else 0]
            _flash_step(g, kv[:, g * HD:(g + 1) * HD],
                        kv[:, (groups + g) * HD:(groups + g + 1) * HD],
                        bias, q_scr, m_scr, l_scr, acc_scr, rep)

    @pl.when(j == nj - 1)
    def _():
        _flash_final(o_ref, m_scr, l_scr, acc_scr, groups, rep, tq)


def _attn(q, kv, *, groups, rep, tq, tk, pos0, bias=None, k_start=0):
    b, t, qw = q.shape
    band = bias is None
    cw = kv.shape[2]
    nq = t // tq
    lead = WINDOW // tk if (band and nq > 1) else 0
    nj = (lead + 1) if (band and nq > 1) else kv.shape[1] // tk

    def kblock(qb, j):
        if band:
            return jnp.maximum(qb - lead + j, 0)
        return jnp.minimum(j, (pos0 + (qb + 1) * tq - 1) // tk)

    in_specs = [pl.BlockSpec((1, tq, qw), lambda bi, qb, j: (bi, qb, 0)),
                pl.BlockSpec((1, tk, cw), lambda bi, qb, j: (bi, kblock(qb, j), 0))]
    args = [q, kv]
    gm = 1
    if not band:
        gm = bias.shape[1]
        in_specs.append(pl.BlockSpec((1, gm, tq, tk), lambda bi, qb, j: (bi, 0, qb, kblock(qb, j))))
        args.append(bias)
    body = functools.partial(_attn_body, groups=groups, rep=rep, tq=tq, tk=tk, nj=nj, pos0=pos0,
                             k_start=k_start, band=band, lead=lead, gm=gm)
    return _call(body, grid=(b, nq, nj), in_specs=in_specs,
                 out_specs=pl.BlockSpec((1, tq, qw), lambda bi, qb, j: (bi, qb, 0)),
                 out_shape=jax.ShapeDtypeStruct((b, t, qw), F32),
                 scratch=_flash_scratch(groups, rep, tq),
                 sem=("parallel", "parallel", "arbitrary"),
                 name="attn_band" if band else "attn_bias")(*args)


def _attn_paged_body(_, q_ref, *refs, groups, rep, tq, page, n_pg, n_main, gm):
    pages = refs[:n_pg]
    tail_ref, bias_ref, bias_tail_ref, o_ref, q_scr, m_scr, l_scr, acc_scr = refs[n_pg:]
    j = pl.program_id(2)
    rpt = 2 * groups

    @pl.when(j == 0)
    def _():
        _flash_init(q_ref, q_scr, m_scr, l_scr, acc_scr, groups, rep, tq)

    @pl.when(j < n_main)
    def _():
        for g in range(groups):
            k = jnp.concatenate([pg[pl.ds(g, page, stride=rpt), :] for pg in pages], axis=0)
            v = jnp.concatenate([pg[pl.ds(groups + g, page, stride=rpt), :] for pg in pages], axis=0)
            _flash_step(g, k, v, bias_ref[0, g if gm > 1 else 0], q_scr, m_scr, l_scr, acc_scr, rep)

    @pl.when(j == n_main)
    def _():
        kv = tail_ref[0]
        for g in range(groups):
            _flash_step(g, kv[:, g * HD:(g + 1) * HD],
                        kv[:, (groups + g) * HD:(groups + g + 1) * HD],
                        bias_tail_ref[0, g if gm > 1 else 0], q_scr, m_scr, l_scr, acc_scr, rep)
        _flash_final(o_ref, m_scr, l_scr, acc_scr, groups, rep, tq)


def _attn_paged(q, pool, pt, off, tail, bias, *, groups, rep, page):
    b, t, qw = q.shape
    n_pages = pt.shape[1]
    n_pg = _tile(n_pages, PAGES_PER_STEP, 1)
    n_main = n_pages // n_pg
    rpt = 2 * groups
    gm = bias.shape[1]
    ix = lambda f: (lambda bi, qb, j, p: f(bi, qb, j))
    page_specs = [pl.BlockSpec(
        (page * rpt, LANES),
        functools.partial(lambda u, bi, qb, j, p:
                          (off + p[bi, jnp.minimum(j, n_main - 1) * n_pg + u], 0), u))
        for u in range(n_pg)]
    in_specs = ([pl.BlockSpec((1, t, qw), ix(lambda bi, qb, j: (bi, 0, 0)))] + page_specs + [
        pl.BlockSpec((1, page, tail.shape[2]), ix(lambda bi, qb, j: (bi, 0, 0))),
        pl.BlockSpec((1, gm, t, n_pg * page),
                     ix(lambda bi, qb, j: (bi, 0, 0, jnp.minimum(j, n_main - 1)))),
        pl.BlockSpec((1, gm, t, page), ix(lambda bi, qb, j: (bi, 0, 0, n_pages)))])
    body = functools.partial(_attn_paged_body, groups=groups, rep=rep, tq=t, page=page,
                             n_pg=n_pg, n_main=n_main, gm=gm)
    return _call(body, grid=(b, 1, n_main + 1), in_specs=in_specs,
                 out_specs=pl.BlockSpec((1, t, qw), ix(lambda bi, qb, j: (bi, 0, 0))),
                 out_shape=jax.ShapeDtypeStruct((b, t, qw), F32),
                 scratch=_flash_scratch(groups, rep, t),
                 sem=("parallel", "parallel", "arbitrary"), name="attn_paged",
                 prefetch=pt)(q, *([pool] * n_pg), tail, bias, bias)


def _nsa_cmp_body(q_ref, kc_ref, cc_ref, cs_ref, oc_ref, bias_ref, *, tq, nc, ns, nsp, s_pad,
                  chunk, pos0):
    qb = pl.program_id(1)
    half = nc // 2
    rep = H_B // KV_B
    kc = kc_ref[0]
    cos = cc_ref[...]
    sin = cs_ref[...]
    qpos1 = pos0 + qb * tq + lax.broadcasted_iota(I32, (tq, 1), 0)
    n_perm = lax.broadcasted_iota(I32, (tq, nc), 1)
    n_orig = jnp.where(n_perm < half, 2 * n_perm, 2 * (n_perm - half) + 1)
    c_ok = jnp.where(((n_orig + 1) * CMP_BLOCK - 1) <= qpos1, 1.0, 0.0)
    c_okr = jnp.concatenate([c_ok] * rep, axis=0) > 0.5
    blk = lax.broadcasted_iota(I32, (tq, nsp), 1)
    sel_shift = SEL_BLOCK.bit_length() - 1
    cur = qpos1 >> sel_shift
    forced = (blk == 0) | (blk == cur) | (blk == cur - 1)
    b_causal = blk * SEL_BLOCK <= qpos1
    for g in range(KV_B):
        k_raw = kc[:, g * HD:(g + 1) * HD]
        k_g = k_raw * cos + pltpu.roll(k_raw, HD // 2, 1) * sin
        v_g = kc[:, (KV_B + g) * HD:(KV_B + g + 1) * HD]
        qg = jnp.concatenate(
            [q_ref[0, :, (g * rep + r) * HD:(g * rep + r + 1) * HD] for r in range(rep)], axis=0)
        s = _dot_nt(qg, k_g) * SCALE
        s = jnp.where(c_okr, s, NEG)
        m = jnp.max(s, axis=1, keepdims=True)
        p = jnp.where(c_okr, jnp.exp(s - m), 0.0)
        den = jnp.sum(p, axis=1, keepdims=True)
        p = p / jnp.where(den > 0.0, den, 1.0)
        o = _dot(p, v_g)
        imp_c = p[0:tq]
        for r in range(rep):
            oc_ref[0, :, (g * rep + r) * HD:(g * rep + r + 1) * HD] = o[r * tq:(r + 1) * tq]
            if r > 0:
                imp_c = imp_c + p[r * tq:(r + 1) * tq]
        imp = imp_c[:, :half] + imp_c[:, half:]
        if nsp > half:
            imp = jnp.concatenate([imp, jnp.zeros((tq, nsp - half), F32)], axis=1)
        val = jnp.where(forced, jnp.inf, jnp.where(b_causal, imp, -jnp.inf))
        rank = jnp.zeros((tq, nsp), F32)
        for i in range(ns):
            col = val[:, i:i + 1]
            low = jnp.where(blk > i, 1.0, 0.0)
            rank = rank + jnp.where(col > val, 1.0, jnp.where(col == val, low, 0.0))
        chosen = jnp.where(rank < min(N_SEL, ns), 1.0, 0.0).astype(BF16)
        row_blk = lax.broadcasted_iota(I32, (nsp, chunk), 0)
        for c in range(s_pad // chunk):
            tok_b = c * chunk + lax.broadcasted_iota(I32, (nsp, chunk), 1)
            expand = jnp.where((tok_b >> sel_shift) == row_blk, 1.0, 0.0).astype(BF16)
            hit = jnp.dot(chosen, expand, preferred_element_type=F32)
            tok = c * chunk + lax.broadcasted_iota(I32, (tq, chunk), 1)
            bias_ref[0, g, :, c * chunk:(c + 1) * chunk] = jnp.where(
                tok <= qpos1, jnp.where(hit > 0.5, 0.0, NEG), NEG).astype(bias_ref.dtype)


def _nsa_cmp(qb, kc, ctab, *, s_len, s_pad, pos0, tq):
    b, t, qw = qb.shape
    nc = kc.shape[1]
    ns = -(-s_len // SEL_BLOCK)
    nsp = -(-ns // LANES) * LANES
    chunk = next(c for c in (512, 384, 256, 128) if s_pad % c == 0)
    dtype = BF16 if tq % 16 == 0 else F32
    body = functools.partial(_nsa_cmp_body, tq=tq, nc=nc, ns=ns, nsp=nsp, s_pad=s_pad,
                             chunk=chunk, pos0=pos0)
    return pl.pallas_call(
        body,
        grid=(b, t // tq),
        in_specs=[pl.BlockSpec((1, tq, qw), lambda bi, qi: (bi, qi, 0)),
                  pl.BlockSpec((1, nc, kc.shape[2]), lambda bi, qi: (bi, 0, 0)),
                  pl.BlockSpec((nc, HD), lambda bi, qi: (0, 0)),
                  pl.BlockSpec((nc, HD), lambda bi, qi: (0, 0))],
        out_specs=[pl.BlockSpec((1, tq, qw), lambda bi, qi: (bi, qi, 0)),
                   pl.BlockSpec((1, KV_B, tq, s_pad), lambda bi, qi: (bi, 0, qi, 0))],
        out_shape=[jax.ShapeDtypeStruct((b, t, qw), F32),
                   jax.ShapeDtypeStruct((b, KV_B, t, s_pad), dtype)],
        compiler_params=_params(("parallel", "parallel")),
        name="nsa_cmp",
    )(qb, kc, *ctab)


def _cmp_rows_body(_, *refs, per_step, per_page, rpt):
    o_ref = refs[per_step]
    stride = CMP_BLOCK * rpt
    for i in range(stride):
        piece = jnp.concatenate([pg[pl.ds(i, per_page, stride=stride), :] for pg in refs[:per_step]],
                                axis=0)
        o_ref[0, :, i * LANES:(i + 1) * LANES] = piece


def _cmp_rows(pool, pt, off, *, page, rpt, per_step=PAGES_PER_STEP):
    b, n_pages = pt.shape
    per_step = _tile(n_pages, per_step, 1)
    per_page = page // CMP_BLOCK
    width = CMP_BLOCK * rpt * LANES

    def spec(u):
        return pl.BlockSpec((page * rpt, LANES),
                            lambda bi, i, p: (off + p[bi, i * per_step + u], 0))

    return _call(functools.partial(_cmp_rows_body, per_step=per_step, per_page=per_page, rpt=rpt),
                 grid=(b, n_pages // per_step), in_specs=[spec(u) for u in range(per_step)],
                 out_specs=pl.BlockSpec((1, per_step * per_page, width), lambda bi, i, p: (bi, i, 0)),
                 out_shape=jax.ShapeDtypeStruct((b, n_pages * per_page, width), F32),
                 sem=("parallel", "arbitrary"), name="cmp_rows",
                 prefetch=pt)(*([pool] * per_step))


def _combine_body(oa_ref, oc_ref, os_ref, ow_ref, misc_ref, o_ref):
    o_ref[:, :H_A * HD] = oa_ref[...]
    gate = misc_ref[...]
    for h in range(H_B):
        c0 = MISC_G + 3 * h
        sl = slice(h * HD, (h + 1) * HD)
        o_ref[:, H_A * HD + h * HD:H_A * HD + (h + 1) * HD] = (
            gate[:, c0:c0 + 1] * oc_ref[:, sl] + gate[:, c0 + 1:c0 + 2] * os_ref[:, sl]
            + gate[:, c0 + 2:c0 + 3] * ow_ref[:, sl])


def _combine(o_a, o_c, o_s, o_w, misc):
    m = o_a.shape[0]
    tm = _tile(m, 512, 8)
    wa, wb = H_A * HD, H_B * HD
    return pl.pallas_call(
        _combine_body,
        grid=(m // tm,),
        in_specs=[pl.BlockSpec((tm, wa), lambda i: (i, 0))]
        + [pl.BlockSpec((tm, wb), lambda i: (i, 0))] * 3
        + [pl.BlockSpec((tm, LANES), lambda i: (i, 0))],
        out_specs=pl.BlockSpec((tm, wa + wb), lambda i: (i, 0)),
        out_shape=jax.ShapeDtypeStruct((m, wa + wb), F32),
        compiler_params=_params(("parallel",)),
        name="combine",
    )(o_a, o_c, o_s, o_w, misc)


def _sb_init(q_ref, q_scr, c_scr, acc_scr, tq):
    rep = H_C // KV_C
    c_scr[...] = jnp.zeros(c_scr.shape, F32)
    acc_scr[...] = jnp.zeros(acc_scr.shape, F32)
    for g in range(KV_C):
        for r in range(rep):
            h = g * rep + r
            q_scr[g, r * tq:(r + 1) * tq, :] = (q_ref[0, :, h * HD:(h + 1) * HD] * QSCALE).astype(MXU)


def _sb_block(ks, vs, before, q_scr, c_scr, acc_scr):
    rep = H_C // KV_C
    rows = q_scr.shape[1]
    nchunk = ks[0].shape[0] // LANES
    older = None if before is None else jnp.concatenate([before] * rep, axis=0) > 0.5
    ys, pieces = [], []
    for g in range(KV_C):
        y = _dot_nt(q_scr[g], ks[g])
        drop = jnp.maximum(y, 0.0) + jnp.log2(1.0 + jnp.exp2(-jnp.abs(y)))
        if older is not None:
            drop = jnp.where(older, drop, 0.0)
        ys.append(y)
        pieces += [drop[:, c * LANES:(c + 1) * LANES] for c in range(nchunk)]
    stacked = jnp.concatenate(pieces, axis=0)
    hi = stacked.astype(BF16)
    lo = (stacked - hi.astype(F32)).astype(BF16)
    r_i = lax.broadcasted_iota(I32, (LANES, LANES), 0)
    c_i = lax.broadcasted_iota(I32, (LANES, LANES), 1)
    later = jnp.where(r_i >= c_i, -1.0, 0.0).astype(BF16)
    incl = (jnp.dot(hi, later, preferred_element_type=F32)
            + jnp.dot(lo, later, preferred_element_type=F32))
    for g in range(KV_C):
        run = c_scr[g]
        cols = [None] * nchunk
        for c in reversed(range(nchunk)):
            part = incl[(g * nchunk + c) * rows:(g * nchunk + c + 1) * rows]
            cols[c] = part + run
            run = run + jnp.broadcast_to(part[:, :1], part.shape)
        a = jnp.exp2(ys[g] + jnp.concatenate(cols, axis=1))
        if older is not None:
            a = jnp.where(older, a, 0.0)
        acc_scr[g] = acc_scr[g] + _dot(a, vs[g])
        c_scr[g] = run


def _sb_final(o_ref, acc_scr, tq):
    rep = H_C // KV_C
    for g in range(KV_C):
        for r in range(rep):
            h = g * rep + r
            o_ref[0, :, h * HD:(h + 1) * HD] = acc_scr[g][r * tq:(r + 1) * tq, :]


def _sb_scratch(tq):
    rows = (H_C // KV_C) * tq
    return [pltpu.VMEM((KV_C, rows, HD), MXU),
            pltpu.VMEM((KV_C, rows, LANES), F32),
            pltpu.VMEM((KV_C, rows, HD), F32)]


def _split_kv(kv):
    ks = [kv[:, g * HD:(g + 1) * HD] for g in range(KV_C)]
    vs = [kv[:, (KV_C + g) * HD:(KV_C + g + 1) * HD] for g in range(KV_C)]
    return ks, vs


def _older_mask(tq, tk, qpos0, kpos0):
    qpos = qpos0 + lax.broadcasted_iota(I32, (tq, tk), 0)
    kpos = kpos0 + lax.broadcasted_iota(I32, (tq, tk), 1)
    return jnp.where(kpos < qpos, 1.0, 0.0)


def _sb_body(q_ref, kv_ref, o_ref, q_scr, c_scr, acc_scr, *, tq, tk, nj, pos0):
    qb = pl.program_id(1)
    j = pl.program_id(2)
    kb = (pos0 + (qb + 1) * tq - 1) // tk - j
    q0 = pos0 + qb * tq

    @pl.when(j == 0)
    def _():
        _sb_init(q_ref, q_scr, c_scr, acc_scr, tq)

    @pl.when((kb >= 0) & ((kb + 1) * tk > q0))
    def _():
        ks, vs = _split_kv(kv_ref[0])
        _sb_block(ks, vs, _older_mask(tq, tk, q0, kb * tk), q_scr, c_scr, acc_scr)

    @pl.when((kb >= 0) & ((kb + 1) * tk <= q0))
    def _():
        ks, vs = _split_kv(kv_ref[0])
        _sb_block(ks, vs, None, q_scr, c_scr, acc_scr)

    @pl.when(j == nj - 1)
    def _():
        _sb_final(o_ref, acc_scr, tq)


def _sb_attn(q, kv, *, tq, tk, pos0):
    b, t, qw = q.shape
    cw = kv.shape[2]
    nj = kv.shape[1] // tk
    body = functools.partial(_sb_body, tq=tq, tk=tk, nj=nj, pos0=pos0)
    return _call(
        body, grid=(b, t // tq, nj),
        in_specs=[pl.BlockSpec((1, tq, qw), lambda bi, qb, j: (bi, qb, 0)),
                  pl.BlockSpec((1, tk, cw), lambda bi, qb, j: (
                      bi, jnp.maximum((pos0 + (qb + 1) * tq - 1) // tk - j, 0), 0))],
        out_specs=pl.BlockSpec((1, tq, qw), lambda bi, qb, j: (bi, qb, 0)),
        out_shape=jax.ShapeDtypeStruct((b, t, qw), F32), scratch=_sb_scratch(tq),
        sem=("parallel", "parallel", "arbitrary"), name="sb_attn")(q, kv)


def _sb_paged_body(_, q_ref, *refs, tq, page, n_pg, n_main, pos0):
    pages = refs[:n_pg]
    tail_ref, o_ref, q_scr, c_scr, acc_scr = refs[n_pg:]
    j = pl.program_id(2)
    rpt = 2 * KV_C

    @pl.when(j == 0)
    def _():
        _sb_init(q_ref, q_scr, c_scr, acc_scr, tq)
        ks, vs = _split_kv(tail_ref[0])
        _sb_block(ks, vs, _older_mask(tq, page, pos0, n_main * n_pg * page), q_scr, c_scr, acc_scr)

    @pl.when(j > 0)
    def _():
        ks = [jnp.concatenate([pg[pl.ds(g, page, stride=rpt), :] for pg in pages], axis=0)
              for g in range(KV_C)]
        vs = [jnp.concatenate([pg[pl.ds(KV_C + g, page, stride=rpt), :] for pg in pages], axis=0)
              for g in range(KV_C)]
        _sb_block(ks, vs, None, q_scr, c_scr, acc_scr)

    @pl.when(j == n_main)
    def _():
        _sb_final(o_ref, acc_scr, tq)


def _sb_attn_paged(q, pool, pt, off, tail, *, page, pos0, pages_per_step=8):
    b, t, qw = q.shape
    n_pages = pt.shape[1]
    n_pg = _tile(n_pages, pages_per_step, 1)
    n_main = n_pages // n_pg
    rpt = 2 * KV_C
    ix = lambda f: (lambda bi, qb, j, p: f(bi, qb, j))
    page_specs = [pl.BlockSpec(
        (page * rpt, LANES),
        functools.partial(lambda u, bi, qb, j, p:
                          (off + p[bi, jnp.clip(n_main - j, 0, n_main - 1) * n_pg + u], 0), u))
        for u in range(n_pg)]
    in_specs = ([pl.BlockSpec((1, t, qw), ix(lambda bi, qb, j: (bi, 0, 0)))] + page_specs
                + [pl.BlockSpec((1, page, tail.shape[2]), ix(lambda bi, qb, j: (bi, 0, 0)))])
    body = functools.partial(_sb_paged_body, tq=t, page=page, n_pg=n_pg, n_main=n_main, pos0=pos0)
    return _call(body, grid=(b, 1, n_main + 1), in_specs=in_specs,
                 out_specs=pl.BlockSpec((1, t, qw), ix(lambda bi, qb, j: (bi, 0, 0))),
                 out_shape=jax.ShapeDtypeStruct((b, t, qw), F32), scratch=_sb_scratch(t),
                 sem=("parallel", "parallel", "arbitrary"), name="sb_attn_paged",
                 prefetch=pt)(q, *([pool] * n_pg), tail)


def _mem_body(x_ref, kv_ref, wq_ref, wo_ref, g_ref, b_ref, o_ref):
    x = x_ref[0]
    kv = kv_ref[0]
    q = _dot(x, wq_ref[...])
    outs = []
    for h in range(MEM_HEADS):
        s = _dot_nt(q[:, h * HD:(h + 1) * HD], kv[:, h * HD:(h + 1) * HD]) * SCALE
        p = jnp.exp(s - jnp.max(s, axis=1, keepdims=True))
        p = p / jnp.sum(p, axis=1, keepdims=True)
        outs.append(_dot(p, kv[:, (MEM_HEADS + h) * HD:(MEM_HEADS + h + 1) * HD]))
    y = _dot(jnp.concatenate(outs, axis=1), wo_ref[...])
    o_ref[0] = _ln(ALPHA * x + y, g_ref[...], b_ref[...])


def _mem_layer(x, mem_kv, wq, wo, g, b, *, layer=None, tq):
    bsz, t, d = x.shape
    nm, cw = mem_kv.shape[1:]
    return pl.pallas_call(
        _mem_body,
        grid=(bsz, t // tq),
        in_specs=[pl.BlockSpec((1, tq, d), lambda bi, qi: (bi, qi, 0)),
                  pl.BlockSpec((1, nm, cw), lambda bi, qi: (bi, 0, 0)),
                  _wspec(wq, layer, wq.shape[-2:], lambda bi, qi: (0, 0)),
                  _wspec(wo, layer, wo.shape[-2:], lambda bi, qi: (0, 0)),
                  pl.BlockSpec((1, d), lambda bi, qi: (0, 0)),
                  pl.BlockSpec((1, d), lambda bi, qi: (0, 0))],
        out_specs=pl.BlockSpec((1, tq, d), lambda bi, qi: (bi, qi, 0)),
        out_shape=jax.ShapeDtypeStruct((bsz, t, d), F32),
        compiler_params=_params(("parallel", "parallel")),
        name="mem_layer",
    )(x, mem_kv, wq, wo, g.reshape(1, d), b.reshape(1, d))


def _ffn_body(x_ref, wg_ref, wu_ref, wd_ref, g_ref, b_ref, o_ref, xb_ref, h_ref, y_ref, *, nf, nn):
    s = pl.program_id(1)

    @pl.when(s == 0)
    def _():
        xb_ref[...] = x_ref[...].astype(MXU)

    @pl.when(s < nf)
    def _():
        xb = xb_ref[...]
        gate = jnp.dot(xb, wg_ref[...], preferred_element_type=F32)
        up = jnp.dot(xb, wu_ref[...], preferred_element_type=F32)
        h_ref[s] = (gate / (1.0 + jnp.exp(-gate)) * up).astype(MXU)

    @pl.when(s >= nf)
    def _():
        hidden = jnp.concatenate([h_ref[f] for f in range(nf)], axis=1)
        y_ref[s - nf] = jnp.dot(hidden, wd_ref[...], preferred_element_type=F32)

    @pl.when(s == nf + nn - 1)
    def _():
        y = jnp.concatenate([y_ref[n] for n in range(nn)], axis=1)
        o_ref[...] = _ln(ALPHA * x_ref[...] + y, g_ref[...], b_ref[...])


def _ffn_layer(x, w_gu, w_d, g, b, *, layer=None, tm=512, tf=512, tn=512):
    m, d = x.shape
    dff = w_d.shape[-2]
    tm = _tile(m, tm, 8)
    tf = _tile(dff, tf, LANES)
    tn = _tile(d, tn, LANES)
    nf, nn = dff // tf, d // tn
    return pl.pallas_call(
        functools.partial(_ffn_body, nf=nf, nn=nn),
        grid=(m // tm, nf + nn),
        in_specs=[pl.BlockSpec((tm, d), lambda i, s: (i, 0)),
                  _wspec(w_gu, layer, (d, tf), lambda i, s: (0, jnp.minimum(s, nf - 1))),
                  _wspec(w_gu, layer, (d, tf), lambda i, s: (0, nf + jnp.minimum(s, nf - 1))),
                  _wspec(w_d, layer, (dff, tn), lambda i, s: (0, jnp.maximum(s - nf, 0))),
                  pl.BlockSpec((1, d), lambda i, s: (0, 0)),
                  pl.BlockSpec((1, d), lambda i, s: (0, 0))],
        out_specs=pl.BlockSpec((tm, d), lambda i, s: (i, 0)),
        out_shape=jax.ShapeDtypeStruct((m, d), F32),
        scratch_shapes=[pltpu.VMEM((tm, d), MXU), pltpu.VMEM((nf, tm, tf), MXU),
                        pltpu.VMEM((nn, tm, tn), F32)],
        compiler_params=_params(("parallel", "arbitrary")),
        name="ffn_layer",
    )(x, w_gu, w_gu, w_d, g.reshape(1, d), b.reshape(1, d))


def _even_odd_rows(a):
    b, n2, c = a.shape
    return a.reshape(b, n2 // 2, 2, c).transpose(0, 2, 1, 3).reshape(b, n2, c)


def _pad_rows(a, rows):
    return jnp.pad(a, ((0, 0), (0, rows - a.shape[1]), (0, 0)))


def _ab_mixer(x, w, layer, past, *, pos0):
    b, t, d = x.shape
    m = b * t
    pos = pos0 + jnp.arange(t, dtype=I32)
    tabs = _rope_tables(pos, HD) + _rope_tables(pos, D_IDX)
    h = _mm(x.reshape(m, d), w["in_ab"][layer], tn=640)
    q_a, kv_a, q_i, q_b, cmp_kv, sel_kv, win_kv, misc = _post_ab(h, tabs, t)
    r3 = lambda a: a.reshape(b, t, a.shape[1])
    q_a, kv_a, q_i, q_b, cmp_kv, sel_kv, win_kv, misc = map(
        r3, (q_a, kv_a, q_i, q_b, cmp_kv, sel_kv, win_kv, misc))
    cw = 2 * KV_B * HD
    rep_a, rep_b = H_A // KV_A, H_B // KV_B
    if past is None:
        s_len = t
        tq = _tile(t, 256, 8)
        tk = _tile(t, 1024, LANES)
        topk = min(TOPK_MAX, s_len // 4)
        bias_a = _dsa_index(q_i, misc, misc, topk=topk, pos0=0, tq=tq, tk=_tile(t, 512, LANES))
        o_a = _attn(q_a, kv_a, groups=KV_A, rep=rep_a, tq=tq, tk=tk, pos0=0, bias=bias_a)
        nc = s_len // CMP_BLOCK
        kc = _mm(cmp_kv.reshape(m // CMP_BLOCK, CMP_BLOCK * cw), w["phi"][layer], tn=cw)
        kc = _even_odd_rows(kc.reshape(b, nc, cw))
        s_pad = s_len
        o_w = _attn(q_b, win_kv, groups=KV_B, rep=rep_b, tq=tq, tk=tq, pos0=0)
        sel_attn = lambda bias: _attn(q_b, sel_kv, groups=KV_B, rep=rep_b, tq=tq, tk=tk,
                                      pos0=0, bias=bias)
        win_state = win_kv[:, t - min(WINDOW, t):]
    else:
        pt, caches, win_prev = past
        n_pages = pt.shape[1]
        page = caches["page"]
        n_pool = caches["n_pool"]
        off = layer * n_pool
        s_len = n_pages * page + t
        s_pad = (n_pages + 1) * page
        tq = t
        topk = min(TOPK_MAX, s_len // 4)
        bias_a = _dsa_index(q_i, misc, caches["a_kidx"], topk=topk, pos0=pos0, tq=tq, tk=page,
                            page=(pt, off, _pad_rows(misc[:, :, :D_IDX], page)))
        o_a = _attn_paged(q_a, caches["a_kv"], pt, off, _pad_rows(kv_a, page), bias_a,
                          groups=KV_A, rep=rep_a, page=page)
        nc = s_len // CMP_BLOCK
        per_page = page // CMP_BLOCK
        assert nc == n_pages * per_page, "new rows must not complete a compressed block"
        raw = _cmp_rows(caches["cmp"], pt, off, page=page, rpt=cw // LANES)
        kc = _mm(raw.reshape(b * nc, raw.shape[2]), w["phi"][layer], tn=cw)
        kc = _even_odd_rows(kc.reshape(b, nc, cw))
        win_all = jnp.concatenate([win_prev, win_kv], axis=1)
        k_start = n_pages * page - win_prev.shape[1]
        o_w = _attn(q_b, _pad_rows(win_all, -(-win_all.shape[1] // page) * page),
                    groups=KV_B, rep=rep_b, tq=tq, tk=page, pos0=pos0, k_start=k_start)
        sel_attn = lambda bias: _attn_paged(q_b, caches["sel"], pt, off, _pad_rows(sel_kv, page),
                                            bias, groups=KV_B, rep=rep_b, page=page)
        win_state = win_all[:, win_all.shape[1] - win_prev.shape[1]:]
    assert nc % 2 == 0
    c_pos = (jnp.arange(nc, dtype=I32) + 1) * CMP_BLOCK - 1
    ctab = tuple(_even_odd_rows(tb[None])[0] for tb in _rope_tables(c_pos, HD))
    o_c, bias_s = _nsa_cmp(q_b, kc, ctab, s_len=s_len, s_pad=s_pad, pos0=pos0, tq=tq)
    o_s = sel_attn(bias_s)
    f2 = lambda a: a.reshape(m, a.shape[2])
    o = _combine(f2(o_a), f2(o_c), f2(o_s), f2(o_w), f2(misc))
    states = (kv_a, misc[:, :, :D_IDX], cmp_kv, sel_kv, win_state)
    return o, states


def _sb_mixer(x, w, layer, past, *, pos0):
    b, t, d = x.shape
    m = b * t
    x2 = x.reshape(m, d)
    q = _mm(x2, w["in_c"], layer=layer, cols=(0, H_C * HD)).reshape(b, t, H_C * HD)
    kv = _mm(x2, w["in_c"], layer=layer,
             cols=(H_C * HD, 2 * KV_C * HD)).reshape(b, t, 2 * KV_C * HD)
    if past is None:
        tq = _tile(t, 256, 8)
        o = _sb_attn(q, kv, tq=tq, tk=tq, pos0=0)
    else:
        pt, caches = past
        page = caches["page"]
        o = _sb_attn_paged(q, caches["c_kv"], pt, layer * caches["n_pool"], _pad_rows(kv, page),
                           page=page, pos0=pos0)
    return o.reshape(m, H_C * HD), kv


def _run_group(x, w, mem_kvs, pasts, ln_g, ln_b, *, pos0):
    b, t, d = x.shape
    m = b * t
    even, odd = [], []
    for i in range(DEPTH):
        if i % 2 == 0:
            mix, st = _ab_mixer(x, w, i // 2, pasts[i], pos0=pos0)
            even.append(st)
        else:
            mix, st = _sb_mixer(x, w, i // 2, pasts[i], pos0=pos0)
            odd.append(st)
        x2 = _mm_ln(mix, w["out"], x.reshape(m, d), ln_g[i, 0], ln_b[i, 0], layer=i)
        x3 = _mem_layer(x2.reshape(b, t, d), mem_kvs[i], w["mem_q"], w["mem_o"],
                        ln_g[i, 1], ln_b[i, 1], layer=i, tq=_tile(t, 256, 8))
        x = _ffn_layer(x3.reshape(m, d), w["gate_up"], w["down"],
                       ln_g[i, 2], ln_b[i, 2], layer=i).reshape(b, t, d)
    return x, even, odd


def _reorder_in_ab(w):
    sizes = (H_A * HD, 2 * KV_A * HD, H_IDX * D_IDX, D_IDX, H_IDX,
             H_B * HD, 2 * KV_B * HD, 2 * KV_B * HD, 2 * KV_B * HD, 3 * H_B)
    offs = np.concatenate([[0], np.cumsum(sizes)])
    piece = lambda i: w[..., offs[i]:offs[i + 1]]
    order = [0, 1, 2, 5, 6, 7, 8, 3, 4, 9]
    cols = [piece(i) for i in order]
    pad = C_END - int(offs[-1])
    cols.append(jnp.zeros(w.shape[:-1] + (pad,), w.dtype))
    return jnp.concatenate(cols, axis=-1)


def _phi_matrix(w_phi):
    phi = w_phi.reshape(2, CMP_BLOCK, HD, HD)
    nb = 2 * KV_B
    per_piece = jnp.stack([phi[c // KV_B] for c in range(nb)], axis=1)
    same = jnp.eye(nb, dtype=bool)[None, :, None, :, None]
    out = jnp.where(same, per_piece[:, :, :, None, :], jnp.zeros((), w_phi.dtype))
    return out.reshape(CMP_BLOCK * nb * HD, nb * HD)


def kernel(x_prompt, x_sample, cache_a_kv, cache_a_kidx, cache_b_cmp_kv, cache_b_sel_kv,
           state_b_win_kv, cache_c_kv, cache_mem_kv, page_table, mem_prompt, w_in_ab, w_cmp_phi,
           w_in_c, w_out, w_mem_q, w_mem_kv, w_mem_o, w_gate_up, w_down, ln_g, ln_b):
    b_p, t_p, d = x_prompt.shape
    b_s, t_s, _ = x_sample.shape
    n_even, n_pool, page = cache_a_kv.shape[:3]
    n_odd = cache_c_kv.shape[0]
    n_mem = mem_prompt.shape[1]
    past_len = page_table.shape[1] * page

    w = {
        "in_ab": [_reorder_in_ab(wl).astype(MXU) for wl in w_in_ab],
        "phi": [_phi_matrix(wl.astype(MXU)) for wl in w_cmp_phi],
        "in_c": w_in_c.astype(MXU),
        "out": w_out.astype(MXU),
        "mem_q": w_mem_q.astype(MXU),
        "mem_o": w_mem_o.astype(MXU),
        "gate_up": w_gate_up.astype(MXU),
        "down": w_down.astype(MXU),
    }
    w_mkv = w_mem_kv.astype(MXU)

    mem_flat = mem_prompt.reshape(b_p * n_mem, d)
    mem_p = [_mm(mem_flat, w_mkv, layer=i).reshape(b_p, n_mem, 2 * MEM_HEADS * HD)
             for i in range(DEPTH)]
    y_p, ev_p, od_p = _run_group(x_prompt, w, mem_p, [None] * DEPTH, ln_g, ln_b, pos0=0)

    cw_b = 2 * KV_B * HD
    caches = {
        "page": page, "n_pool": n_pool,
        "a_kv": cache_a_kv.reshape(-1, LANES),
        "a_kidx": cache_a_kidx.reshape(n_even * n_pool, page, D_IDX),
        "cmp": cache_b_cmp_kv.reshape(-1, LANES),
        "sel": cache_b_sel_kv.reshape(-1, LANES),
    }
    caches_c = {"page": page, "n_pool": n_pool, "c_kv": cache_c_kv.reshape(-1, LANES)}
    win_prev = state_b_win_kv.reshape(n_even, b_s, state_b_win_kv.shape[2], cw_b)
    pasts = []
    for i in range(DEPTH):
        if i % 2 == 0:
            pasts.append((page_table, caches, win_prev[i // 2]))
        else:
            pasts.append((page_table, caches_c))
    mem_s = [cache_mem_kv[i].reshape(b_s, n_mem, 2 * MEM_HEADS * HD) for i in range(DEPTH)]
    y_s, ev_s, od_s = _run_group(x_sample, w, mem_s, pasts, ln_g, ln_b, pos0=past_len)

    def pack(states, bsz):
        def kv5(a, g):
            return a.reshape(bsz, a.shape[1], 2, g, HD)
        return (jnp.stack([kv5(s[0], KV_A) for s in states]),
                jnp.stack([s[1] for s in states]),
                jnp.stack([kv5(s[2], KV_B) for s in states]),
                jnp.stack([kv5(s[3], KV_B) for s in states]),
                jnp.stack([kv5(s[4], KV_B) for s in states]))

    a_p, ki_p, cmp_p, sel_p, win_p = pack(ev_p, b_p)
    a_s, ki_s, cmp_s, sel_s, win_s = pack(ev_s, b_s)
    c_p = jnp.stack([s.reshape(b_p, t_p, 2, KV_C, HD) for s in od_p])
    c_s = jnp.stack([s.reshape(b_s, t_s, 2, KV_C, HD) for s in od_s])
    mem_out = jnp.stack([mk.reshape(b_p, n_mem, 2, MEM_HEADS, HD) for mk in mem_p])
    return (y_p, y_s, a_p, ki_p, cmp_p, sel_p, win_p, c_p, mem_out,
            a_s, ki_s, cmp_s, sel_s, win_s, c_s)
```

```python
import functools
import math

import jax
import jax.numpy as jnp
import numpy as np
from jax import lax
from jax.experimental import pallas as pl
from jax.experimental.pallas import tpu as pltpu

F32 = jnp.float32
BF16 = jnp.bfloat16
I32 = jnp.int32

HD = 128
LANES = 128
H_A, KV_A, H_IDX, D_IDX = 8, 2, 4, 64
H_B, KV_B = 8, 2
H_C, KV_C = 16, 4
MEM_HEADS = 4
TOPK_MAX = 256
CMP_BLOCK, SEL_BLOCK, N_SEL, WINDOW = 32, 64, 16, 512
ROPE_THETA = 10000.0
LN_EPS = 1e-5
DEPTH = 4
ALPHA = (2 * DEPTH) ** 0.25
SCALE = HD ** -0.5
QSCALE = SCALE * math.log2(math.e)
NEG = -1e30
INT_MIN = -2 ** 31
VMEM_LIMIT = 56 * 1024 * 1024
PAGES_PER_STEP = 16

C_QA, C_KVA, C_QI, C_QB, C_CMP, C_SEL, C_WIN, C_MISC, C_END = (
    0, 1024, 1536, 1792, 2816, 3328, 3840, 4352, 4480)
MISC_WI = D_IDX
MISC_G = D_IDX + H_IDX


def _params(sem):
    return pltpu.CompilerParams(dimension_semantics=sem, vmem_limit_bytes=VMEM_LIMIT)


def _tile(n, pref, align):
    t = (min(pref, n) // align) * align
    while t >= align:
        if n % t == 0:
            return t
        t -= align
    return n


def _ln(y, g, b):
    mu = jnp.mean(y, axis=-1, keepdims=True)
    d = y - mu
    var = jnp.mean(d * d, axis=-1, keepdims=True)
    return d * lax.rsqrt(var + LN_EPS) * g + b


MXU = BF16


def _dot(a, b):
    return jnp.dot(a.astype(MXU), b.astype(MXU), preferred_element_type=F32)


def _dot_nt(a, b):
    return lax.dot_general(a.astype(MXU), b.astype(MXU), (((1,), (1,)), ((), ())),
                           preferred_element_type=F32)


def _call(body, *, grid, in_specs, out_specs, out_shape, scratch=(), sem, name, prefetch=None):
    if prefetch is None:
        return pl.pallas_call(body, grid=grid, in_specs=in_specs, out_specs=out_specs,
                              out_shape=out_shape, scratch_shapes=list(scratch),
                              compiler_params=_params(sem), name=name)
    gs = pltpu.PrefetchScalarGridSpec(num_scalar_prefetch=1, grid=grid, in_specs=in_specs,
                                      out_specs=out_specs, scratch_shapes=list(scratch))
    return functools.partial(
        pl.pallas_call(body, grid_spec=gs, out_shape=out_shape, compiler_params=_params(sem),
                       name=name), prefetch)


def _mm_body(x_ref, w_ref, o_ref, acc_ref, *, nk):
    part = _dot(x_ref[...], w_ref[...])
    if nk == 1:
        o_ref[...] = part
        return
    k = pl.program_id(2)

    @pl.when(k == 0)
    def _():
        acc_ref[...] = part

    @pl.when(k > 0)
    def _():
        acc_ref[...] += part

    @pl.when(k == nk - 1)
    def _():
        o_ref[...] = acc_ref[...]


def _wspec(w, layer, block, index):
    if w.ndim == 2:
        return pl.BlockSpec(block, index)
    return pl.BlockSpec((None,) + block, lambda *a: (layer,) + index(*a))


def _mm(x, w, *, layer=None, cols=None, tm=1024, tn=512, tk=2048):
    m, kd = x.shape
    col0, n = cols if cols is not None else (0, w.shape[-1])
    tm = _tile(m, tm, 8)
    tn = _tile(math.gcd(n, col0), tn, LANES)
    tk = _tile(kd, tk, LANES)
    nk = kd // tk
    cb0 = col0 // tn
    return pl.pallas_call(
        functools.partial(_mm_body, nk=nk),
        grid=(m // tm, n // tn, nk),
        in_specs=[pl.BlockSpec((tm, tk), lambda i, j, k: (i, k)),
                  _wspec(w, layer, (tk, tn), lambda i, j, k: (k, cb0 + j))],
        out_specs=pl.BlockSpec((tm, tn), lambda i, j, k: (i, j)),
        out_shape=jax.ShapeDtypeStruct((m, n), F32),
        scratch_shapes=[pltpu.VMEM((tm, tn) if nk > 1 else (8, LANES), F32)],
        compiler_params=_params(("parallel", "parallel", "arbitrary")),
        name="mm",
    )(x, w)


def _mm_ln_body(x_ref, w_ref, r_ref, g_ref, b_ref, o_ref, acc_ref, *, nk):
    k = pl.program_id(1)
    part = _dot(x_ref[...], w_ref[...])
    if nk == 1:
        o_ref[...] = _ln(ALPHA * r_ref[...] + part, g_ref[...], b_ref[...])
        return

    @pl.when(k == 0)
    def _():
        acc_ref[...] = part

    @pl.when(k > 0)
    def _():
        acc_ref[...] += part

    @pl.when(k == nk - 1)
    def _():
        o_ref[...] = _ln(ALPHA * r_ref[...] + acc_ref[...], g_ref[...], b_ref[...])


def _mm_ln(x, w, res, g, b, *, layer=None, tm=512, tk=2048):
    m, kd = x.shape
    d = w.shape[-1]
    tm = _tile(m, tm, 8)
    tk = _tile(kd, tk, LANES)
    nk = kd // tk
    return pl.pallas_call(
        functools.partial(_mm_ln_body, nk=nk),
        grid=(m // tm, nk),
        in_specs=[pl.BlockSpec((tm, tk), lambda i, k: (i, k)),
                  _wspec(w, layer, (tk, d), lambda i, k: (k, 0)),
                  pl.BlockSpec((tm, d), lambda i, k: (i, 0)),
                  pl.BlockSpec((1, d), lambda i, k: (0, 0)),
                  pl.BlockSpec((1, d), lambda i, k: (0, 0))],
        out_specs=pl.BlockSpec((tm, d), lambda i, k: (i, 0)),
        out_shape=jax.ShapeDtypeStruct((m, d), F32),
        scratch_shapes=[pltpu.VMEM((tm, d) if nk > 1 else (8, LANES), F32)],
        compiler_params=_params(("parallel", "arbitrary")),
        name="mm_ln",
    )(x, w, res, g.reshape(1, d), b.reshape(1, d))


def _post_ab_body(h_ref, c128_ref, s128_ref, c64_ref, s64_ref,
                  qa_ref, kva_ref, qi_ref, qb_ref, cmp_ref, sel_ref, win_ref, misc_ref):
    cos = c128_ref[...]
    sin = s128_ref[...]
    ci = c64_ref[...]
    si = s64_ref[...]
    lane = lax.broadcasted_iota(I32, cos.shape, 1)
    first_half = (lane & (D_IDX - 1)) < (D_IDX // 2)

    def rope128(x):
        return x * cos + pltpu.roll(x, HD // 2, 1) * sin

    def rope64(x):
        partner = jnp.where(first_half, pltpu.roll(x, LANES - D_IDX // 2, 1),
                            pltpu.roll(x, D_IDX // 2, 1))
        return x * ci + partner * si

    def head(c0, i):
        return h_ref[:, c0 + i * HD:c0 + (i + 1) * HD]

    for i in range(H_A):
        qa_ref[:, i * HD:(i + 1) * HD] = rope128(head(C_QA, i))
    for i in range(H_B):
        qb_ref[:, i * HD:(i + 1) * HD] = rope128(head(C_QB, i))
    for src, dst, kv in ((C_KVA, kva_ref, KV_A), (C_SEL, sel_ref, KV_B), (C_WIN, win_ref, KV_B)):
        for i in range(kv):
            dst[:, i * HD:(i + 1) * HD] = rope128(head(src, i))
            dst[:, (kv + i) * HD:(kv + i + 1) * HD] = head(src, kv + i)
    cmp_ref[...] = h_ref[:, C_CMP:C_SEL]
    for i in range(H_IDX * D_IDX // LANES):
        qi_ref[:, i * LANES:(i + 1) * LANES] = rope64(head(C_QI, i))
    raw = h_ref[:, C_MISC:C_END]
    misc_ref[...] = jnp.where(
        lane < MISC_WI, rope64(raw),
        jnp.where(lane < MISC_G, raw * (H_IDX ** -0.5),
                  jnp.where(lane < MISC_G + 3 * H_B, 1.0 / (1.0 + jnp.exp(-raw)), 0.0)))


def _post_ab(h, tabs, t):
    m = h.shape[0]
    tm = _tile(t, 256, 8)
    nt = t // tm
    widths = (H_A * HD, 2 * KV_A * HD, H_IDX * D_IDX, H_B * HD, 2 * KV_B * HD,
              2 * KV_B * HD, 2 * KV_B * HD, LANES)
    tab_spec = pl.BlockSpec((tm, LANES), lambda i: (i % nt, 0))
    return pl.pallas_call(
        _post_ab_body,
        grid=(m // tm,),
        in_specs=[pl.BlockSpec((tm, C_END), lambda i: (i, 0))] + [tab_spec] * 4,
        out_specs=[pl.BlockSpec((tm, w), lambda i: (i, 0)) for w in widths],
        out_shape=[jax.ShapeDtypeStruct((m, w), F32) for w in widths],
        compiler_params=_params(("parallel",)),
        name="post_ab",
    )(h, *tabs)


def _rope_tables(pos, d):
    half = d // 2
    inv = ROPE_THETA ** (-(jnp.arange(half, dtype=F32) * 2.0 / d))
    ang = pos.astype(F32)[:, None] * inv[None, :]
    cos, sin = jnp.cos(ang), jnp.sin(ang)
    reps = LANES // d
    return (jnp.tile(jnp.concatenate([cos, cos], axis=1), (1, reps)),
            jnp.tile(jnp.concatenate([-sin, sin], axis=1), (1, reps)))


def _sortable(x):
    bits = lax.bitcast_convert_type(jnp.where(x == 0.0, 0.0, x), I32)
    return bits ^ ((bits >> 31) & 0x7FFFFFFF)


def _dsa_index_body(*refs, tq, wk, n_kx, n_main, tail_w, topk, pos0, n_bucket):
    refs = list(refs)
    if tail_w:
        refs.pop(0)
    qi_ref, wi_ref = refs[0], refs[1]
    kx_refs = refs[2:2 + n_kx]
    i = 2 + n_kx
    tail_ref = None
    if tail_w:
        tail_ref = refs[i]
        i += 1
    bias_ref, key_ref, kth_ref, need_ref = refs[i:i + 4]
    qb = pl.program_id(1)
    j = pl.program_id(2)
    nch = key_ref.shape[0]
    nj = n_main + (1 if tail_w else 0)
    last_q = pos0 + (qb + 1) * tq - 1

    def put_keys(kx, chunk0, width, kpos0):
        qi = qi_ref[0]
        wi = wi_ref[0]
        score = jnp.zeros((tq, width), F32)
        for h in range(H_IDX):
            dots = _dot_nt(qi[:, h * D_IDX:(h + 1) * D_IDX], kx) * (D_IDX ** -0.5)
            score = score + wi[:, MISC_WI + h:MISC_WI + h + 1] * jnp.maximum(dots, 0.0)
        qpos = pos0 + qb * tq + lax.broadcasted_iota(I32, (tq, width), 0)
        kpos = kpos0 + lax.broadcasted_iota(I32, (tq, width), 1)
        keys = jnp.where(kpos <= qpos, _sortable(score), INT_MIN)
        for u in range(width // LANES):
            key_ref[chunk0 + u] = keys[:, u * LANES:(u + 1) * LANES]

    def main_step():
        kx = jnp.concatenate([r[0][:, :D_IDX] for r in kx_refs], axis=0)
        put_keys(kx, j * (wk // LANES), wk, j * wk)

    if tail_w:
        pl.when(j < n_main)(main_step)

        @pl.when(j == n_main)
        def _():
            put_keys(tail_ref[0][:, :D_IDX], n_main * (wk // LANES), tail_w, n_main * wk)
    else:
        pl.when(j * wk <= last_q)(main_step)

        @pl.when(j * wk > last_q)
        def _():
            for u in range(wk // LANES):
                key_ref[j * (wk // LANES) + u] = jnp.full((tq, LANES), INT_MIN, I32)

    def search(lim):
        n_part = max(1, min(4, tq // 64))
        pr = tq // n_part

        def count(pred, p):
            hit = pred(key_ref[0:lim, p * pr:(p + 1) * pr, :]).astype(I32)
            return jnp.sum(jnp.sum(hit, axis=0), axis=1, keepdims=True)

        ans0 = tuple(jnp.where(count(lambda k: k >= 0, p) >= topk, 0, INT_MIN)
                     for p in range(n_part))

        def bit_step(b, ans):
            bit = jnp.left_shift(jnp.int32(1), 30 - b)
            out = []
            for p in range(n_part):
                cand = ans[p] | bit
                out.append(jnp.where(count(lambda k: k >= cand, p) >= topk, cand, ans[p]))
            return tuple(out)

        kth = lax.fori_loop(0, 31, bit_step, ans0)
        for p in range(n_part):
            rows = slice(p * pr, (p + 1) * pr)
            kth_ref[rows, :] = jnp.broadcast_to(kth[p], (pr, LANES))
            need_ref[rows, :] = jnp.broadcast_to(
                (topk - count(lambda k: k > kth[p], p)).astype(F32), (pr, LANES))

    @pl.when(j == nj - 1)
    def _():
        n_live = last_q // LANES + 1
        lims = [nch * (i + 1) // n_bucket for i in range(n_bucket)]
        for i, lim in enumerate(lims):
            lo = lims[i - 1] if i else 0
            pl.when((n_live > lo) & ((n_live <= lim) | (i == n_bucket - 1)))(
                functools.partial(search, lim))
        kth = kth_ref[...]
        need = need_ref[...]
        r_i = lax.broadcasted_iota(I32, (LANES, LANES), 0)
        c_i = lax.broadcasted_iota(I32, (LANES, LANES), 1)
        strict = jnp.where(r_i < c_i, 1.0, 0.0).astype(BF16)
        ones = jnp.ones((LANES, LANES), BF16)
        seen = jnp.zeros((tq, LANES), F32)
        for c in range(nch):
            kc = key_ref[c]
            tie = jnp.where(kc == kth, 1.0, 0.0).astype(BF16)
            rank = seen + jnp.dot(tie, strict, preferred_element_type=F32)
            tie_bias = jnp.where(kc == kth, jnp.where(rank < need, 0.0, NEG), NEG)
            take = jnp.where(kc > kth, 0.0, tie_bias)
            bias_ref[0, 0, :, c * LANES:(c + 1) * LANES] = (
                jnp.where(kc > INT_MIN, take, NEG).astype(bias_ref.dtype))
            seen = seen + jnp.dot(tie, ones, preferred_element_type=F32)


def _dsa_index(qi, misc, kx, *, topk, pos0, tq, tk, page=None):
    b, t, _ = qi.shape
    nq = t // tq
    if page is not None:
        pt, off, tail = page
        n_pages = pt.shape[1]
        n_kx = _tile(n_pages, PAGES_PER_STEP, 1)
        n_main = n_pages // n_kx
        tail_w = tk
        wk = n_kx * tk
        ix = lambda f: (lambda bi, qb, j, p: f(bi, qb, j))
        kx_specs = [pl.BlockSpec(
            (1, tk, kx.shape[2]),
            functools.partial(lambda u, bi, qb, j, p:
                              (off + p[bi, jnp.minimum(j, n_main - 1) * n_kx + u], 0, 0), u))
            for u in range(n_kx)]
        extra = [pl.BlockSpec((1, tk, tail.shape[2]), ix(lambda bi, qb, j: (bi, 0, 0)))]
        args = [qi, misc] + [kx] * n_kx + [tail]
        prefetch = pt
        s_pad = n_pages * tk + tail_w
    else:
        n_kx, n_main, tail_w, wk = 1, kx.shape[1] // tk, 0, tk
        ix = lambda f: f
        kx_specs = [pl.BlockSpec((1, tk, kx.shape[2]), lambda bi, qb, j: (bi, j, 0))]
        extra = []
        args = [qi, misc, kx]
        prefetch = None
        s_pad = kx.shape[1]
    nj = n_main + (1 if tail_w else 0)
    dtype = BF16 if tq % 16 == 0 else F32
    in_specs = [pl.BlockSpec((1, tq, qi.shape[2]), ix(lambda bi, qb, j: (bi, qb, 0))),
                pl.BlockSpec((1, tq, LANES), ix(lambda bi, qb, j: (bi, qb, 0)))] + kx_specs + extra
    body = functools.partial(_dsa_index_body, tq=tq, wk=wk, n_kx=n_kx, n_main=n_main,
                             tail_w=tail_w, topk=topk, pos0=pos0,
                             n_bucket=min(4, nq) if not tail_w else 1)
    return _call(body, grid=(b, nq, nj), in_specs=in_specs,
                 out_specs=pl.BlockSpec((1, 1, tq, s_pad), ix(lambda bi, qb, j: (bi, 0, qb, 0))),
                 out_shape=jax.ShapeDtypeStruct((b, 1, t, s_pad), dtype),
                 scratch=[pltpu.VMEM((s_pad // LANES, tq, LANES), I32),
                          pltpu.VMEM((tq, LANES), I32), pltpu.VMEM((tq, LANES), F32)],
                 sem=("parallel", "parallel", "arbitrary"), name="dsa_index",
                 prefetch=prefetch)(*args)


def _flash_init(q_ref, q_scr, m_scr, l_scr, acc_scr, groups, rep, tq):
    m_scr[...] = jnp.full(m_scr.shape, NEG, F32)
    l_scr[...] = jnp.zeros(l_scr.shape, F32)
    acc_scr[...] = jnp.zeros(acc_scr.shape, F32)
    for g in range(groups):
        for r in range(rep):
            h = g * rep + r
            q_scr[g, r * tq:(r + 1) * tq, :] = (q_ref[0, :, h * HD:(h + 1) * HD] * QSCALE).astype(MXU)


def _flash_step(g, k, v, bias, q_scr, m_scr, l_scr, acc_scr, rep):
    s = _dot_nt(q_scr[g], k) + jnp.concatenate([bias.astype(F32)] * rep, axis=0)
    m_prev = m_scr[g]
    m_new = jnp.maximum(m_prev, jnp.max(s, axis=1, keepdims=True))
    p = jnp.exp2(s - jnp.concatenate([m_new] * (k.shape[0] // LANES), axis=1))
    a = jnp.exp2(m_prev - m_new)
    l_scr[g] = a * l_scr[g] + jnp.sum(p, axis=1, keepdims=True)
    acc_scr[g] = a * acc_scr[g] + _dot(p, v)
    m_scr[g] = m_new


def _flash_final(o_ref, m_scr, l_scr, acc_scr, groups, rep, tq):
    for g in range(groups):
        for r in range(rep):
            h = g * rep + r
            rows = slice(r * tq, (r + 1) * tq)
            seen = m_scr[g][rows, :] > 0.5 * NEG
            o_ref[0, :, h * HD:(h + 1) * HD] = jnp.where(
                seen, acc_scr[g][rows, :] / l_scr[g][rows, :], 0.0)


def _flash_scratch(groups, rep, tq):
    rows = rep * tq
    return [pltpu.VMEM((groups, rows, HD), MXU),
            pltpu.VMEM((groups, rows, LANES), F32),
            pltpu.VMEM((groups, rows, LANES), F32),
            pltpu.VMEM((groups, rows, HD), F32)]


def _attn_body(*refs, groups, rep, tq, tk, nj, pos0, k_start, band, lead, gm):
    q_ref, kv_ref = refs[0], refs[1]
    bias_ref = None if band else refs[2]
    o_ref, q_scr, m_scr, l_scr, acc_scr = refs[2 if band else 3:]
    qb = pl.program_id(1)
    j = pl.program_id(2)
    kb = qb - lead + j if band else j

    @pl.when(j == 0)
    def _():
        _flash_init(q_ref, q_scr, m_scr, l_scr, acc_scr, groups, rep, tq)

    last_q = pos0 + qb * tq + tq - 1

    @pl.when((kb >= 0) & (k_start + kb * tk <= last_q))
    def _():
        kv = kv_ref[0]
        if band:
            qpos = pos0 + qb * tq + lax.broadcasted_iota(I32, (tq, tk), 0)
            kpos = k_start + kb * tk + lax.broadcasted_iota(I32, (tq, tk), 1)
            bias = jnp.where(kpos <= qpos, jnp.where(kpos > qpos - WINDOW, 0.0, NEG), NEG)
        for g in range(groups):
            if not band:
                bias = bias_ref[0, g if gm > 1 else 0]
            _flash_step(g, kv[:, g * HD:(g + 1) * HD],
                        kv[:, (groups + g) * HD:(groups + g + 1) * HD],
                        bias, q_scr, m_scr, l_scr, acc_scr, rep)

    @pl.when(j == nj - 1)
    def _():
        _flash_final(o_ref, m_scr, l_scr, acc_scr, groups, rep, tq)


def _attn(q, kv, *, groups, rep, tq, tk, pos0, bias=None, k_start=0):
    b, t, qw = q.shape
    band = bias is None
    cw = kv.shape[2]
    nq = t // tq
    lead = WINDOW // tk if (band and nq > 1) else 0
    nj = (lead + 1) if (band and nq > 1) else kv.shape[1] // tk

    def kblock(qb, j):
        if band:
            return jnp.maximum(qb - lead + j, 0)
        return jnp.minimum(j, (pos0 + (qb + 1) * tq - 1) // tk)

    in_specs = [pl.BlockSpec((1, tq, qw), lambda bi, qb, j: (bi, qb, 0)),
                pl.BlockSpec((1, tk, cw), lambda bi, qb, j: (bi, kblock(qb, j), 0))]
    args = [q, kv]
    gm = 1
    if not band:
        gm = bias.shape[1]
        in_specs.append(pl.BlockSpec((1, gm, tq, tk), lambda bi, qb, j: (bi, 0, qb, kblock(qb, j))))
        args.append(bias)
    body = functools.partial(_attn_body, groups=groups, rep=rep, tq=tq, tk=tk, nj=nj, pos0=pos0,
                             k_start=k_start, band=band, lead=lead, gm=gm)
    return _call(body, grid=(b, nq, nj), in_specs=in_specs,
                 out_specs=pl.BlockSpec((1, tq, qw), lambda bi, qb, j: (bi, qb, 0)),
                 out_shape=jax.ShapeDtypeStruct((b, t, qw), F32),
                 scratch=_flash_scratch(groups, rep, tq),
                 sem=("parallel", "parallel", "arbitrary"),
                 name="attn_band" if band else "attn_bias")(*args)


def _attn_paged_body(_, q_ref, *refs, groups, rep, tq, page, n_pg, n_main, gm):
    pages = refs[:n_pg]
    tail_ref, bias_ref, bias_tail_ref, o_ref, q_scr, m_scr, l_scr, acc_scr = refs[n_pg:]
    j = pl.program_id(2)
    rpt = 2 * groups

    @pl.when(j == 0)
    def _():
        _flash_init(q_ref, q_scr, m_scr, l_scr, acc_scr, groups, rep, tq)

    @pl.when(j < n_main)
    def _():
        for g in range(groups):
            k = jnp.concatenate([pg[pl.ds(g, page, stride=rpt), :] for pg in pages], axis=0)
            v = jnp.concatenate([pg[pl.ds(groups + g, page, stride=rpt), :] for pg in pages], axis=0)
            _flash_step(g, k, v, bias_ref[0, g if gm > 1 else 0], q_scr, m_scr, l_scr, acc_scr, rep)

    @pl.when(j == n_main)
    def _():
        kv = tail_ref[0]
        for g in range(groups):
            _flash_step(g, kv[:, g * HD:(g + 1) * HD],
                        kv[:, (groups + g) * HD:(groups + g + 1) * HD],
                        bias_tail_ref[0, g if gm > 1 else 0], q_scr, m_scr, l_scr, acc_scr, rep)
        _flash_final(o_ref, m_scr, l_scr, acc_scr, groups, rep, tq)


def _attn_paged(q, pool, pt, off, tail, bias, *, groups, rep, page):
    b, t, qw = q.shape
    n_pages = pt.shape[1]
    n_pg = _tile(n_pages, PAGES_PER_STEP, 1)
    n_main = n_pages // n_pg
    rpt = 2 * groups
    gm = bias.shape[1]
    ix = lambda f: (lambda bi, qb, j, p: f(bi, qb, j))
    page_specs = [pl.BlockSpec(
        (page * rpt, LANES),
        functools.partial(lambda u, bi, qb, j, p:
                          (off + p[bi, jnp.minimum(j, n_main - 1) * n_pg + u], 0), u))
        for u in range(n_pg)]
    in_specs = ([pl.BlockSpec((1, t, qw), ix(lambda bi, qb, j: (bi, 0, 0)))] + page_specs + [
        pl.BlockSpec((1, page, tail.shape[2]), ix(lambda bi, qb, j: (bi, 0, 0))),
        pl.BlockSpec((1, gm, t, n_pg * page),
                     ix(lambda bi, qb, j: (bi, 0, 0, jnp.minimum(j, n_main - 1)))),
        pl.BlockSpec((1, gm, t, page), ix(lambda bi, qb, j: (bi, 0, 0, n_pages)))])
    body = functools.partial(_attn_paged_body, groups=groups, rep=rep, tq=t, page=page,
                             n_pg=n_pg, n_main=n_main, gm=gm)
    return _call(body, grid=(b, 1, n_main + 1), in_specs=in_specs,
                 out_specs=pl.BlockSpec((1, t, qw), ix(lambda bi, qb, j: (bi, 0, 0))),
                 out_shape=jax.ShapeDtypeStruct((b, t, qw), F32),
                 scratch=_flash_scratch(groups, rep, t),
                 sem=("parallel", "parallel", "arbitrary"), name="attn_paged",
                 prefetch=pt)(q, *([pool] * n_pg), tail, bias, bias)


def _nsa_cmp_body(q_ref, kc_ref, cc_ref, cs_ref, oc_ref, bias_ref, *, tq, nc, ns, nsp, s_pad,
                  chunk, pos0):
    qb = pl.program_id(1)
    half = nc // 2
    rep = H_B // KV_B
    kc = kc_ref[0]
    cos = cc_ref[...]
    sin = cs_ref[...]
    qpos1 = pos0 + qb * tq + lax.broadcasted_iota(I32, (tq, 1), 0)
    n_perm = lax.broadcasted_iota(I32, (tq, nc), 1)
    n_orig = jnp.where(n_perm < half, 2 * n_perm, 2 * (n_perm - half) + 1)
    c_ok = jnp.where(((n_orig + 1) * CMP_BLOCK - 1) <= qpos1, 1.0, 0.0)
    c_okr = jnp.concatenate([c_ok] * rep, axis=0) > 0.5
    blk = lax.broadcasted_iota(I32, (tq, nsp), 1)
    sel_shift = SEL_BLOCK.bit_length() - 1
    cur = qpos1 >> sel_shift
    forced = (blk == 0) | (blk == cur) | (blk == cur - 1)
    b_causal = blk * SEL_BLOCK <= qpos1
    vals = []
    for g in range(KV_B):
        k_raw = kc[:, g * HD:(g + 1) * HD]
        k_g = k_raw * cos + pltpu.roll(k_raw, HD // 2, 1) * sin
        v_g = kc[:, (KV_B + g) * HD:(KV_B + g + 1) * HD]
        qg = jnp.concatenate(
            [q_ref[0, :, (g * rep + r) * HD:(g * rep + r + 1) * HD] for r in range(rep)], axis=0)
        s = _dot_nt(qg, k_g) * SCALE
        s = jnp.where(c_okr, s, NEG)
        m = jnp.max(s, axis=1, keepdims=True)
        p = jnp.where(c_okr, jnp.exp(s - m), 0.0)
        den = jnp.sum(p, axis=1, keepdims=True)
        p = p / jnp.where(den > 0.0, den, 1.0)
        o = _dot(p, v_g)
        imp_c = p[0:tq]
        for r in range(rep):
            oc_ref[0, :, (g * rep + r) * HD:(g * rep + r + 1) * HD] = o[r * tq:(r + 1) * tq]
            if r > 0:
                imp_c = imp_c + p[r * tq:(r + 1) * tq]
        imp = imp_c[:, :half] + imp_c[:, half:]
        if nsp > half:
            imp = jnp.concatenate([imp, jnp.zeros((tq, nsp - half), F32)], axis=1)
        vals.append(jnp.where(forced, jnp.inf, jnp.where(b_causal, imp, -jnp.inf)))

    def ranks(val, cols, idx):
        rank = jnp.zeros(val.shape, F32)
        for i in range(ns):
            low = jnp.where(idx > i, 1.0, 0.0)
            col = cols(i)
            rank = rank + jnp.where(col > val, 1.0, jnp.where(col == val, low, 0.0))
        return jnp.where(rank < min(N_SEL, ns), 1.0, 0.0)

    span = LANES // KV_B
    if nsp == LANES and ns <= span:
        packed = vals[0]
        for g in range(1, KV_B):
            packed = jnp.where(blk >= g * span, pltpu.roll(vals[g], g * span, 1), packed)

        def cols(i):
            col = vals[0][:, i:i + 1]
            for g in range(1, KV_B):
                col = jnp.where(blk >= g * span, vals[g][:, i:i + 1], col)
            return col

        picked = ranks(packed, cols, blk & (span - 1))
        chosen_g = [picked if g == 0 else pltpu.roll(picked, LANES - g * span, 1)
                    for g in range(KV_B)]
        chosen_g = [jnp.where(blk < span, c, 0.0) for c in chosen_g]
    else:
        chosen_g = [ranks(v, lambda i, v=v: v[:, i:i + 1], blk) for v in vals]

    chosen_g = [c.astype(BF16) for c in chosen_g]
    row_blk = lax.broadcasted_iota(I32, (nsp, chunk), 0)
    for c in range(s_pad // chunk):
        tok_b = c * chunk + lax.broadcasted_iota(I32, (nsp, chunk), 1)
        expand = jnp.where((tok_b >> sel_shift) == row_blk, 1.0, 0.0).astype(BF16)
        tok = c * chunk + lax.broadcasted_iota(I32, (tq, chunk), 1)
        for g in range(KV_B):
            hit = jnp.dot(chosen_g[g], expand, preferred_element_type=F32)
            bias_ref[0, g, :, c * chunk:(c + 1) * chunk] = jnp.where(
                tok <= qpos1, jnp.where(hit > 0.5, 0.0, NEG), NEG).astype(bias_ref.dtype)


def _nsa_cmp(qb, kc, ctab, *, s_len, s_pad, pos0, tq):
    b, t, qw = qb.shape
    nc = kc.shape[1]
    ns = -(-s_len // SEL_BLOCK)
    nsp = -(-ns // LANES) * LANES
    chunk = next(c for c in (512, 384, 256, 128) if s_pad % c == 0)
    dtype = BF16 if tq % 16 == 0 else F32
    body = functools.partial(_nsa_cmp_body, tq=tq, nc=nc, ns=ns, nsp=nsp, s_pad=s_pad,
                             chunk=chunk, pos0=pos0)
    return pl.pallas_call(
        body,
        grid=(b, t // tq),
        in_specs=[pl.BlockSpec((1, tq, qw), lambda bi, qi: (bi, qi, 0)),
                  pl.BlockSpec((1, nc, kc.shape[2]), lambda bi, qi: (bi, 0, 0)),
                  pl.BlockSpec((nc, HD), lambda bi, qi: (0, 0)),
                  pl.BlockSpec((nc, HD), lambda bi, qi: (0, 0))],
        out_specs=[pl.BlockSpec((1, tq, qw), lambda bi, qi: (bi, qi, 0)),
                   pl.BlockSpec((1, KV_B, tq, s_pad), lambda bi, qi: (bi, 0, qi, 0))],
        out_shape=[jax.ShapeDtypeStruct((b, t, qw), F32),
                   jax.ShapeDtypeStruct((b, KV_B, t, s_pad), dtype)],
        compiler_params=_params(("parallel", "parallel")),
        name="nsa_cmp",
    )(qb, kc, *ctab)


def _cmp_rows_body(_, *refs, per_step, per_page, rpt):
    o_ref = refs[per_step]
    stride = CMP_BLOCK * rpt
    for i in range(stride):
        piece = jnp.concatenate([pg[pl.ds(i, per_page, stride=stride), :] for pg in refs[:per_step]],
                                axis=0)
        o_ref[0, :, i * LANES:(i + 1) * LANES] = piece


def _cmp_rows(pool, pt, off, *, page, rpt, per_step=PAGES_PER_STEP):
    b, n_pages = pt.shape
    per_step = _tile(n_pages, per_step, 1)
    per_page = page // CMP_BLOCK
    width = CMP_BLOCK * rpt * LANES

    def spec(u):
        return pl.BlockSpec((page * rpt, LANES),
                            lambda bi, i, p: (off + p[bi, i * per_step + u], 0))

    return _call(functools.partial(_cmp_rows_body, per_step=per_step, per_page=per_page, rpt=rpt),
                 grid=(b, n_pages // per_step), in_specs=[spec(u) for u in range(per_step)],
                 out_specs=pl.BlockSpec((1, per_step * per_page, width), lambda bi, i, p: (bi, i, 0)),
                 out_shape=jax.ShapeDtypeStruct((b, n_pages * per_page, width), F32),
                 sem=("parallel", "arbitrary"), name="cmp_rows",
                 prefetch=pt)(*([pool] * per_step))


def _mix_ln_body(oa_ref, oc_ref, os_ref, ow_ref, misc_ref, w_ref, r_ref, g_ref, b_ref, o_ref):
    gate = misc_ref[...]
    gated = []
    for h in range(H_B):
        c0 = MISC_G + 3 * h
        sl = slice(h * HD, (h + 1) * HD)
        gated.append(gate[:, c0:c0 + 1] * oc_ref[:, sl] + gate[:, c0 + 1:c0 + 2] * os_ref[:, sl]
                     + gate[:, c0 + 2:c0 + 3] * ow_ref[:, sl])
    wa = H_A * HD
    y = _dot(oa_ref[...], w_ref[0:wa, :]) + _dot(jnp.concatenate(gated, axis=1), w_ref[wa:, :])
    o_ref[...] = _ln(ALPHA * r_ref[...] + y, g_ref[...], b_ref[...])


def _mix_ln(o_a, o_c, o_s, o_w, misc, w, res, g, b, *, layer, tm=512):
    m = o_a.shape[0]
    d = w.shape[-1]
    tm = _tile(m, tm, 8)
    wa, wb = H_A * HD, H_B * HD
    return pl.pallas_call(
        _mix_ln_body,
        grid=(m // tm,),
        in_specs=[pl.BlockSpec((tm, wa), lambda i: (i, 0))]
        + [pl.BlockSpec((tm, wb), lambda i: (i, 0))] * 3
        + [pl.BlockSpec((tm, LANES), lambda i: (i, 0)),
           _wspec(w, layer, (wa + wb, d), lambda i: (0, 0)),
           pl.BlockSpec((tm, d), lambda i: (i, 0)),
           pl.BlockSpec((1, d), lambda i: (0, 0)),
           pl.BlockSpec((1, d), lambda i: (0, 0))],
        out_specs=pl.BlockSpec((tm, d), lambda i: (i, 0)),
        out_shape=jax.ShapeDtypeStruct((m, d), F32),
        compiler_params=_params(("parallel",)),
        name="mix_ln",
    )(o_a, o_c, o_s, o_w, misc, w, res, g.reshape(1, d), b.reshape(1, d))


def _sb_init(q_ref, q_scr, c_scr, acc_scr, tq):
    rep = H_C // KV_C
    c_scr[...] = jnp.zeros(c_scr.shape, F32)
    acc_scr[...] = jnp.zeros(acc_scr.shape, F32)
    for g in range(KV_C):
        for r in range(rep):
            h = g * rep + r
            q_scr[g, r * tq:(r + 1) * tq, :] = (q_ref[0, :, h * HD:(h + 1) * HD] * QSCALE).astype(MXU)


def _sb_block(ks, vs, before, q_scr, c_scr, acc_scr):
    rep = H_C // KV_C
    rows = q_scr.shape[1]
    nchunk = ks[0].shape[0] // LANES
    older = None if before is None else jnp.concatenate([before] * rep, axis=0) > 0.5
    ys, pieces = [], []
    for g in range(KV_C):
        y = _dot_nt(q_scr[g], ks[g])
        drop = jnp.maximum(y, 0.0) + jnp.log2(1.0 + jnp.exp2(-jnp.abs(y)))
        if older is not None:
            drop = jnp.where(older, drop, 0.0)
        ys.append(y)
        pieces += [drop[:, c * LANES:(c + 1) * LANES] for c in range(nchunk)]
    stacked = jnp.concatenate(pieces, axis=0)
    hi = stacked.astype(BF16)
    lo = (stacked - hi.astype(F32)).astype(BF16)
    r_i = lax.broadcasted_iota(I32, (LANES, LANES), 0)
    c_i = lax.broadcasted_iota(I32, (LANES, LANES), 1)
    later = jnp.where(r_i >= c_i, -1.0, 0.0).astype(BF16)
    incl = jnp.dot(jnp.concatenate([hi, lo], axis=1), jnp.concatenate([later, later], axis=0),
                   preferred_element_type=F32)
    for g in range(KV_C):
        run = c_scr[g]
        cols = [None] * nchunk
        for c in reversed(range(nchunk)):
            part = incl[(g * nchunk + c) * rows:(g * nchunk + c + 1) * rows]
            cols[c] = part + run
            run = run + jnp.broadcast_to(part[:, :1], part.shape)
        a = jnp.exp2(ys[g] + jnp.concatenate(cols, axis=1))
        if older is not None:
            a = jnp.where(older, a, 0.0)
        acc_scr[g] = acc_scr[g] + _dot(a, vs[g])
        c_scr[g] = run


def _sb_final(o_ref, acc_scr, tq):
    rep = H_C // KV_C
    for g in range(KV_C):
        for r in range(rep):
            h = g * rep + r
            o_ref[0, :, h * HD:(h + 1) * HD] = acc_scr[g][r * tq:(r + 1) * tq, :]


def _sb_scratch(tq):
    rows = (H_C // KV_C) * tq
    return [pltpu.VMEM((KV_C, rows, HD), MXU),
            pltpu.VMEM((KV_C, rows, LANES), F32),
            pltpu.VMEM((KV_C, rows, HD), F32)]


def _split_kv(kv):
    ks = [kv[:, g * HD:(g + 1) * HD] for g in range(KV_C)]
    vs = [kv[:, (KV_C + g) * HD:(KV_C + g + 1) * HD] for g in range(KV_C)]
    return ks, vs


def _older_mask(tq, tk, qpos0, kpos0):
    qpos = qpos0 + lax.broadcasted_iota(I32, (tq, tk), 0)
    kpos = kpos0 + lax.broadcasted_iota(I32, (tq, tk), 1)
    return jnp.where(kpos < qpos, 1.0, 0.0)


def _sb_body(q_ref, kv_ref, o_ref, q_scr, c_scr, acc_scr, *, tq, tk, nj, pos0):
    qb = pl.program_id(1)
    j = pl.program_id(2)
    kb = (pos0 + (qb + 1) * tq - 1) // tk - j
    q0 = pos0 + qb * tq

    @pl.when(j == 0)
    def _():
        _sb_init(q_ref, q_scr, c_scr, acc_scr, tq)

    @pl.when((kb >= 0) & ((kb + 1) * tk > q0))
    def _():
        ks, vs = _split_kv(kv_ref[0])
        _sb_block(ks, vs, _older_mask(tq, tk, q0, kb * tk), q_scr, c_scr, acc_scr)

    @pl.when((kb >= 0) & ((kb + 1) * tk <= q0))
    def _():
        ks, vs = _split_kv(kv_ref[0])
        _sb_block(ks, vs, None, q_scr, c_scr, acc_scr)

    @pl.when(j == nj - 1)
    def _():
        _sb_final(o_ref, acc_scr, tq)


def _sb_attn(q, kv, *, tq, tk, pos0):
    b, t, qw = q.shape
    cw = kv.shape[2]
    nj = kv.shape[1] // tk
    body = functools.partial(_sb_body, tq=tq, tk=tk, nj=nj, pos0=pos0)
    return _call(
        body, grid=(b, t // tq, nj),
        in_specs=[pl.BlockSpec((1, tq, qw), lambda bi, qb, j: (bi, qb, 0)),
                  pl.BlockSpec((1, tk, cw), lambda bi, qb, j: (
                      bi, jnp.maximum((pos0 + (qb + 1) * tq - 1) // tk - j, 0), 0))],
        out_specs=pl.BlockSpec((1, tq, qw), lambda bi, qb, j: (bi, qb, 0)),
        out_shape=jax.ShapeDtypeStruct((b, t, qw), F32), scratch=_sb_scratch(tq),
        sem=("parallel", "parallel", "arbitrary"), name="sb_attn")(q, kv)


def _sb_paged_body(_, q_ref, *refs, tq, page, n_pg, n_main, pos0):
    pages = refs[:n_pg]
    tail_ref, o_ref, q_scr, c_scr, acc_scr = refs[n_pg:]
    j = pl.program_id(2)
    rpt = 2 * KV_C

    @pl.when(j == 0)
    def _():
        _sb_init(q_ref, q_scr, c_scr, acc_scr, tq)
        ks, vs = _split_kv(tail_ref[0])
        _sb_block(ks, vs, _older_mask(tq, page, pos0, n_main * n_pg * page), q_scr, c_scr, acc_scr)

    @pl.when(j > 0)
    def _():
        ks = [jnp.concatenate([pg[pl.ds(g, page, stride=rpt), :] for pg in pages], axis=0)
              for g in range(KV_C)]
        vs = [jnp.concatenate([pg[pl.ds(KV_C + g, page, stride=rpt), :] for pg in pages], axis=0)
              for g in range(KV_C)]
        _sb_block(ks, vs, None, q_scr, c_scr, acc_scr)

    @pl.when(j == n_main)
    def _():
        _sb_final(o_ref, acc_scr, tq)


def _sb_attn_paged(q, pool, pt, off, tail, *, page, pos0, pages_per_step=8):
    b, t, qw = q.shape
    n_pages = pt.shape[1]
    n_pg = _tile(n_pages, pages_per_step, 1)
    n_main = n_pages // n_pg
    rpt = 2 * KV_C
    ix = lambda f: (lambda bi, qb, j, p: f(bi, qb, j))
    page_specs = [pl.BlockSpec(
        (page * rpt, LANES),
        functools.partial(lambda u, bi, qb, j, p:
                          (off + p[bi, jnp.clip(n_main - j, 0, n_main - 1) * n_pg + u], 0), u))
        for u in range(n_pg)]
    in_specs = ([pl.BlockSpec((1, t, qw), ix(lambda bi, qb, j: (bi, 0, 0)))] + page_specs
                + [pl.BlockSpec((1, page, tail.shape[2]), ix(lambda bi, qb, j: (bi, 0, 0)))])
    body = functools.partial(_sb_paged_body, tq=t, page=page, n_pg=n_pg, n_main=n_main, pos0=pos0)
    return _call(body, grid=(b, 1, n_main + 1), in_specs=in_specs,
                 out_specs=pl.BlockSpec((1, t, qw), ix(lambda bi, qb, j: (bi, 0, 0))),
                 out_shape=jax.ShapeDtypeStruct((b, t, qw), F32), scratch=_sb_scratch(t),
                 sem=("parallel", "parallel", "arbitrary"), name="sb_attn_paged",
                 prefetch=pt)(q, *([pool] * n_pg), tail)


def _mem_body(x_ref, kv_ref, wq_ref, wo_ref, g_ref, b_ref, o_ref):
    x = x_ref[0]
    kv = kv_ref[0]
    q = _dot(x, wq_ref[...])
    outs = []
    for h in range(MEM_HEADS):
        s = _dot_nt(q[:, h * HD:(h + 1) * HD], kv[:, h * HD:(h + 1) * HD]) * SCALE
        p = jnp.exp(s - jnp.max(s, axis=1, keepdims=True))
        p = p / jnp.sum(p, axis=1, keepdims=True)
        outs.append(_dot(p, kv[:, (MEM_HEADS + h) * HD:(MEM_HEADS + h + 1) * HD]))
    y = _dot(jnp.concatenate(outs, axis=1), wo_ref[...])
    o_ref[0] = _ln(ALPHA * x + y, g_ref[...], b_ref[...])


def _mem_layer(x, mem_kv, wq, wo, g, b, *, layer=None, tq):
    bsz, t, d = x.shape
    nm, cw = mem_kv.shape[1:]
    return pl.pallas_call(
        _mem_body,
        grid=(bsz, t // tq),
        in_specs=[pl.BlockSpec((1, tq, d), lambda bi, qi: (bi, qi, 0)),
                  pl.BlockSpec((1, nm, cw), lambda bi, qi: (bi, 0, 0)),
                  _wspec(wq, layer, wq.shape[-2:], lambda bi, qi: (0, 0)),
                  _wspec(wo, layer, wo.shape[-2:], lambda bi, qi: (0, 0)),
                  pl.BlockSpec((1, d), lambda bi, qi: (0, 0)),
                  pl.BlockSpec((1, d), lambda bi, qi: (0, 0))],
        out_specs=pl.BlockSpec((1, tq, d), lambda bi, qi: (bi, qi, 0)),
        out_shape=jax.ShapeDtypeStruct((bsz, t, d), F32),
        compiler_params=_params(("parallel", "parallel")),
        name="mem_layer",
    )(x, mem_kv, wq, wo, g.reshape(1, d), b.reshape(1, d))


def _ffn_body(x_ref, wg_ref, wu_ref, wd_ref, g_ref, b_ref, o_ref, xb_ref, acc_ref, *, nf):
    f = pl.program_id(1)

    @pl.when(f == 0)
    def _():
        xb_ref[...] = x_ref[...].astype(MXU)
        acc_ref[...] = jnp.zeros(acc_ref.shape, F32)

    xb = xb_ref[...]
    gate = jnp.dot(xb, wg_ref[...], preferred_element_type=F32)
    up = jnp.dot(xb, wu_ref[...], preferred_element_type=F32)
    hidden = gate / (1.0 + jnp.exp(-gate)) * up
    acc_ref[...] += _dot(hidden, wd_ref[...])

    @pl.when(f == nf - 1)
    def _():
        o_ref[...] = _ln(ALPHA * x_ref[...] + acc_ref[...], g_ref[...], b_ref[...])


def _ffn_layer(x, w_gu, w_d, g, b, *, layer=None, tm=512, tf=512):
    m, d = x.shape
    dff = w_d.shape[-2]
    tm = _tile(m, tm, 8)
    tf = _tile(dff, tf, LANES)
    nf = dff // tf
    return pl.pallas_call(
        functools.partial(_ffn_body, nf=nf),
        grid=(m // tm, nf),
        in_specs=[pl.BlockSpec((tm, d), lambda i, f: (i, 0)),
                  _wspec(w_gu, layer, (d, tf), lambda i, f: (0, f)),
                  _wspec(w_gu, layer, (d, tf), lambda i, f: (0, nf + f)),
                  _wspec(w_d, layer, (tf, d), lambda i, f: (f, 0)),
                  pl.BlockSpec((1, d), lambda i, f: (0, 0)),
                  pl.BlockSpec((1, d), lambda i, f: (0, 0))],
        out_specs=pl.BlockSpec((tm, d), lambda i, f: (i, 0)),
        out_shape=jax.ShapeDtypeStruct((m, d), F32),
        scratch_shapes=[pltpu.VMEM((tm, d), MXU), pltpu.VMEM((tm, d), F32)],
        compiler_params=_params(("parallel", "arbitrary")),
        name="ffn_layer",
    )(x, w_gu, w_gu, w_d, g.reshape(1, d), b.reshape(1, d))


def _even_odd_rows(a):
    b, n2, c = a.shape
    return a.reshape(b, n2 // 2, 2, c).transpose(0, 2, 1, 3).reshape(b, n2, c)


def _pad_rows(a, rows):
    return jnp.pad(a, ((0, 0), (0, rows - a.shape[1]), (0, 0)))


def _ab_mixer(x, w, layer, past, *, pos0):
    b, t, d = x.shape
    m = b * t
    pos = pos0 + jnp.arange(t, dtype=I32)
    tabs = _rope_tables(pos, HD) + _rope_tables(pos, D_IDX)
    h = _mm(x.reshape(m, d), w["in_ab"][layer], tn=640)
    q_a, kv_a, q_i, q_b, cmp_kv, sel_kv, win_kv, misc = _post_ab(h, tabs, t)
    r3 = lambda a: a.reshape(b, t, a.shape[1])
    q_a, kv_a, q_i, q_b, cmp_kv, sel_kv, win_kv, misc = map(
        r3, (q_a, kv_a, q_i, q_b, cmp_kv, sel_kv, win_kv, misc))
    cw = 2 * KV_B * HD
    rep_a, rep_b = H_A // KV_A, H_B // KV_B
    if past is None:
        s_len = t
        tq = _tile(t, 256, 8)
        tk = _tile(t, 1024, LANES)
        topk = min(TOPK_MAX, s_len // 4)
        bias_a = _dsa_index(q_i, misc, misc, topk=topk, pos0=0, tq=tq, tk=_tile(t, 512, LANES))
        o_a = _attn(q_a, kv_a, groups=KV_A, rep=rep_a, tq=tq, tk=tk, pos0=0, bias=bias_a)
        nc = s_len // CMP_BLOCK
        kc = _mm(cmp_kv.reshape(m // CMP_BLOCK, CMP_BLOCK * cw), w["phi"][layer], tn=cw)
        kc = _even_odd_rows(kc.reshape(b, nc, cw))
        s_pad = s_len
        o_w = _attn(q_b, win_kv, groups=KV_B, rep=rep_b, tq=tq, tk=tq, pos0=0)
        sel_attn = lambda bias: _attn(q_b, sel_kv, groups=KV_B, rep=rep_b, tq=tq, tk=tk,
                                      pos0=0, bias=bias)
        win_state = win_kv[:, t - min(WINDOW, t):]
    else:
        pt, caches, win_prev = past
        n_pages = pt.shape[1]
        page = caches["page"]
        n_pool = caches["n_pool"]
        off = layer * n_pool
        s_len = n_pages * page + t
        s_pad = (n_pages + 1) * page
        tq = t
        topk = min(TOPK_MAX, s_len // 4)
        bias_a = _dsa_index(q_i, misc, caches["a_kidx"], topk=topk, pos0=pos0, tq=tq, tk=page,
                            page=(pt, off, _pad_rows(misc[:, :, :D_IDX], page)))
        o_a = _attn_paged(q_a, caches["a_kv"], pt, off, _pad_rows(kv_a, page), bias_a,
                          groups=KV_A, rep=rep_a, page=page)
        nc = s_len // CMP_BLOCK
        per_page = page // CMP_BLOCK
        assert nc == n_pages * per_page, "new rows must not complete a compressed block"
        raw = _cmp_rows(caches["cmp"], pt, off, page=page, rpt=cw // LANES)
        kc = _mm(raw.reshape(b * nc, raw.shape[2]), w["phi"][layer], tn=cw)
        kc = _even_odd_rows(kc.reshape(b, nc, cw))
        win_all = jnp.concatenate([win_prev, win_kv], axis=1)
        k_start = n_pages * page - win_prev.shape[1]
        o_w = _attn(q_b, _pad_rows(win_all, -(-win_all.shape[1] // page) * page),
                    groups=KV_B, rep=rep_b, tq=tq, tk=page, pos0=pos0, k_start=k_start)
        sel_attn = lambda bias: _attn_paged(q_b, caches["sel"], pt, off, _pad_rows(sel_kv, page),
                                            bias, groups=KV_B, rep=rep_b, page=page)
        win_state = win_all[:, win_all.shape[1] - win_prev.shape[1]:]
    assert nc % 2 == 0
    c_pos = (jnp.arange(nc, dtype=I32) + 1) * CMP_BLOCK - 1
    ctab = tuple(_even_odd_rows(tb[None])[0] for tb in _rope_tables(c_pos, HD))
    o_c, bias_s = _nsa_cmp(q_b, kc, ctab, s_len=s_len, s_pad=s_pad, pos0=pos0, tq=tq)
    o_s = sel_attn(bias_s)
    f2 = lambda a: a.reshape(m, a.shape[2])
    pieces = (f2(o_a), f2(o_c), f2(o_s), f2(o_w), f2(misc))
    states = (kv_a, misc[:, :, :D_IDX], cmp_kv, sel_kv, win_state)
    return pieces, states


def _sb_mixer(x, w, layer, past, *, pos0):
    b, t, d = x.shape
    m = b * t
    x2 = x.reshape(m, d)
    q = _mm(x2, w["in_c"], layer=layer, cols=(0, H_C * HD)).reshape(b, t, H_C * HD)
    kv = _mm(x2, w["in_c"], layer=layer,
             cols=(H_C * HD, 2 * KV_C * HD)).reshape(b, t, 2 * KV_C * HD)
    if past is None:
        tq = _tile(t, 256, 8)
        o = _sb_attn(q, kv, tq=tq, tk=tq, pos0=0)
    else:
        pt, caches = past
        page = caches["page"]
        o = _sb_attn_paged(q, caches["c_kv"], pt, layer * caches["n_pool"], _pad_rows(kv, page),
                           page=page, pos0=pos0)
    return o.reshape(m, H_C * HD), kv


def _run_group(x, w, mem_kvs, pasts, ln_g, ln_b, *, pos0):
    b, t, d = x.shape
    m = b * t
    even, odd = [], []
    for i in range(DEPTH):
        if i % 2 == 0:
            pieces, st = _ab_mixer(x, w, i // 2, pasts[i], pos0=pos0)
            even.append(st)
            x2 = _mix_ln(*pieces, w["out"], x.reshape(m, d), ln_g[i, 0], ln_b[i, 0], layer=i)
        else:
            mix, st = _sb_mixer(x, w, i // 2, pasts[i], pos0=pos0)
            odd.append(st)
            x2 = _mm_ln(mix, w["out"], x.reshape(m, d), ln_g[i, 0], ln_b[i, 0], layer=i)
        x3 = _mem_layer(x2.reshape(b, t, d), mem_kvs[i], w["mem_q"], w["mem_o"],
                        ln_g[i, 1], ln_b[i, 1], layer=i, tq=_tile(t, 256, 8))
        x = _ffn_layer(x3.reshape(m, d), w["gate_up"], w["down"],
                       ln_g[i, 2], ln_b[i, 2], layer=i).reshape(b, t, d)
    return x, even, odd


def _reorder_in_ab(w):
    sizes = (H_A * HD, 2 * KV_A * HD, H_IDX * D_IDX, D_IDX, H_IDX,
             H_B * HD, 2 * KV_B * HD, 2 * KV_B * HD, 2 * KV_B * HD, 3 * H_B)
    offs = np.concatenate([[0], np.cumsum(sizes)])
    piece = lambda i: w[..., offs[i]:offs[i + 1]]
    order = [0, 1, 2, 5, 6, 7, 8, 3, 4, 9]
    cols = [piece(i) for i in order]
    pad = C_END - int(offs[-1])
    cols.append(jnp.zeros(w.shape[:-1] + (pad,), w.dtype))
    return jnp.concatenate(cols, axis=-1)


def _phi_matrix(w_phi):
    phi = w_phi.reshape(2, CMP_BLOCK, HD, HD)
    nb = 2 * KV_B
    per_piece = jnp.stack([phi[c // KV_B] for c in range(nb)], axis=1)
    same = jnp.eye(nb, dtype=bool)[None, :, None, :, None]
    out = jnp.where(same, per_piece[:, :, :, None, :], jnp.zeros((), w_phi.dtype))
    return out.reshape(CMP_BLOCK * nb * HD, nb * HD)


def kernel(x_prompt, x_sample, cache_a_kv, cache_a_kidx, cache_b_cmp_kv, cache_b_sel_kv,
           state_b_win_kv, cache_c_kv, cache_mem_kv, page_table, mem_prompt, w_in_ab, w_cmp_phi,
           w_in_c, w_out, w_mem_q, w_mem_kv, w_mem_o, w_gate_up, w_down, ln_g, ln_b):
    b_p, t_p, d = x_prompt.shape
    b_s, t_s, _ = x_sample.shape
    n_even, n_pool, page = cache_a_kv.shape[:3]
    n_odd = cache_c_kv.shape[0]
    n_mem = mem_prompt.shape[1]
    past_len = page_table.shape[1] * page

    w = {
        "in_ab": [_reorder_in_ab(wl).astype(MXU) for wl in w_in_ab],
        "phi": [_phi_matrix(wl.astype(MXU)) for wl in w_cmp_phi],
        "in_c": w_in_c.astype(MXU),
        "out": w_out.astype(MXU),
        "mem_q": w_mem_q.astype(MXU),
        "mem_o": w_mem_o.astype(MXU),
        "gate_up": w_gate_up.astype(MXU),
        "down": w_down.astype(MXU),
    }
    w_mkv = w_mem_kv.astype(MXU)

    mem_flat = mem_prompt.reshape(b_p * n_mem, d)
    mem_p = [_mm(mem_flat, w_mkv, layer=i).reshape(b_p, n_mem, 2 * MEM_HEADS * HD)
             for i in range(DEPTH)]
    y_p, ev_p, od_p = _run_group(x_prompt, w, mem_p, [None] * DEPTH, ln_g, ln_b, pos0=0)

    cw_b = 2 * KV_B * HD
    caches = {
        "page": page, "n_pool": n_pool,
        "a_kv": cache_a_kv.reshape(-1, LANES),
        "a_kidx": cache_a_kidx.reshape(n_even * n_pool, page, D_IDX),
        "cmp": cache_b_cmp_kv.reshape(-1, LANES),
        "sel": cache_b_sel_kv.reshape(-1, LANES),
    }
    caches_c = {"page": page, "n_pool": n_pool, "c_kv": cache_c_kv.reshape(-1, LANES)}
    win_prev = state_b_win_kv.reshape(n_even, b_s, state_b_win_kv.shape[2], cw_b)
    pasts = []
    for i in range(DEPTH):
        if i % 2 == 0:
            pasts.append((page_table, caches, win_prev[i // 2]))
        else:
            pasts.append((page_table, caches_c))
    mem_s = [cache_mem_kv[i].reshape(b_s, n_mem, 2 * MEM_HEADS * HD) for i in range(DEPTH)]
    y_s, ev_s, od_s = _run_group(x_sample, w, mem_s, pasts, ln_g, ln_b, pos0=past_len)

    def pack(states, bsz):
        def kv5(a, g):
            return a.reshape(bsz, a.shape[1], 2, g, HD)
        return (jnp.stack([kv5(s[0], KV_A) for s in states]),
                jnp.stack([s[1] for s in states]),
                jnp.stack([kv5(s[2], KV_B) for s in states]),
                jnp.stack([kv5(s[3], KV_B) for s in states]),
                jnp.stack([kv5(s[4], KV_B) for s in states]))

    a_p, ki_p, cmp_p, sel_p, win_p = pack(ev_p, b_p)
    a_s, ki_s, cmp_s, sel_s, win_s = pack(ev_s, b_s)
    c_p = jnp.stack([s.reshape(b_p, t_p, 2, KV_C, HD) for s in od_p])
    c_s = jnp.stack([s.reshape(b_s, t_s, 2, KV_C, HD) for s in od_s])
    mem_out = jnp.stack([mk.reshape(b_p, n_mem, 2, MEM_HEADS, HD) for mk in mem_p])
    return (y_p, y_s, a_p, ki_p, cmp_p, sel_p, win_p, c_p, mem_out,
            a_s, ki_s, cmp_s, sel_s, win_s, c_s)
```

```python
import functools
import math

import jax
import jax.numpy as jnp
import numpy as np
from jax import lax
from jax.experimental import pallas as pl
from jax.experimental.pallas import tpu as pltpu

F32 = jnp.float32
BF16 = jnp.bfloat16
I32 = jnp.int32

HD = 128
LANES = 128
H_A, KV_A, H_IDX, D_IDX = 8, 2, 4, 64
H_B, KV_B = 8, 2
H_C, KV_C = 16, 4
MEM_HEADS = 4
TOPK_MAX = 256
CMP_BLOCK, SEL_BLOCK, N_SEL, WINDOW = 32, 64, 16, 512
ROPE_THETA = 10000.0
LN_EPS = 1e-5
DEPTH = 4
ALPHA = (2 * DEPTH) ** 0.25
SCALE = HD ** -0.5
QSCALE = SCALE * math.log2(math.e)
NEG = -1e30
INT_MIN = -2 ** 31
VMEM_LIMIT = 56 * 1024 * 1024
PAGES_PER_STEP = 16

C_QA, C_KVA, C_QI, C_QB, C_CMP, C_SEL, C_WIN, C_MISC, C_END = (
    0, 1024, 1536, 1792, 2816, 3328, 3840, 4352, 4480)
MISC_WI = D_IDX
MISC_G = D_IDX + H_IDX


def _params(sem):
    return pltpu.CompilerParams(dimension_semantics=sem, vmem_limit_bytes=VMEM_LIMIT)


def _tile(n, pref, align):
    t = (min(pref, n) // align) * align
    while t >= align:
        if n % t == 0:
            return t
        t -= align
    return n


def _ln(y, g, b):
    mu = jnp.mean(y, axis=-1, keepdims=True)
    d = y - mu
    var = jnp.mean(d * d, axis=-1, keepdims=True)
    return d * lax.rsqrt(var + LN_EPS) * g + b


MXU = BF16


def _dot(a, b):
    return jnp.dot(a.astype(MXU), b.astype(MXU), preferred_element_type=F32)


def _dot_nt(a, b):
    return lax.dot_general(a.astype(MXU), b.astype(MXU), (((1,), (1,)), ((), ())),
                           preferred_element_type=F32)


def _call(body, *, grid, in_specs, out_specs, out_shape, scratch=(), sem, name, prefetch=None):
    if prefetch is None:
        return pl.pallas_call(body, grid=grid, in_specs=in_specs, out_specs=out_specs,
                              out_shape=out_shape, scratch_shapes=list(scratch),
                              compiler_params=_params(sem), name=name)
    gs = pltpu.PrefetchScalarGridSpec(num_scalar_prefetch=1, grid=grid, in_specs=in_specs,
                                      out_specs=out_specs, scratch_shapes=list(scratch))
    return functools.partial(
        pl.pallas_call(body, grid_spec=gs, out_shape=out_shape, compiler_params=_params(sem),
                       name=name), prefetch)


def _mm_body(x_ref, w_ref, o_ref, acc_ref, *, nk):
    part = _dot(x_ref[...], w_ref[...])
    if nk == 1:
        o_ref[...] = part
        return
    k = pl.program_id(2)

    @pl.when(k == 0)
    def _():
        acc_ref[...] = part

    @pl.when(k > 0)
    def _():
        acc_ref[...] += part

    @pl.when(k == nk - 1)
    def _():
        o_ref[...] = acc_ref[...]


def _wspec(w, layer, block, index):
    if w.ndim == 2:
        return pl.BlockSpec(block, index)
    return pl.BlockSpec((None,) + block, lambda *a: (layer,) + index(*a))


def _mm(x, w, *, layer=None, cols=None, tm=1024, tn=512, tk=2048):
    m, kd = x.shape
    col0, n = cols if cols is not None else (0, w.shape[-1])
    tm = _tile(m, tm, 8)
    tn = _tile(math.gcd(n, col0), tn, LANES)
    tk = _tile(kd, tk, LANES)
    nk = kd // tk
    cb0 = col0 // tn
    return pl.pallas_call(
        functools.partial(_mm_body, nk=nk),
        grid=(m // tm, n // tn, nk),
        in_specs=[pl.BlockSpec((tm, tk), lambda i, j, k: (i, k)),
                  _wspec(w, layer, (tk, tn), lambda i, j, k: (k, cb0 + j))],
        out_specs=pl.BlockSpec((tm, tn), lambda i, j, k: (i, j)),
        out_shape=jax.ShapeDtypeStruct((m, n), F32),
        scratch_shapes=[pltpu.VMEM((tm, tn) if nk > 1 else (8, LANES), F32)],
        compiler_params=_params(("parallel", "parallel", "arbitrary")),
        name="mm",
    )(x, w)


def _mm_ln_body(x_ref, w_ref, r_ref, g_ref, b_ref, o_ref, acc_ref, *, nk):
    k = pl.program_id(1)
    part = _dot(x_ref[...], w_ref[...])
    if nk == 1:
        o_ref[...] = _ln(ALPHA * r_ref[...] + part, g_ref[...], b_ref[...])
        return

    @pl.when(k == 0)
    def _():
        acc_ref[...] = part

    @pl.when(k > 0)
    def _():
        acc_ref[...] += part

    @pl.when(k == nk - 1)
    def _():
        o_ref[...] = _ln(ALPHA * r_ref[...] + acc_ref[...], g_ref[...], b_ref[...])


def _mm_ln(x, w, res, g, b, *, layer=None, tm=512, tk=2048):
    m, kd = x.shape
    d = w.shape[-1]
    tm = _tile(m, tm, 8)
    tk = _tile(kd, tk, LANES)
    nk = kd // tk
    return pl.pallas_call(
        functools.partial(_mm_ln_body, nk=nk),
        grid=(m // tm, nk),
        in_specs=[pl.BlockSpec((tm, tk), lambda i, k: (i, k)),
                  _wspec(w, layer, (tk, d), lambda i, k: (k, 0)),
                  pl.BlockSpec((tm, d), lambda i, k: (i, 0)),
                  pl.BlockSpec((1, d), lambda i, k: (0, 0)),
                  pl.BlockSpec((1, d), lambda i, k: (0, 0))],
        out_specs=pl.BlockSpec((tm, d), lambda i, k: (i, 0)),
        out_shape=jax.ShapeDtypeStruct((m, d), F32),
        scratch_shapes=[pltpu.VMEM((tm, d) if nk > 1 else (8, LANES), F32)],
        compiler_params=_params(("parallel", "arbitrary")),
        name="mm_ln",
    )(x, w, res, g.reshape(1, d), b.reshape(1, d))


def _post_ab_body(h_ref, c128_ref, s128_ref, c64_ref, s64_ref,
                  qa_ref, kva_ref, qi_ref, qb_ref, cmp_ref, sel_ref, win_ref, misc_ref):
    cos = c128_ref[...]
    sin = s128_ref[...]
    ci = c64_ref[...]
    si = s64_ref[...]
    lane = lax.broadcasted_iota(I32, cos.shape, 1)
    first_half = (lane & (D_IDX - 1)) < (D_IDX // 2)

    def rope128(x):
        return x * cos + pltpu.roll(x, HD // 2, 1) * sin

    def rope64(x):
        partner = jnp.where(first_half, pltpu.roll(x, LANES - D_IDX // 2, 1),
                            pltpu.roll(x, D_IDX // 2, 1))
        return x * ci + partner * si

    def head(c0, i):
        return h_ref[:, c0 + i * HD:c0 + (i + 1) * HD]

    for i in range(H_A):
        qa_ref[:, i * HD:(i + 1) * HD] = rope128(head(C_QA, i))
    for i in range(H_B):
        qb_ref[:, i * HD:(i + 1) * HD] = rope128(head(C_QB, i))
    for src, dst, kv in ((C_KVA, kva_ref, KV_A), (C_SEL, sel_ref, KV_B), (C_WIN, win_ref, KV_B)):
        for i in range(kv):
            dst[:, i * HD:(i + 1) * HD] = rope128(head(src, i))
            dst[:, (kv + i) * HD:(kv + i + 1) * HD] = head(src, kv + i)
    cmp_ref[...] = h_ref[:, C_CMP:C_SEL]
    for i in range(H_IDX * D_IDX // LANES):
        qi_ref[:, i * LANES:(i + 1) * LANES] = rope64(head(C_QI, i))
    raw = h_ref[:, C_MISC:C_END]
    misc_ref[...] = jnp.where(
        lane < MISC_WI, rope64(raw),
        jnp.where(lane < MISC_G, raw * (H_IDX ** -0.5),
                  jnp.where(lane < MISC_G + 3 * H_B, 1.0 / (1.0 + jnp.exp(-raw)), 0.0)))


def _post_ab(h, tabs, t):
    m = h.shape[0]
    tm = _tile(t, 256, 8)
    nt = t // tm
    widths = (H_A * HD, 2 * KV_A * HD, H_IDX * D_IDX, H_B * HD, 2 * KV_B * HD,
              2 * KV_B * HD, 2 * KV_B * HD, LANES)
    tab_spec = pl.BlockSpec((tm, LANES), lambda i: (i % nt, 0))
    return pl.pallas_call(
        _post_ab_body,
        grid=(m // tm,),
        in_specs=[pl.BlockSpec((tm, C_END), lambda i: (i, 0))] + [tab_spec] * 4,
        out_specs=[pl.BlockSpec((tm, w), lambda i: (i, 0)) for w in widths],
        out_shape=[jax.ShapeDtypeStruct((m, w), F32) for w in widths],
        compiler_params=_params(("parallel",)),
        name="post_ab",
    )(h, *tabs)


def _rope_tables(pos, d):
    half = d // 2
    inv = ROPE_THETA ** (-(jnp.arange(half, dtype=F32) * 2.0 / d))
    ang = pos.astype(F32)[:, None] * inv[None, :]
    cos, sin = jnp.cos(ang), jnp.sin(ang)
    reps = LANES // d
    return (jnp.tile(jnp.concatenate([cos, cos], axis=1), (1, reps)),
            jnp.tile(jnp.concatenate([-sin, sin], axis=1), (1, reps)))


def _sortable(x):
    bits = lax.bitcast_convert_type(jnp.where(x == 0.0, 0.0, x), I32)
    return bits ^ ((bits >> 31) & 0x7FFFFFFF)


def _dsa_index_body(*refs, tq, wk, n_kx, n_main, tail_w, topk, pos0, n_bucket):
    refs = list(refs)
    if tail_w:
        refs.pop(0)
    qi_ref, wi_ref = refs[0], refs[1]
    kx_refs = refs[2:2 + n_kx]
    i = 2 + n_kx
    tail_ref = None
    if tail_w:
        tail_ref = refs[i]
        i += 1
    bias_ref, key_ref, kth_ref, need_ref = refs[i:i + 4]
    qb = pl.program_id(1)
    j = pl.program_id(2)
    nch = key_ref.shape[0]
    nj = n_main + (1 if tail_w else 0)
    last_q = pos0 + (qb + 1) * tq - 1

    def put_keys(kx, chunk0, width, kpos0):
        qi = qi_ref[0]
        wi = wi_ref[0]
        score = jnp.zeros((tq, width), F32)
        for h in range(H_IDX):
            dots = _dot_nt(qi[:, h * D_IDX:(h + 1) * D_IDX], kx) * (D_IDX ** -0.5)
            score = score + wi[:, MISC_WI + h:MISC_WI + h + 1] * jnp.maximum(dots, 0.0)
        qpos = pos0 + qb * tq + lax.broadcasted_iota(I32, (tq, width), 0)
        kpos = kpos0 + lax.broadcasted_iota(I32, (tq, width), 1)
        keys = jnp.where(kpos <= qpos, _sortable(score), INT_MIN)
        for u in range(width // LANES):
            key_ref[chunk0 + u] = keys[:, u * LANES:(u + 1) * LANES]

    def main_step():
        kx = jnp.concatenate([r[0][:, :D_IDX] for r in kx_refs], axis=0)
        put_keys(kx, j * (wk // LANES), wk, j * wk)

    if tail_w:
        pl.when(j < n_main)(main_step)

        @pl.when(j == n_main)
        def _():
            put_keys(tail_ref[0][:, :D_IDX], n_main * (wk // LANES), tail_w, n_main * wk)
    else:
        pl.when(j * wk <= last_q)(main_step)

        @pl.when(j * wk > last_q)
        def _():
            for u in range(wk // LANES):
                key_ref[j * (wk // LANES) + u] = jnp.full((tq, LANES), INT_MIN, I32)

    def search(lim):
        n_part = max(1, min(4, tq // 64))
        pr = tq // n_part

        def count(pred, p):
            hit = pred(key_ref[0:lim, p * pr:(p + 1) * pr, :]).astype(I32)
            return jnp.sum(jnp.sum(hit, axis=0), axis=1, keepdims=True)

        ans0 = tuple(jnp.where(count(lambda k: k >= 0, p) >= topk, 0, INT_MIN)
                     for p in range(n_part))

        def bit_step(b, ans):
            bit = jnp.left_shift(jnp.int32(1), 30 - b)
            out = []
            for p in range(n_part):
                cand = ans[p] | bit
                out.append(jnp.where(count(lambda k: k >= cand, p) >= topk, cand, ans[p]))
            return tuple(out)

        kth = lax.fori_loop(0, 31, bit_step, ans0)
        for p in range(n_part):
            rows = slice(p * pr, (p + 1) * pr)
            kth_ref[rows, :] = jnp.broadcast_to(kth[p], (pr, LANES))
            need_ref[rows, :] = jnp.broadcast_to(
                (topk - count(lambda k: k > kth[p], p)).astype(F32), (pr, LANES))

    @pl.when(j == nj - 1)
    def _():
        n_live = last_q // LANES + 1
        lims = [nch * (i + 1) // n_bucket for i in range(n_bucket)]
        for i, lim in enumerate(lims):
            lo = lims[i - 1] if i else 0
            pl.when((n_live > lo) & ((n_live <= lim) | (i == n_bucket - 1)))(
                functools.partial(search, lim))
        kth = kth_ref[...]
        need = need_ref[...]
        r_i = lax.broadcasted_iota(I32, (LANES, LANES), 0)
        c_i = lax.broadcasted_iota(I32, (LANES, LANES), 1)
        strict = jnp.where(r_i < c_i, 1.0, 0.0).astype(BF16)
        ones = jnp.ones((LANES, LANES), BF16)
        seen = jnp.zeros((tq, LANES), F32)
        for c in range(nch):
            kc = key_ref[c]
            tie = jnp.where(kc == kth, 1.0, 0.0).astype(BF16)
            rank = seen + jnp.dot(tie, strict, preferred_element_type=F32)
            tie_bias = jnp.where(kc == kth, jnp.where(rank < need, 0.0, NEG), NEG)
            take = jnp.where(kc > kth, 0.0, tie_bias)
            bias_ref[0, 0, :, c * LANES:(c + 1) * LANES] = (
                jnp.where(kc > INT_MIN, take, NEG).astype(bias_ref.dtype))
            seen = seen + jnp.dot(tie, ones, preferred_element_type=F32)


def _dsa_index(qi, misc, kx, *, topk, pos0, tq, tk, page=None):
    b, t, _ = qi.shape
    nq = t // tq
    if page is not None:
        pt, off, tail = page
        n_pages = pt.shape[1]
        n_kx = _tile(n_pages, PAGES_PER_STEP, 1)
        n_main = n_pages // n_kx
        tail_w = tk
        wk = n_kx * tk
        ix = lambda f: (lambda bi, qb, j, p: f(bi, qb, j))
        kx_specs = [pl.BlockSpec(
            (1, tk, kx.shape[2]),
            functools.partial(lambda u, bi, qb, j, p:
                              (off + p[bi, jnp.minimum(j, n_main - 1) * n_kx + u], 0, 0), u))
            for u in range(n_kx)]
        extra = [pl.BlockSpec((1, tk, tail.shape[2]), ix(lambda bi, qb, j: (bi, 0, 0)))]
        args = [qi, misc] + [kx] * n_kx + [tail]
        prefetch = pt
        s_pad = n_pages * tk + tail_w
    else:
        n_kx, n_main, tail_w, wk = 1, kx.shape[1] // tk, 0, tk
        ix = lambda f: f
        kx_specs = [pl.BlockSpec((1, tk, kx.shape[2]), lambda bi, qb, j: (bi, j, 0))]
        extra = []
        args = [qi, misc, kx]
        prefetch = None
        s_pad = kx.shape[1]
    nj = n_main + (1 if tail_w else 0)
    dtype = BF16 if tq % 16 == 0 else F32
    in_specs = [pl.BlockSpec((1, tq, qi.shape[2]), ix(lambda bi, qb, j: (bi, qb, 0))),
                pl.BlockSpec((1, tq, LANES), ix(lambda bi, qb, j: (bi, qb, 0)))] + kx_specs + extra
    body = functools.partial(_dsa_index_body, tq=tq, wk=wk, n_kx=n_kx, n_main=n_main,
                             tail_w=tail_w, topk=topk, pos0=pos0,
                             n_bucket=min(4, nq) if not tail_w else 1)
    return _call(body, grid=(b, nq, nj), in_specs=in_specs,
                 out_specs=pl.BlockSpec((1, 1, tq, s_pad), ix(lambda bi, qb, j: (bi, 0, qb, 0))),
                 out_shape=jax.ShapeDtypeStruct((b, 1, t, s_pad), dtype),
                 scratch=[pltpu.VMEM((s_pad // LANES, tq, LANES), I32),
                          pltpu.VMEM((tq, LANES), I32), pltpu.VMEM((tq, LANES), F32)],
                 sem=("parallel", "parallel", "arbitrary"), name="dsa_index",
                 prefetch=prefetch)(*args)


def _flash_init(q_ref, q_scr, m_scr, l_scr, acc_scr, groups, rep, tq):
    m_scr[...] = jnp.full(m_scr.shape, NEG, F32)
    l_scr[...] = jnp.zeros(l_scr.shape, F32)
    acc_scr[...] = jnp.zeros(acc_scr.shape, F32)
    for g in range(groups):
        for r in range(rep):
            h = g * rep + r
            q_scr[g, r * tq:(r + 1) * tq, :] = (q_ref[0, :, h * HD:(h + 1) * HD] * QSCALE).astype(MXU)


def _flash_step(g, k, v, bias, q_scr, m_scr, l_scr, acc_scr, rep):
    s = _dot_nt(q_scr[g], k) + jnp.concatenate([bias.astype(F32)] * rep, axis=0)
    m_prev = m_scr[g]
    m_new = jnp.maximum(m_prev, jnp.max(s, axis=1, keepdims=True))
    p = jnp.exp2(s - jnp.concatenate([m_new] * (k.shape[0] // LANES), axis=1))
    a = jnp.exp2(m_prev - m_new)
    l_scr[g] = a * l_scr[g] + jnp.sum(p, axis=1, keepdims=True)
    acc_scr[g] = a * acc_scr[g] + _dot(p, v)
    m_scr[g] = m_new


def _flash_final(o_ref, m_scr, l_scr, acc_scr, groups, rep, tq):
    for g in range(groups):
        for r in range(rep):
            h = g * rep + r
            rows = slice(r * tq, (r + 1) * tq)
            seen = m_scr[g][rows, :] > 0.5 * NEG
            o_ref[0, :, h * HD:(h + 1) * HD] = jnp.where(
                seen, acc_scr[g][rows, :] / l_scr[g][rows, :], 0.0)


def _flash_scratch(groups, rep, tq):
    rows = rep * tq
    return [pltpu.VMEM((groups, rows, HD), MXU),
            pltpu.VMEM((groups, rows, LANES), F32),
            pltpu.VMEM((groups, rows, LANES), F32),
            pltpu.VMEM((groups, rows, HD), F32)]


def _attn_body(*refs, groups, rep, tq, tk, nj, pos0, k_start, band, lead, gm):
    q_ref, kv_ref = refs[0], refs[1]
    bias_ref = None if band else refs[2]
    o_ref, q_scr, m_scr, l_scr, acc_scr = refs[2 if band else 3:]
    qb = pl.program_id(1)
    j = pl.program_id(2)
    kb = qb - lead + j if band else j

    @pl.when(j == 0)
    def _():
        _flash_init(q_ref, q_scr, m_scr, l_scr, acc_scr, groups, rep, tq)

    last_q = pos0 + qb * tq + tq - 1

    @pl.when((kb >= 0) & (k_start + kb * tk <= last_q))
    def _():
        kv = kv_ref[0]
        if band:
            qpos = pos0 + qb * tq + lax.broadcasted_iota(I32, (tq, tk), 0)
            kpos = k_start + kb * tk + lax.broadcasted_iota(I32, (tq, tk), 1)
            bias = jnp.where(kpos <= qpos, jnp.where(kpos > qpos - WINDOW, 0.0, NEG), NEG)
        for g in range(groups):
            if not band:
                bias = bias_ref[0, g if gm > 1 else 0]
            _flash_step(g, kv[:, g * HD:(g + 1) * HD],
                        kv[:, (groups + g) * HD:(groups + g + 1) * HD],
                        bias, q_scr, m_scr, l_scr, acc_scr, rep)

    @pl.when(j == nj - 1)
    def _():
        _flash_final(o_ref, m_scr, l_scr, acc_scr, groups, rep, tq)


def _attn(q, kv, *, groups, rep, tq, tk, pos0, bias=None, k_start=0):
    b, t, qw = q.shape
    band = bias is None
    cw = kv.shape[2]
    nq = t // tq
    lead = WINDOW // tk if (band and nq > 1) else 0
    nj = (lead + 1) if (band and nq > 1) else kv.shape[1] // tk

    def kblock(qb, j):
        if band:
            return jnp.maximum(qb - lead + j, 0)
        return jnp.minimum(j, (pos0 + (qb + 1) * tq - 1) // tk)

    in_specs = [pl.BlockSpec((1, tq, qw), lambda bi, qb, j: (bi, qb, 0)),
                pl.BlockSpec((1, tk, cw), lambda bi, qb, j: (bi, kblock(qb, j), 0))]
    args = [q, kv]
    gm = 1
    if not band:
        gm = bias.shape[1]
        in_specs.append(pl.BlockSpec((1, gm, tq, tk), lambda bi, qb, j: (bi, 0, qb, kblock(qb, j))))
        args.append(bias)
    body = functools.partial(_attn_body, groups=groups, rep=rep, tq=tq, tk=tk, nj=nj, pos0=pos0,
                             k_start=k_start, band=band, lead=lead, gm=gm)
    return _call(body, grid=(b, nq, nj), in_specs=in_specs,
                 out_specs=pl.BlockSpec((1, tq, qw), lambda bi, qb, j: (bi, qb, 0)),
                 out_shape=jax.ShapeDtypeStruct((b, t, qw), F32),
                 scratch=_flash_scratch(groups, rep, tq),
                 sem=("parallel", "parallel", "arbitrary"),
                 name="attn_band" if band else "attn_bias")(*args)


def _attn_paged_body(_, q_ref, *refs, groups, rep, tq, page, n_pg, n_main, gm):
    pages = refs[:n_pg]
    tail_ref, bias_ref, bias_tail_ref, o_ref, q_scr, m_scr, l_scr, acc_scr = refs[n_pg:]
    j = pl.program_id(2)
    rpt = 2 * groups

    @pl.when(j == 0)
    def _():
        _flash_init(q_ref, q_scr, m_scr, l_scr, acc_scr, groups, rep, tq)

    @pl.when(j < n_main)
    def _():
        for g in range(groups):
            k = jnp.concatenate([pg[pl.ds(g, page, stride=rpt), :] for pg in pages], axis=0)
            v = jnp.concatenate([pg[pl.ds(groups + g, page, stride=rpt), :] for pg in pages], axis=0)
            _flash_step(g, k, v, bias_ref[0, g if gm > 1 else 0], q_scr, m_scr, l_scr, acc_scr, rep)

    @pl.when(j == n_main)
    def _():
        kv = tail_ref[0]
        for g in range(groups):
            _flash_step(g, kv[:, g * HD:(g + 1) * HD],
                        kv[:, (groups + g) * HD:(groups + g + 1) * HD],
                        bias_tail_ref[0, g if gm > 1 else 0], q_scr, m_scr, l_scr, acc_scr, rep)
        _flash_final(o_ref, m_scr, l_scr, acc_scr, groups, rep, tq)


def _attn_paged(q, pool, pt, off, tail, bias, *, groups, rep, page):
    b, t, qw = q.shape
    n_pages = pt.shape[1]
    n_pg = _tile(n_pages, PAGES_PER_STEP, 1)
    n_main = n_pages // n_pg
    rpt = 2 * groups
    gm = bias.shape[1]
    ix = lambda f: (lambda bi, qb, j, p: f(bi, qb, j))
    page_specs = [pl.BlockSpec(
        (page * rpt, LANES),
        functools.partial(lambda u, bi, qb, j, p:
                          (off + p[bi, jnp.minimum(j, n_main - 1) * n_pg + u], 0), u))
        for u in range(n_pg)]
    in_specs = ([pl.BlockSpec((1, t, qw), ix(lambda bi, qb, j: (bi, 0, 0)))] + page_specs + [
        pl.BlockSpec((1, page, tail.shape[2]), ix(lambda bi, qb, j: (bi, 0, 0))),
        pl.BlockSpec((1, gm, t, n_pg * page),
                     ix(lambda bi, qb, j: (bi, 0, 0, jnp.minimum(j, n_main - 1)))),
        pl.BlockSpec((1, gm, t, page), ix(lambda bi, qb, j: (bi, 0, 0, n_pages)))])
    body = functools.partial(_attn_paged_body, groups=groups, rep=rep, tq=t, page=page,
                             n_pg=n_pg, n_main=n_main, gm=gm)
    return _call(body, grid=(b, 1, n_main + 1), in_specs=in_specs,
                 out_specs=pl.BlockSpec((1, t, qw), ix(lambda bi, qb, j: (bi, 0, 0))),
                 out_shape=jax.ShapeDtypeStruct((b, t, qw), F32),
                 scratch=_flash_scratch(groups, rep, t),
                 sem=("parallel", "parallel", "arbitrary"), name="attn_paged",
                 prefetch=pt)(q, *([pool] * n_pg), tail, bias, bias)


def _nsa_cmp_body(q_ref, kc_ref, cc_ref, cs_ref, oc_ref, bias_ref, *, tq, nc, ns, nsp, s_pad,
                  chunk, pos0):
    qb = pl.program_id(1)
    half = nc // 2
    rep = H_B // KV_B
    kc = kc_ref[0]
    cos = cc_ref[...]
    sin = cs_ref[...]
    qpos1 = pos0 + qb * tq + lax.broadcasted_iota(I32, (tq, 1), 0)
    n_perm = lax.broadcasted_iota(I32, (tq, nc), 1)
    n_orig = jnp.where(n_perm < half, 2 * n_perm, 2 * (n_perm - half) + 1)
    c_ok = jnp.where(((n_orig + 1) * CMP_BLOCK - 1) <= qpos1, 1.0, 0.0)
    c_okr = jnp.concatenate([c_ok] * rep, axis=0) > 0.5
    blk = lax.broadcasted_iota(I32, (tq, nsp), 1)
    sel_shift = SEL_BLOCK.bit_length() - 1
    cur = qpos1 >> sel_shift
    forced = (blk == 0) | (blk == cur) | (blk == cur - 1)
    b_causal = blk * SEL_BLOCK <= qpos1
    vals = []
    for g in range(KV_B):
        k_raw = kc[:, g * HD:(g + 1) * HD]
        k_g = k_raw * cos + pltpu.roll(k_raw, HD // 2, 1) * sin
        v_g = kc[:, (KV_B + g) * HD:(KV_B + g + 1) * HD]
        qg = jnp.concatenate(
            [q_ref[0, :, (g * rep + r) * HD:(g * rep + r + 1) * HD] for r in range(rep)], axis=0)
        s = _dot_nt(qg, k_g) * SCALE
        s = jnp.where(c_okr, s, NEG)
        m = jnp.max(s, axis=1, keepdims=True)
        p = jnp.where(c_okr, jnp.exp(s - m), 0.0)
        den = jnp.sum(p, axis=1, keepdims=True)
        p = p / jnp.where(den > 0.0, den, 1.0)
        o = _dot(p, v_g)
        imp_c = p[0:tq]
        for r in range(rep):
            oc_ref[0, :, (g * rep + r) * HD:(g * rep + r + 1) * HD] = o[r * tq:(r + 1) * tq]
            if r > 0:
                imp_c = imp_c + p[r * tq:(r + 1) * tq]
        imp = imp_c[:, :half] + imp_c[:, half:]
        if nsp > half:
            imp = jnp.concatenate([imp, jnp.zeros((tq, nsp - half), F32)], axis=1)
        vals.append(jnp.where(forced, jnp.inf, jnp.where(b_causal, imp, -jnp.inf)))

    def ranks(val, cols, idx):
        rank = jnp.zeros(val.shape, F32)
        for i in range(ns):
            low = jnp.where(idx > i, 1.0, 0.0)
            col = cols(i)
            rank = rank + jnp.where(col > val, 1.0, jnp.where(col == val, low, 0.0))
        return jnp.where(rank < min(N_SEL, ns), 1.0, 0.0)

    span = LANES // KV_B
    if nsp == LANES and ns <= span:
        packed = vals[0]
        for g in range(1, KV_B):
            packed = jnp.where(blk >= g * span, pltpu.roll(vals[g], g * span, 1), packed)

        def cols(i):
            col = vals[0][:, i:i + 1]
            for g in range(1, KV_B):
                col = jnp.where(blk >= g * span, vals[g][:, i:i + 1], col)
            return col

        picked = ranks(packed, cols, blk & (span - 1))
        chosen_g = [picked if g == 0 else pltpu.roll(picked, LANES - g * span, 1)
                    for g in range(KV_B)]
        chosen_g = [jnp.where(blk < span, c, 0.0) for c in chosen_g]
    else:
        chosen_g = [ranks(v, lambda i, v=v: v[:, i:i + 1], blk) for v in vals]

    chosen_g = [c.astype(BF16) for c in chosen_g]
    row_blk = lax.broadcasted_iota(I32, (nsp, chunk), 0)
    for c in range(s_pad // chunk):
        tok_b = c * chunk + lax.broadcasted_iota(I32, (nsp, chunk), 1)
        expand = jnp.where((tok_b >> sel_shift) == row_blk, 1.0, 0.0).astype(BF16)
        tok = c * chunk + lax.broadcasted_iota(I32, (tq, chunk), 1)
        for g in range(KV_B):
            hit = jnp.dot(chosen_g[g], expand, preferred_element_type=F32)
            bias_ref[0, g, :, c * chunk:(c + 1) * chunk] = jnp.where(
                tok <= qpos1, jnp.where(hit > 0.5, 0.0, NEG), NEG).astype(bias_ref.dtype)


def _nsa_cmp(qb, kc, ctab, *, s_len, s_pad, pos0, tq):
    b, t, qw = qb.shape
    nc = kc.shape[1]
    ns = -(-s_len // SEL_BLOCK)
    nsp = -(-ns // LANES) * LANES
    chunk = next(c for c in (512, 384, 256, 128) if s_pad % c == 0)
    dtype = BF16 if tq % 16 == 0 else F32
    body = functools.partial(_nsa_cmp_body, tq=tq, nc=nc, ns=ns, nsp=nsp, s_pad=s_pad,
                             chunk=chunk, pos0=pos0)
    return pl.pallas_call(
        body,
        grid=(b, t // tq),
        in_specs=[pl.BlockSpec((1, tq, qw), lambda bi, qi: (bi, qi, 0)),
                  pl.BlockSpec((1, nc, kc.shape[2]), lambda bi, qi: (bi, 0, 0)),
                  pl.BlockSpec((nc, HD), lambda bi, qi: (0, 0)),
                  pl.BlockSpec((nc, HD), lambda bi, qi: (0, 0))],
        out_specs=[pl.BlockSpec((1, tq, qw), lambda bi, qi: (bi, qi, 0)),
                   pl.BlockSpec((1, KV_B, tq, s_pad), lambda bi, qi: (bi, 0, qi, 0))],
        out_shape=[jax.ShapeDtypeStruct((b, t, qw), F32),
                   jax.ShapeDtypeStruct((b, KV_B, t, s_pad), dtype)],
        compiler_params=_params(("parallel", "parallel")),
        name="nsa_cmp",
    )(qb, kc, *ctab)


def _cmp_rows_body(_, *refs, per_step, per_page, rpt):
    o_ref = refs[per_step]
    stride = CMP_BLOCK * rpt
    for i in range(stride):
        piece = jnp.concatenate([pg[pl.ds(i, per_page, stride=stride), :] for pg in refs[:per_step]],
                                axis=0)
        o_ref[0, :, i * LANES:(i + 1) * LANES] = piece


def _cmp_rows(pool, pt, off, *, page, rpt, per_step=PAGES_PER_STEP):
    b, n_pages = pt.shape
    per_step = _tile(n_pages, per_step, 1)
    per_page = page // CMP_BLOCK
    width = CMP_BLOCK * rpt * LANES

    def spec(u):
        return pl.BlockSpec((page * rpt, LANES),
                            lambda bi, i, p: (off + p[bi, i * per_step + u], 0))

    return _call(functools.partial(_cmp_rows_body, per_step=per_step, per_page=per_page, rpt=rpt),
                 grid=(b, n_pages // per_step), in_specs=[spec(u) for u in range(per_step)],
                 out_specs=pl.BlockSpec((1, per_step * per_page, width), lambda bi, i, p: (bi, i, 0)),
                 out_shape=jax.ShapeDtypeStruct((b, n_pages * per_page, width), F32),
                 sem=("parallel", "arbitrary"), name="cmp_rows",
                 prefetch=pt)(*([pool] * per_step))


def _mix_ln_body(oa_ref, oc_ref, os_ref, ow_ref, misc_ref, w_ref, r_ref, g_ref, b_ref, o_ref):
    gate = misc_ref[...]
    gated = []
    for h in range(H_B):
        c0 = MISC_G + 3 * h
        sl = slice(h * HD, (h + 1) * HD)
        gated.append(gate[:, c0:c0 + 1] * oc_ref[:, sl] + gate[:, c0 + 1:c0 + 2] * os_ref[:, sl]
                     + gate[:, c0 + 2:c0 + 3] * ow_ref[:, sl])
    wa = H_A * HD
    y = _dot(oa_ref[...], w_ref[0:wa, :]) + _dot(jnp.concatenate(gated, axis=1), w_ref[wa:, :])
    o_ref[...] = _ln(ALPHA * r_ref[...] + y, g_ref[...], b_ref[...])


def _mix_ln(o_a, o_c, o_s, o_w, misc, w, res, g, b, *, layer, tm=512):
    m = o_a.shape[0]
    d = w.shape[-1]
    tm = _tile(m, tm, 8)
    wa, wb = H_A * HD, H_B * HD
    return pl.pallas_call(
        _mix_ln_body,
        grid=(m // tm,),
        in_specs=[pl.BlockSpec((tm, wa), lambda i: (i, 0))]
        + [pl.BlockSpec((tm, wb), lambda i: (i, 0))] * 3
        + [pl.BlockSpec((tm, LANES), lambda i: (i, 0)),
           _wspec(w, layer, (wa + wb, d), lambda i: (0, 0)),
           pl.BlockSpec((tm, d), lambda i: (i, 0)),
           pl.BlockSpec((1, d), lambda i: (0, 0)),
           pl.BlockSpec((1, d), lambda i: (0, 0))],
        out_specs=pl.BlockSpec((tm, d), lambda i: (i, 0)),
        out_shape=jax.ShapeDtypeStruct((m, d), F32),
        compiler_params=_params(("parallel",)),
        name="mix_ln",
    )(o_a, o_c, o_s, o_w, misc, w, res, g.reshape(1, d), b.reshape(1, d))


def _sb_init(q_ref, q_scr, c_scr, acc_scr, tq):
    rep = H_C // KV_C
    c_scr[...] = jnp.zeros(c_scr.shape, F32)
    acc_scr[...] = jnp.zeros(acc_scr.shape, F32)
    for g in range(KV_C):
        for r in range(rep):
            h = g * rep + r
            q_scr[g, r * tq:(r + 1) * tq, :] = (q_ref[0, :, h * HD:(h + 1) * HD] * QSCALE).astype(MXU)


def _sb_block(ks, vs, before, q_scr, c_scr, acc_scr):
    rep = H_C // KV_C
    rows = q_scr.shape[1]
    nchunk = ks[0].shape[0] // LANES
    older = None if before is None else jnp.concatenate([before] * rep, axis=0) > 0.5
    ys, pieces = [], []
    for g in range(KV_C):
        y = _dot_nt(q_scr[g], ks[g])
        drop = jnp.maximum(y, 0.0) + jnp.log2(1.0 + jnp.exp2(-jnp.abs(y)))
        if older is not None:
            drop = jnp.where(older, drop, 0.0)
        ys.append(y)
        pieces += [drop[:, c * LANES:(c + 1) * LANES] for c in range(nchunk)]
    stacked = jnp.concatenate(pieces, axis=0)
    hi = stacked.astype(BF16)
    lo = (stacked - hi.astype(F32)).astype(BF16)
    r_i = lax.broadcasted_iota(I32, (LANES, LANES), 0)
    c_i = lax.broadcasted_iota(I32, (LANES, LANES), 1)
    later = jnp.where(r_i >= c_i, -1.0, 0.0).astype(BF16)
    incl = jnp.dot(jnp.concatenate([hi, lo], axis=1), jnp.concatenate([later, later], axis=0),
                   preferred_element_type=F32)
    for g in range(KV_C):
        run = c_scr[g]
        cols = [None] * nchunk
        for c in reversed(range(nchunk)):
            part = incl[(g * nchunk + c) * rows:(g * nchunk + c + 1) * rows]
            cols[c] = part + run
            run = run + jnp.broadcast_to(part[:, :1], part.shape)
        a = jnp.exp2(ys[g] + jnp.concatenate(cols, axis=1))
        if older is not None:
            a = jnp.where(older, a, 0.0)
        acc_scr[g] = acc_scr[g] + _dot(a, vs[g])
        c_scr[g] = run


def _sb_final(o_ref, acc_scr, tq):
    rep = H_C // KV_C
    for g in range(KV_C):
        for r in range(rep):
            h = g * rep + r
            o_ref[0, :, h * HD:(h + 1) * HD] = acc_scr[g][r * tq:(r + 1) * tq, :]


def _sb_scratch(tq):
    rows = (H_C // KV_C) * tq
    return [pltpu.VMEM((KV_C, rows, HD), MXU),
            pltpu.VMEM((KV_C, rows, LANES), F32),
            pltpu.VMEM((KV_C, rows, HD), F32)]


def _split_kv(kv):
    ks = [kv[:, g * HD:(g + 1) * HD] for g in range(KV_C)]
    vs = [kv[:, (KV_C + g) * HD:(KV_C + g + 1) * HD] for g in range(KV_C)]
    return ks, vs


def _older_mask(tq, tk, qpos0, kpos0):
    qpos = qpos0 + lax.broadcasted_iota(I32, (tq, tk), 0)
    kpos = kpos0 + lax.broadcasted_iota(I32, (tq, tk), 1)
    return jnp.where(kpos < qpos, 1.0, 0.0)


def _key_norm_body(kv_ref, o_ref):
    @pl.when(pl.program_id(1) == 0)
    def _():
        o_ref[...] = jnp.zeros(o_ref.shape, F32)

    kv = kv_ref[0]
    for g in range(KV_C):
        k = kv[:, g * HD:(g + 1) * HD]
        top = jnp.max(jnp.sum(k * k, axis=1, keepdims=True), axis=0, keepdims=True)
        o_ref[0, g:g + 1, :] = jnp.maximum(o_ref[0, g:g + 1, :], jnp.broadcast_to(top, (1, LANES)))


def _key_norm(kv, tk):
    b, s_len, cw = kv.shape
    return pl.pallas_call(
        _key_norm_body,
        grid=(b, s_len // tk),
        in_specs=[pl.BlockSpec((1, tk, cw), lambda bi, j: (bi, j, 0))],
        out_specs=pl.BlockSpec((1, 8, LANES), lambda bi, j: (bi, 0, 0)),
        out_shape=jax.ShapeDtypeStruct((b, 8, LANES), F32),
        compiler_params=_params(("parallel", "arbitrary")),
        name="key_norm",
    )(kv)


SB_DEAD = -160.0
NORM_SLACK = 1.02


def _sb_body(q_ref, kv_ref, kn_ref, o_ref, q_scr, c_scr, acc_scr, qn_scr, done_ref, *,
             tq, tk, nj, pos0):
    rep = H_C // KV_C
    qb = pl.program_id(1)
    j = pl.program_id(2)
    kb = (pos0 + (qb + 1) * tq - 1) // tk - j
    q0 = pos0 + qb * tq

    @pl.when(j == 0)
    def _():
        _sb_init(q_ref, q_scr, c_scr, acc_scr, tq)
        done_ref[0] = 0
        for g in range(KV_C):
            for r in range(rep):
                h = g * rep + r
                qs = q_ref[0, :, h * HD:(h + 1) * HD] * QSCALE
                qn_scr[g, r * tq:(r + 1) * tq, :] = jnp.broadcast_to(
                    jnp.sqrt(jnp.sum(qs * qs, axis=1, keepdims=True)), (tq, LANES))

    live = (kb >= 0) & (done_ref[0] == 0)

    @pl.when(live & ((kb + 1) * tk > q0))
    def _():
        ks, vs = _split_kv(kv_ref[0])
        _sb_block(ks, vs, _older_mask(tq, tk, q0, kb * tk), q_scr, c_scr, acc_scr)

    @pl.when(live & ((kb + 1) * tk <= q0))
    def _():
        ks, vs = _split_kv(kv_ref[0])
        _sb_block(ks, vs, None, q_scr, c_scr, acc_scr)

    @pl.when(live)
    def _():
        worst = None
        for g in range(KV_C):
            reach = qn_scr[g] * (jnp.sqrt(kn_ref[0, g:g + 1, :]) * NORM_SLACK) + c_scr[g]
            top = jnp.max(reach)
            worst = top if worst is None else jnp.maximum(worst, top)
        done_ref[0] = jnp.where(worst < SB_DEAD, 1, 0).astype(I32)

    @pl.when(j == nj - 1)
    def _():
        _sb_final(o_ref, acc_scr, tq)


def _sb_attn(q, kv, *, tq, tk, pos0):
    b, t, qw = q.shape
    cw = kv.shape[2]
    nj = kv.shape[1] // tk
    rows = (H_C // KV_C) * tq
    body = functools.partial(_sb_body, tq=tq, tk=tk, nj=nj, pos0=pos0)
    return _call(
        body, grid=(b, t // tq, nj),
        in_specs=[pl.BlockSpec((1, tq, qw), lambda bi, qb, j: (bi, qb, 0)),
                  pl.BlockSpec((1, tk, cw), lambda bi, qb, j: (
                      bi, jnp.maximum((pos0 + (qb + 1) * tq - 1) // tk - j, 0), 0)),
                  pl.BlockSpec((1, 8, LANES), lambda bi, qb, j: (bi, 0, 0))],
        out_specs=pl.BlockSpec((1, tq, qw), lambda bi, qb, j: (bi, qb, 0)),
        out_shape=jax.ShapeDtypeStruct((b, t, qw), F32),
        scratch=_sb_scratch(tq) + [pltpu.VMEM((KV_C, rows, LANES), F32),
                                   pltpu.SMEM((1,), I32)],
        sem=("parallel", "parallel", "arbitrary"), name="sb_attn")(q, kv, _key_norm(kv, tk))


def _sb_paged_body(_, q_ref, *refs, tq, page, n_pg, n_main, pos0):
    pages = refs[:n_pg]
    tail_ref, o_ref, q_scr, c_scr, acc_scr = refs[n_pg:]
    j = pl.program_id(2)
    rpt = 2 * KV_C

    @pl.when(j == 0)
    def _():
        _sb_init(q_ref, q_scr, c_scr, acc_scr, tq)
        ks, vs = _split_kv(tail_ref[0])
        _sb_block(ks, vs, _older_mask(tq, page, pos0, n_main * n_pg * page), q_scr, c_scr, acc_scr)

    @pl.when(j > 0)
    def _():
        ks = [jnp.concatenate([pg[pl.ds(g, page, stride=rpt), :] for pg in pages], axis=0)
              for g in range(KV_C)]
        vs = [jnp.concatenate([pg[pl.ds(KV_C + g, page, stride=rpt), :] for pg in pages], axis=0)
              for g in range(KV_C)]
        _sb_block(ks, vs, None, q_scr, c_scr, acc_scr)

    @pl.when(j == n_main)
    def _():
        _sb_final(o_ref, acc_scr, tq)


def _sb_attn_paged(q, pool, pt, off, tail, *, page, pos0, pages_per_step=8):
    b, t, qw = q.shape
    n_pages = pt.shape[1]
    n_pg = _tile(n_pages, pages_per_step, 1)
    n_main = n_pages // n_pg
    rpt = 2 * KV_C
    ix = lambda f: (lambda bi, qb, j, p: f(bi, qb, j))
    page_specs = [pl.BlockSpec(
        (page * rpt, LANES),
        functools.partial(lambda u, bi, qb, j, p:
                          (off + p[bi, jnp.clip(n_main - j, 0, n_main - 1) * n_pg + u], 0), u))
        for u in range(n_pg)]
    in_specs = ([pl.BlockSpec((1, t, qw), ix(lambda bi, qb, j: (bi, 0, 0)))] + page_specs
                + [pl.BlockSpec((1, page, tail.shape[2]), ix(lambda bi, qb, j: (bi, 0, 0)))])
    body = functools.partial(_sb_paged_body, tq=t, page=page, n_pg=n_pg, n_main=n_main, pos0=pos0)
    return _call(body, grid=(b, 1, n_main + 1), in_specs=in_specs,
                 out_specs=pl.BlockSpec((1, t, qw), ix(lambda bi, qb, j: (bi, 0, 0))),
                 out_shape=jax.ShapeDtypeStruct((b, t, qw), F32), scratch=_sb_scratch(t),
                 sem=("parallel", "parallel", "arbitrary"), name="sb_attn_paged",
                 prefetch=pt)(q, *([pool] * n_pg), tail)


def _mem_body(x_ref, kv_ref, wq_ref, wo_ref, g_ref, b_ref, o_ref):
    x = x_ref[0]
    kv = kv_ref[0]
    q = _dot(x, wq_ref[...])
    outs = []
    for h in range(MEM_HEADS):
        s = _dot_nt(q[:, h * HD:(h + 1) * HD], kv[:, h * HD:(h + 1) * HD]) * SCALE
        p = jnp.exp(s - jnp.max(s, axis=1, keepdims=True))
        p = p / jnp.sum(p, axis=1, keepdims=True)
        outs.append(_dot(p, kv[:, (MEM_HEADS + h) * HD:(MEM_HEADS + h + 1) * HD]))
    y = _dot(jnp.concatenate(outs, axis=1), wo_ref[...])
    o_ref[0] = _ln(ALPHA * x + y, g_ref[...], b_ref[...])


def _mem_layer(x, mem_kv, wq, wo, g, b, *, layer=None, tq):
    bsz, t, d = x.shape
    nm, cw = mem_kv.shape[1:]
    return pl.pallas_call(
        _mem_body,
        grid=(bsz, t // tq),
        in_specs=[pl.BlockSpec((1, tq, d), lambda bi, qi: (bi, qi, 0)),
                  pl.BlockSpec((1, nm, cw), lambda bi, qi: (bi, 0, 0)),
                  _wspec(wq, layer, wq.shape[-2:], lambda bi, qi: (0, 0)),
                  _wspec(wo, layer, wo.shape[-2:], lambda bi, qi: (0, 0)),
                  pl.BlockSpec((1, d), lambda bi, qi: (0, 0)),
                  pl.BlockSpec((1, d), lambda bi, qi: (0, 0))],
        out_specs=pl.BlockSpec((1, tq, d), lambda bi, qi: (bi, qi, 0)),
        out_shape=jax.ShapeDtypeStruct((bsz, t, d), F32),
        compiler_params=_params(("parallel", "parallel")),
        name="mem_layer",
    )(x, mem_kv, wq, wo, g.reshape(1, d), b.reshape(1, d))


def _ffn_body(x_ref, wg_ref, wu_ref, wd_ref, g_ref, b_ref, o_ref, xb_ref, acc_ref, *, nf):
    f = pl.program_id(1)

    @pl.when(f == 0)
    def _():
        xb_ref[...] = x_ref[...].astype(MXU)
        acc_ref[...] = jnp.zeros(acc_ref.shape, F32)

    xb = xb_ref[...]
    gate = jnp.dot(xb, wg_ref[...], preferred_element_type=F32)
    up = jnp.dot(xb, wu_ref[...], preferred_element_type=F32)
    hidden = gate / (1.0 + jnp.exp(-gate)) * up
    acc_ref[...] += _dot(hidden, wd_ref[...])

    @pl.when(f == nf - 1)
    def _():
        o_ref[...] = _ln(ALPHA * x_ref[...] + acc_ref[...], g_ref[...], b_ref[...])


def _ffn_layer(x, w_gu, w_d, g, b, *, layer=None, tm=512, tf=512):
    m, d = x.shape
    dff = w_d.shape[-2]
    tm = _tile(m, tm, 8)
    tf = _tile(dff, tf, LANES)
    nf = dff // tf
    return pl.pallas_call(
        functools.partial(_ffn_body, nf=nf),
        grid=(m // tm, nf),
        in_specs=[pl.BlockSpec((tm, d), lambda i, f: (i, 0)),
                  _wspec(w_gu, layer, (d, tf), lambda i, f: (0, f)),
                  _wspec(w_gu, layer, (d, tf), lambda i, f: (0, nf + f)),
                  _wspec(w_d, layer, (tf, d), lambda i, f: (f, 0)),
                  pl.BlockSpec((1, d), lambda i, f: (0, 0)),
                  pl.BlockSpec((1, d), lambda i, f: (0, 0))],
        out_specs=pl.BlockSpec((tm, d), lambda i, f: (i, 0)),
        out_shape=jax.ShapeDtypeStruct((m, d), F32),
        scratch_shapes=[pltpu.VMEM((tm, d), MXU), pltpu.VMEM((tm, d), F32)],
        compiler_params=_params(("parallel", "arbitrary")),
        name="ffn_layer",
    )(x, w_gu, w_gu, w_d, g.reshape(1, d), b.reshape(1, d))


def _even_odd_rows(a):
    b, n2, c = a.shape
    return a.reshape(b, n2 // 2, 2, c).transpose(0, 2, 1, 3).reshape(b, n2, c)


def _pad_rows(a, rows):
    return jnp.pad(a, ((0, 0), (0, rows - a.shape[1]), (0, 0)))


def _ab_mixer(x, w, layer, past, *, pos0):
    b, t, d = x.shape
    m = b * t
    pos = pos0 + jnp.arange(t, dtype=I32)
    tabs = _rope_tables(pos, HD) + _rope_tables(pos, D_IDX)
    h = _mm(x.reshape(m, d), w["in_ab"][layer], tn=640)
    q_a, kv_a, q_i, q_b, cmp_kv, sel_kv, win_kv, misc = _post_ab(h, tabs, t)
    r3 = lambda a: a.reshape(b, t, a.shape[1])
    q_a, kv_a, q_i, q_b, cmp_kv, sel_kv, win_kv, misc = map(
        r3, (q_a, kv_a, q_i, q_b, cmp_kv, sel_kv, win_kv, misc))
    cw = 2 * KV_B * HD
    rep_a, rep_b = H_A // KV_A, H_B // KV_B
    if past is None:
        s_len = t
        tq = _tile(t, 256, 8)
        tk = _tile(t, 1024, LANES)
        topk = min(TOPK_MAX, s_len // 4)
        bias_a = _dsa_index(q_i, misc, misc, topk=topk, pos0=0, tq=tq, tk=_tile(t, 512, LANES))
        o_a = _attn(q_a, kv_a, groups=KV_A, rep=rep_a, tq=tq, tk=tk, pos0=0, bias=bias_a)
        nc = s_len // CMP_BLOCK
        kc = _mm(cmp_kv.reshape(m // CMP_BLOCK, CMP_BLOCK * cw), w["phi"][layer], tn=cw)
        kc = _even_odd_rows(kc.reshape(b, nc, cw))
        s_pad = s_len
        o_w = _attn(q_b, win_kv, groups=KV_B, rep=rep_b, tq=tq, tk=tq, pos0=0)
        sel_attn = lambda bias: _attn(q_b, sel_kv, groups=KV_B, rep=rep_b, tq=tq, tk=tk,
                                      pos0=0, bias=bias)
        win_state = win_kv[:, t - min(WINDOW, t):]
    else:
        pt, caches, win_prev = past
        n_pages = pt.shape[1]
        page = caches["page"]
        n_pool = caches["n_pool"]
        off = layer * n_pool
        s_len = n_pages * page + t
        s_pad = (n_pages + 1) * page
        tq = t
        topk = min(TOPK_MAX, s_len // 4)
        bias_a = _dsa_index(q_i, misc, caches["a_kidx"], topk=topk, pos0=pos0, tq=tq, tk=page,
                            page=(pt, off, _pad_rows(misc[:, :, :D_IDX], page)))
        o_a = _attn_paged(q_a, caches["a_kv"], pt, off, _pad_rows(kv_a, page), bias_a,
                          groups=KV_A, rep=rep_a, page=page)
        nc = s_len // CMP_BLOCK
        per_page = page // CMP_BLOCK
        assert nc == n_pages * per_page, "new rows must not complete a compressed block"
        raw = _cmp_rows(caches["cmp"], pt, off, page=page, rpt=cw // LANES)
        kc = _mm(raw.reshape(b * nc, raw.shape[2]), w["phi"][layer], tn=cw)
        kc = _even_odd_rows(kc.reshape(b, nc, cw))
        win_all = jnp.concatenate([win_prev, win_kv], axis=1)
        k_start = n_pages * page - win_prev.shape[1]
        o_w = _attn(q_b, _pad_rows(win_all, -(-win_all.shape[1] // page) * page),
                    groups=KV_B, rep=rep_b, tq=tq, tk=page, pos0=pos0, k_start=k_start)
        sel_attn = lambda bias: _attn_paged(q_b, caches["sel"], pt, off, _pad_rows(sel_kv, page),
                                            bias, groups=KV_B, rep=rep_b, page=page)
        win_state = win_all[:, win_all.shape[1] - win_prev.shape[1]:]
    assert nc % 2 == 0
    c_pos = (jnp.arange(nc, dtype=I32) + 1) * CMP_BLOCK - 1
    ctab = tuple(_even_odd_rows(tb[None])[0] for tb in _rope_tables(c_pos, HD))
    o_c, bias_s = _nsa_cmp(q_b, kc, ctab, s_len=s_len, s_pad=s_pad, pos0=pos0, tq=tq)
    o_s = sel_attn(bias_s)
    f2 = lambda a: a.reshape(m, a.shape[2])
    pieces = (f2(o_a), f2(o_c), f2(o_s), f2(o_w), f2(misc))
    states = (kv_a, misc[:, :, :D_IDX], cmp_kv, sel_kv, win_state)
    return pieces, states


def _sb_mixer(x, w, layer, past, *, pos0):
    b, t, d = x.shape
    m = b * t
    x2 = x.reshape(m, d)
    q = _mm(x2, w["in_c"], layer=layer, cols=(0, H_C * HD)).reshape(b, t, H_C * HD)
    kv = _mm(x2, w["in_c"], layer=layer,
             cols=(H_C * HD, 2 * KV_C * HD)).reshape(b, t, 2 * KV_C * HD)
    if past is None:
        tq = _tile(t, 256, 8)
        o = _sb_attn(q, kv, tq=tq, tk=tq, pos0=0)
    else:
        pt, caches = past
        page = caches["page"]
        o = _sb_attn_paged(q, caches["c_kv"], pt, layer * caches["n_pool"], _pad_rows(kv, page),
                           page=page, pos0=pos0)
    return o.reshape(m, H_C * HD), kv


def _run_group(x, w, mem_kvs, pasts, ln_g, ln_b, *, pos0):
    b, t, d = x.shape
    m = b * t
    even, odd = [], []
    for i in range(DEPTH):
        if i % 2 == 0:
            pieces, st = _ab_mixer(x, w, i // 2, pasts[i], pos0=pos0)
            even.append(st)
            x2 = _mix_ln(*pieces, w["out"], x.reshape(m, d), ln_g[i, 0], ln_b[i, 0], layer=i)
        else:
            mix, st = _sb_mixer(x, w, i // 2, pasts[i], pos0=pos0)
            odd.append(st)
            x2 = _mm_ln(mix, w["out"], x.reshape(m, d), ln_g[i, 0], ln_b[i, 0], layer=i)
        x3 = _mem_layer(x2.reshape(b, t, d), mem_kvs[i], w["mem_q"], w["mem_o"],
                        ln_g[i, 1], ln_b[i, 1], layer=i, tq=_tile(t, 256, 8))
        x = _ffn_layer(x3.reshape(m, d), w["gate_up"], w["down"],
                       ln_g[i, 2], ln_b[i, 2], layer=i).reshape(b, t, d)
    return x, even, odd


def _reorder_in_ab(w):
    sizes = (H_A * HD, 2 * KV_A * HD, H_IDX * D_IDX, D_IDX, H_IDX,
             H_B * HD, 2 * KV_B * HD, 2 * KV_B * HD, 2 * KV_B * HD, 3 * H_B)
    offs = np.concatenate([[0], np.cumsum(sizes)])
    piece = lambda i: w[..., offs[i]:offs[i + 1]]
    order = [0, 1, 2, 5, 6, 7, 8, 3, 4, 9]
    cols = [piece(i) for i in order]
    pad = C_END - int(offs[-1])
    cols.append(jnp.zeros(w.shape[:-1] + (pad,), w.dtype))
    return jnp.concatenate(cols, axis=-1)


def _phi_matrix(w_phi):
    phi = w_phi.reshape(2, CMP_BLOCK, HD, HD)
    nb = 2 * KV_B
    per_piece = jnp.stack([phi[c // KV_B] for c in range(nb)], axis=1)
    same = jnp.eye(nb, dtype=bool)[None, :, None, :, None]
    out = jnp.where(same, per_piece[:, :, :, None, :], jnp.zeros((), w_phi.dtype))
    return out.reshape(CMP_BLOCK * nb * HD, nb * HD)


def kernel(x_prompt, x_sample, cache_a_kv, cache_a_kidx, cache_b_cmp_kv, cache_b_sel_kv,
           state_b_win_kv, cache_c_kv, cache_mem_kv, page_table, mem_prompt, w_in_ab, w_cmp_phi,
           w_in_c, w_out, w_mem_q, w_mem_kv, w_mem_o, w_gate_up, w_down, ln_g, ln_b):
    b_p, t_p, d = x_prompt.shape
    b_s, t_s, _ = x_sample.shape
    n_even, n_pool, page = cache_a_kv.shape[:3]
    n_odd = cache_c_kv.shape[0]
    n_mem = mem_prompt.shape[1]
    past_len = page_table.shape[1] * page

    w = {
        "in_ab": [_reorder_in_ab(wl).astype(MXU) for wl in w_in_ab],
        "phi": [_phi_matrix(wl.astype(MXU)) for wl in w_cmp_phi],
        "in_c": w_in_c.astype(MXU),
        "out": w_out.astype(MXU),
        "mem_q": w_mem_q.astype(MXU),
        "mem_o": w_mem_o.astype(MXU),
        "gate_up": w_gate_up.astype(MXU),
        "down": w_down.astype(MXU),
    }
    w_mkv = w_mem_kv.astype(MXU)

    mem_flat = mem_prompt.reshape(b_p * n_mem, d)
    mem_p = [_mm(mem_flat, w_mkv, layer=i).reshape(b_p, n_mem, 2 * MEM_HEADS * HD)
             for i in range(DEPTH)]
    y_p, ev_p, od_p = _run_group(x_prompt, w, mem_p, [None] * DEPTH, ln_g, ln_b, pos0=0)

    cw_b = 2 * KV_B * HD
    caches = {
        "page": page, "n_pool": n_pool,
        "a_kv": cache_a_kv.reshape(-1, LANES),
        "a_kidx": cache_a_kidx.reshape(n_even * n_pool, page, D_IDX),
        "cmp": cache_b_cmp_kv.reshape(-1, LANES),
        "sel": cache_b_sel_kv.reshape(-1, LANES),
    }
    caches_c = {"page": page, "n_pool": n_pool, "c_kv": cache_c_kv.reshape(-1, LANES)}
    win_prev = state_b_win_kv.reshape(n_even, b_s, state_b_win_kv.shape[2], cw_b)
    pasts = []
    for i in range(DEPTH):
        if i % 2 == 0:
            pasts.append((page_table, caches, win_prev[i // 2]))
        else:
            pasts.append((page_table, caches_c))
    mem_s = [cache_mem_kv[i].reshape(b_s, n_mem, 2 * MEM_HEADS * HD) for i in range(DEPTH)]
    y_s, ev_s, od_s = _run_group(x_sample, w, mem_s, pasts, ln_g, ln_b, pos0=past_len)

    def pack(states, bsz):
        def kv5(a, g):
            return a.reshape(bsz, a.shape[1], 2, g, HD)
        return (jnp.stack([kv5(s[0], KV_A) for s in states]),
                jnp.stack([s[1] for s in states]),
                jnp.stack([kv5(s[2], KV_B) for s in states]),
                jnp.stack([kv5(s[3], KV_B) for s in states]),
                jnp.stack([kv5(s[4], KV_B) for s in states]))

    a_p, ki_p, cmp_p, sel_p, win_p = pack(ev_p, b_p)
    a_s, ki_s, cmp_s, sel_s, win_s = pack(ev_s, b_s)
    c_p = jnp.stack([s.reshape(b_p, t_p, 2, KV_C, HD) for s in od_p])
    c_s = jnp.stack([s.reshape(b_s, t_s, 2, KV_C, HD) for s in od_s])
    mem_out = jnp.stack([mk.reshape(b_p, n_mem, 2, MEM_HEADS, HD) for mk in mem_p])
    return (y_p, y_s, a_p, ki_p, cmp_p, sel_p, win_p, c_p, mem_out,
            a_s, ki_s, cmp_s, sel_s, win_s, c_s)
```

```python
import functools
import math

import jax
import jax.numpy as jnp
import numpy as np
from jax import lax
from jax.experimental import pallas as pl
from jax.experimental.pallas import tpu as pltpu

F32 = jnp.float32
BF16 = jnp.bfloat16
I32 = jnp.int32

HD = 128
LANES = 128
H_A, KV_A, H_IDX, D_IDX = 8, 2, 4, 64
H_B, KV_B = 8, 2
H_C, KV_C = 16, 4
MEM_HEADS = 4
TOPK_MAX = 256
CMP_BLOCK, SEL_BLOCK, N_SEL, WINDOW = 32, 64, 16, 512
ROPE_THETA = 10000.0
LN_EPS = 1e-5
DEPTH = 4
ALPHA = (2 * DEPTH) ** 0.25
SCALE = HD ** -0.5
QSCALE = SCALE * math.log2(math.e)
NEG = -1e30
INT_MIN = -2 ** 31
VMEM_LIMIT = 56 * 1024 * 1024
PAGES_PER_STEP = 16

C_QA, C_KVA, C_QI, C_QB, C_CMP, C_SEL, C_WIN, C_MISC, C_END = (
    0, 1024, 1536, 1792, 2816, 3328, 3840, 4352, 4480)
MISC_WI = D_IDX
MISC_G = D_IDX + H_IDX


def _params(sem):
    return pltpu.CompilerParams(dimension_semantics=sem, vmem_limit_bytes=VMEM_LIMIT)


def _tile(n, pref, align):
    t = (min(pref, n) // align) * align
    while t >= align:
        if n % t == 0:
            return t
        t -= align
    return n


def _ln(y, g, b):
    mu = jnp.mean(y, axis=-1, keepdims=True)
    d = y - mu
    var = jnp.mean(d * d, axis=-1, keepdims=True)
    return d * lax.rsqrt(var + LN_EPS) * g + b


MXU = BF16


def _dot(a, b):
    return jnp.dot(a.astype(MXU), b.astype(MXU), preferred_element_type=F32)


def _dot_nt(a, b):
    return lax.dot_general(a.astype(MXU), b.astype(MXU), (((1,), (1,)), ((), ())),
                           preferred_element_type=F32)


def _call(body, *, grid, in_specs, out_specs, out_shape, scratch=(), sem, name, prefetch=None):
    if prefetch is None:
        return pl.pallas_call(body, grid=grid, in_specs=in_specs, out_specs=out_specs,
                              out_shape=out_shape, scratch_shapes=list(scratch),
                              compiler_params=_params(sem), name=name)
    gs = pltpu.PrefetchScalarGridSpec(num_scalar_prefetch=1, grid=grid, in_specs=in_specs,
                                      out_specs=out_specs, scratch_shapes=list(scratch))
    return functools.partial(
        pl.pallas_call(body, grid_spec=gs, out_shape=out_shape, compiler_params=_params(sem),
                       name=name), prefetch)


def _mm_body(x_ref, w_ref, o_ref, acc_ref, *, nk):
    part = _dot(x_ref[...], w_ref[...])
    if nk == 1:
        o_ref[...] = part
        return
    k = pl.program_id(2)

    @pl.when(k == 0)
    def _():
        acc_ref[...] = part

    @pl.when(k > 0)
    def _():
        acc_ref[...] += part

    @pl.when(k == nk - 1)
    def _():
        o_ref[...] = acc_ref[...]


def _wspec(w, layer, block, index):
    if w.ndim == 2:
        return pl.BlockSpec(block, index)
    return pl.BlockSpec((None,) + block, lambda *a: (layer,) + index(*a))


def _mm(x, w, *, layer=None, cols=None, tm=1024, tn=512, tk=2048):
    m, kd = x.shape
    col0, n = cols if cols is not None else (0, w.shape[-1])
    tm = _tile(m, tm, 8)
    tn = _tile(math.gcd(n, col0), tn, LANES)
    tk = _tile(kd, tk, LANES)
    nk = kd // tk
    cb0 = col0 // tn
    return pl.pallas_call(
        functools.partial(_mm_body, nk=nk),
        grid=(m // tm, n // tn, nk),
        in_specs=[pl.BlockSpec((tm, tk), lambda i, j, k: (i, k)),
                  _wspec(w, layer, (tk, tn), lambda i, j, k: (k, cb0 + j))],
        out_specs=pl.BlockSpec((tm, tn), lambda i, j, k: (i, j)),
        out_shape=jax.ShapeDtypeStruct((m, n), F32),
        scratch_shapes=[pltpu.VMEM((tm, tn) if nk > 1 else (8, LANES), F32)],
        compiler_params=_params(("parallel", "parallel", "arbitrary")),
        name="mm",
    )(x, w)


def _mm_ln_body(x_ref, w_ref, r_ref, g_ref, b_ref, o_ref, acc_ref, *, nk):
    k = pl.program_id(1)
    part = _dot(x_ref[...], w_ref[...])
    if nk == 1:
        o_ref[...] = _ln(ALPHA * r_ref[...] + part, g_ref[...], b_ref[...])
        return

    @pl.when(k == 0)
    def _():
        acc_ref[...] = part

    @pl.when(k > 0)
    def _():
        acc_ref[...] += part

    @pl.when(k == nk - 1)
    def _():
        o_ref[...] = _ln(ALPHA * r_ref[...] + acc_ref[...], g_ref[...], b_ref[...])


def _mm_ln(x, w, res, g, b, *, layer=None, tm=512, tk=2048):
    m, kd = x.shape
    d = w.shape[-1]
    tm = _tile(m, tm, 8)
    tk = _tile(kd, tk, LANES)
    nk = kd // tk
    return pl.pallas_call(
        functools.partial(_mm_ln_body, nk=nk),
        grid=(m // tm, nk),
        in_specs=[pl.BlockSpec((tm, tk), lambda i, k: (i, k)),
                  _wspec(w, layer, (tk, d), lambda i, k: (k, 0)),
                  pl.BlockSpec((tm, d), lambda i, k: (i, 0)),
                  pl.BlockSpec((1, d), lambda i, k: (0, 0)),
                  pl.BlockSpec((1, d), lambda i, k: (0, 0))],
        out_specs=pl.BlockSpec((tm, d), lambda i, k: (i, 0)),
        out_shape=jax.ShapeDtypeStruct((m, d), F32),
        scratch_shapes=[pltpu.VMEM((tm, d) if nk > 1 else (8, LANES), F32)],
        compiler_params=_params(("parallel", "arbitrary")),
        name="mm_ln",
    )(x, w, res, g.reshape(1, d), b.reshape(1, d))


def _post_ab_body(h_ref, c128_ref, s128_ref, c64_ref, s64_ref,
                  qa_ref, kva_ref, qi_ref, qb_ref, cmp_ref, sel_ref, win_ref, misc_ref):
    cos = c128_ref[...]
    sin = s128_ref[...]
    ci = c64_ref[...]
    si = s64_ref[...]
    lane = lax.broadcasted_iota(I32, cos.shape, 1)
    first_half = (lane & (D_IDX - 1)) < (D_IDX // 2)

    def rope128(x):
        return x * cos + pltpu.roll(x, HD // 2, 1) * sin

    def rope64(x):
        partner = jnp.where(first_half, pltpu.roll(x, LANES - D_IDX // 2, 1),
                            pltpu.roll(x, D_IDX // 2, 1))
        return x * ci + partner * si

    def head(c0, i):
        return h_ref[:, c0 + i * HD:c0 + (i + 1) * HD]

    for i in range(H_A):
        qa_ref[:, i * HD:(i + 1) * HD] = rope128(head(C_QA, i))
    for i in range(H_B):
        qb_ref[:, i * HD:(i + 1) * HD] = rope128(head(C_QB, i))
    for src, dst, kv in ((C_KVA, kva_ref, KV_A), (C_SEL, sel_ref, KV_B), (C_WIN, win_ref, KV_B)):
        for i in range(kv):
            dst[:, i * HD:(i + 1) * HD] = rope128(head(src, i))
            dst[:, (kv + i) * HD:(kv + i + 1) * HD] = head(src, kv + i)
    cmp_ref[...] = h_ref[:, C_CMP:C_SEL]
    for i in range(H_IDX * D_IDX // LANES):
        qi_ref[:, i * LANES:(i + 1) * LANES] = rope64(head(C_QI, i))
    raw = h_ref[:, C_MISC:C_END]
    misc_ref[...] = jnp.where(
        lane < MISC_WI, rope64(raw),
        jnp.where(lane < MISC_G, raw * (H_IDX ** -0.5),
                  jnp.where(lane < MISC_G + 3 * H_B, 1.0 / (1.0 + jnp.exp(-raw)), 0.0)))


def _post_ab(h, tabs, t):
    m = h.shape[0]
    tm = _tile(t, 256, 8)
    nt = t // tm
    widths = (H_A * HD, 2 * KV_A * HD, H_IDX * D_IDX, H_B * HD, 2 * KV_B * HD,
              2 * KV_B * HD, 2 * KV_B * HD, LANES)
    tab_spec = pl.BlockSpec((tm, LANES), lambda i: (i % nt, 0))
    return pl.pallas_call(
        _post_ab_body,
        grid=(m // tm,),
        in_specs=[pl.BlockSpec((tm, C_END), lambda i: (i, 0))] + [tab_spec] * 4,
        out_specs=[pl.BlockSpec((tm, w), lambda i: (i, 0)) for w in widths],
        out_shape=[jax.ShapeDtypeStruct((m, w), F32) for w in widths],
        compiler_params=_params(("parallel",)),
        name="post_ab",
    )(h, *tabs)


def _rope_tables(pos, d):
    half = d // 2
    inv = ROPE_THETA ** (-(jnp.arange(half, dtype=F32) * 2.0 / d))
    ang = pos.astype(F32)[:, None] * inv[None, :]
    cos, sin = jnp.cos(ang), jnp.sin(ang)
    reps = LANES // d
    return (jnp.tile(jnp.concatenate([cos, cos], axis=1), (1, reps)),
            jnp.tile(jnp.concatenate([-sin, sin], axis=1), (1, reps)))


def _sortable(x):
    bits = lax.bitcast_convert_type(jnp.where(x == 0.0, 0.0, x), I32)
    return bits ^ ((bits >> 31) & 0x7FFFFFFF)


def _dsa_index_body(*refs, tq, wk, n_kx, n_main, tail_w, topk, pos0, n_bucket):
    refs = list(refs)
    if tail_w:
        refs.pop(0)
    qi_ref, wi_ref = refs[0], refs[1]
    kx_refs = refs[2:2 + n_kx]
    i = 2 + n_kx
    tail_ref = None
    if tail_w:
        tail_ref = refs[i]
        i += 1
    bias_ref, key_ref, kth_ref, need_ref = refs[i:i + 4]
    qb = pl.program_id(1)
    j = pl.program_id(2)
    nch = key_ref.shape[0]
    nj = n_main + (1 if tail_w else 0)
    last_q = pos0 + (qb + 1) * tq - 1

    def put_keys(kx, chunk0, width, kpos0):
        qi = qi_ref[0]
        wi = wi_ref[0]
        score = jnp.zeros((tq, width), F32)
        for h in range(H_IDX):
            dots = _dot_nt(qi[:, h * D_IDX:(h + 1) * D_IDX], kx) * (D_IDX ** -0.5)
            score = score + wi[:, MISC_WI + h:MISC_WI + h + 1] * jnp.maximum(dots, 0.0)
        qpos = pos0 + qb * tq + lax.broadcasted_iota(I32, (tq, width), 0)
        kpos = kpos0 + lax.broadcasted_iota(I32, (tq, width), 1)
        keys = jnp.where(kpos <= qpos, _sortable(score), INT_MIN)
        for u in range(width // LANES):
            key_ref[chunk0 + u] = keys[:, u * LANES:(u + 1) * LANES]

    def main_step():
        kx = jnp.concatenate([r[0][:, :D_IDX] for r in kx_refs], axis=0)
        put_keys(kx, j * (wk // LANES), wk, j * wk)

    if tail_w:
        pl.when(j < n_main)(main_step)

        @pl.when(j == n_main)
        def _():
            put_keys(tail_ref[0][:, :D_IDX], n_main * (wk // LANES), tail_w, n_main * wk)
    else:
        pl.when(j * wk <= last_q)(main_step)

        @pl.when(j * wk > last_q)
        def _():
            for u in range(wk // LANES):
                key_ref[j * (wk // LANES) + u] = jnp.full((tq, LANES), INT_MIN, I32)

    def search(lim):
        n_part = max(1, min(4, tq // 64))
        pr = tq // n_part

        def count(pred, p):
            hit = pred(key_ref[0:lim, p * pr:(p + 1) * pr, :]).astype(I32)
            return jnp.sum(jnp.sum(hit, axis=0), axis=1, keepdims=True)

        ans0 = tuple(jnp.where(count(lambda k: k >= 0, p) >= topk, 0, INT_MIN)
                     for p in range(n_part))

        def bit_step(b, ans):
            bit = jnp.left_shift(jnp.int32(1), 30 - b)
            out = []
            for p in range(n_part):
                cand = ans[p] | bit
                out.append(jnp.where(count(lambda k: k >= cand, p) >= topk, cand, ans[p]))
            return tuple(out)

        kth = lax.fori_loop(0, 31, bit_step, ans0)
        for p in range(n_part):
            rows = slice(p * pr, (p + 1) * pr)
            kth_ref[rows, :] = jnp.broadcast_to(kth[p], (pr, LANES))
            need_ref[rows, :] = jnp.broadcast_to(
                (topk - count(lambda k: k > kth[p], p)).astype(F32), (pr, LANES))

    @pl.when(j == nj - 1)
    def _():
        n_live = last_q // LANES + 1
        lims = [nch * (i + 1) // n_bucket for i in range(n_bucket)]
        for i, lim in enumerate(lims):
            lo = lims[i - 1] if i else 0
            pl.when((n_live > lo) & ((n_live <= lim) | (i == n_bucket - 1)))(
                functools.partial(search, lim))
        kth = kth_ref[...]
        need = need_ref[...]
        r_i = lax.broadcasted_iota(I32, (LANES, LANES), 0)
        c_i = lax.broadcasted_iota(I32, (LANES, LANES), 1)
        strict = jnp.where(r_i < c_i, 1.0, 0.0).astype(BF16)
        ones = jnp.ones((LANES, LANES), BF16)
        seen = jnp.zeros((tq, LANES), F32)
        for c in range(nch):
            kc = key_ref[c]
            tie = jnp.where(kc == kth, 1.0, 0.0).astype(BF16)
            rank = seen + jnp.dot(tie, strict, preferred_element_type=F32)
            tie_bias = jnp.where(kc == kth, jnp.where(rank < need, 0.0, NEG), NEG)
            take = jnp.where(kc > kth, 0.0, tie_bias)
            bias_ref[0, 0, :, c * LANES:(c + 1) * LANES] = (
                jnp.where(kc > INT_MIN, take, NEG).astype(bias_ref.dtype))
            seen = seen + jnp.dot(tie, ones, preferred_element_type=F32)


def _dsa_index(qi, misc, kx, *, topk, pos0, tq, tk, page=None):
    b, t, _ = qi.shape
    nq = t // tq
    if page is not None:
        pt, off, tail = page
        n_pages = pt.shape[1]
        n_kx = _tile(n_pages, PAGES_PER_STEP, 1)
        n_main = n_pages // n_kx
        tail_w = tk
        wk = n_kx * tk
        ix = lambda f: (lambda bi, qb, j, p: f(bi, qb, j))
        kx_specs = [pl.BlockSpec(
            (1, tk, kx.shape[2]),
            functools.partial(lambda u, bi, qb, j, p:
                              (off + p[bi, jnp.minimum(j, n_main - 1) * n_kx + u], 0, 0), u))
            for u in range(n_kx)]
        extra = [pl.BlockSpec((1, tk, tail.shape[2]), ix(lambda bi, qb, j: (bi, 0, 0)))]
        args = [qi, misc] + [kx] * n_kx + [tail]
        prefetch = pt
        s_pad = n_pages * tk + tail_w
    else:
        n_kx, n_main, tail_w, wk = 1, kx.shape[1] // tk, 0, tk
        ix = lambda f: f
        kx_specs = [pl.BlockSpec((1, tk, kx.shape[2]), lambda bi, qb, j: (bi, j, 0))]
        extra = []
        args = [qi, misc, kx]
        prefetch = None
        s_pad = kx.shape[1]
    nj = n_main + (1 if tail_w else 0)
    dtype = BF16 if tq % 16 == 0 else F32
    in_specs = [pl.BlockSpec((1, tq, qi.shape[2]), ix(lambda bi, qb, j: (bi, qb, 0))),
                pl.BlockSpec((1, tq, LANES), ix(lambda bi, qb, j: (bi, qb, 0)))] + kx_specs + extra
    body = functools.partial(_dsa_index_body, tq=tq, wk=wk, n_kx=n_kx, n_main=n_main,
                             tail_w=tail_w, topk=topk, pos0=pos0,
                             n_bucket=min(4, nq) if not tail_w else 1)
    return _call(body, grid=(b, nq, nj), in_specs=in_specs,
                 out_specs=pl.BlockSpec((1, 1, tq, s_pad), ix(lambda bi, qb, j: (bi, 0, qb, 0))),
                 out_shape=jax.ShapeDtypeStruct((b, 1, t, s_pad), dtype),
                 scratch=[pltpu.VMEM((s_pad // LANES, tq, LANES), I32),
                          pltpu.VMEM((tq, LANES), I32), pltpu.VMEM((tq, LANES), F32)],
                 sem=("parallel", "parallel", "arbitrary"), name="dsa_index",
                 prefetch=prefetch)(*args)


def _flash_init(q_ref, q_scr, m_scr, l_scr, acc_scr, groups, rep, tq):
    m_scr[...] = jnp.full(m_scr.shape, NEG, F32)
    l_scr[...] = jnp.zeros(l_scr.shape, F32)
    acc_scr[...] = jnp.zeros(acc_scr.shape, F32)
    for g in range(groups):
        for r in range(rep):
            h = g * rep + r
            q_scr[g, r * tq:(r + 1) * tq, :] = (q_ref[0, :, h * HD:(h + 1) * HD] * QSCALE).astype(MXU)


def _flash_step(g, k, v, bias, q_scr, m_scr, l_scr, acc_scr, rep):
    s = _dot_nt(q_scr[g], k) + jnp.concatenate([bias.astype(F32)] * rep, axis=0)
    m_prev = m_scr[g]
    m_new = jnp.maximum(m_prev, jnp.max(s, axis=1, keepdims=True))
    p = jnp.exp2(s - jnp.concatenate([m_new] * (k.shape[0] // LANES), axis=1))
    a = jnp.exp2(m_prev - m_new)
    l_scr[g] = a * l_scr[g] + jnp.sum(p, axis=1, keepdims=True)
    acc_scr[g] = a * acc_scr[g] + _dot(p, v)
    m_scr[g] = m_new


def _flash_final(o_ref, m_scr, l_scr, acc_scr, groups, rep, tq):
    for g in range(groups):
        for r in range(rep):
            h = g * rep + r
            rows = slice(r * tq, (r + 1) * tq)
            seen = m_scr[g][rows, :] > 0.5 * NEG
            o_ref[0, :, h * HD:(h + 1) * HD] = jnp.where(
                seen, acc_scr[g][rows, :] / l_scr[g][rows, :], 0.0)


def _flash_scratch(groups, rep, tq):
    rows = rep * tq
    return [pltpu.VMEM((groups, rows, HD), MXU),
            pltpu.VMEM((groups, rows, LANES), F32),
            pltpu.VMEM((groups, rows, LANES), F32),
            pltpu.VMEM((groups, rows, HD), F32)]


def _attn_body(*refs, groups, rep, tq, tk, nj, pos0, k_start, band, lead, gm):
    q_ref, kv_ref = refs[0], refs[1]
    bias_ref = None if band else refs[2]
    o_ref, q_scr, m_scr, l_scr, acc_scr = refs[2 if band else 3:]
    qb = pl.program_id(1)
    j = pl.program_id(2)
    kb = qb - lead + j if band else j

    @pl.when(j == 0)
    def _():
        _flash_init(q_ref, q_scr, m_scr, l_scr, acc_scr, groups, rep, tq)

    last_q = pos0 + qb * tq + tq - 1

    @pl.when((kb >= 0) & (k_start + kb * tk <= last_q))
    def _():
        kv = kv_ref[0]
        if band:
            qpos = pos0 + qb * tq + lax.broadcasted_iota(I32, (tq, tk), 0)
            kpos = k_start + kb * tk + lax.broadcasted_iota(I32, (tq, tk), 1)
            bias = jnp.where(kpos <= qpos, jnp.where(kpos > qpos - WINDOW, 0.0, NEG), NEG)
        for g in range(groups):
            if not band:
                bias = bias_ref[0, g if gm > 1 else 0]
            _flash_step(g, kv[:, g * HD:(g + 1) * HD],
                        kv[:, (groups + g) * HD:(groups + g + 1) * HD],
                        bias, q_scr, m_scr, l_scr, acc_scr, rep)

    @pl.when(j == nj - 1)
    def _():
        _flash_final(o_ref, m_scr, l_scr, acc_scr, groups, rep, tq)


def _attn(q, kv, *, groups, rep, tq, tk, pos0, bias=None, k_start=0):
    b, t, qw = q.shape
    band = bias is None
    cw = kv.shape[2]
    nq = t // tq
    lead = WINDOW // tk if (band and nq > 1) else 0
    nj = (lead + 1) if (band and nq > 1) else kv.shape[1] // tk

    def kblock(qb, j):
        if band:
            return jnp.maximum(qb - lead + j, 0)
        return jnp.minimum(j, (pos0 + (qb + 1) * tq - 1) // tk)

    in_specs = [pl.BlockSpec((1, tq, qw), lambda bi, qb, j: (bi, qb, 0)),
                pl.BlockSpec((1, tk, cw), lambda bi, qb, j: (bi, kblock(qb, j), 0))]
    args = [q, kv]
    gm = 1
    if not band:
        gm = bias.shape[1]
        in_specs.append(pl.BlockSpec((1, gm, tq, tk), lambda bi, qb, j: (bi, 0, qb, kblock(qb, j))))
        args.append(bias)
    body = functools.partial(_attn_body, groups=groups, rep=rep, tq=tq, tk=tk, nj=nj, pos0=pos0,
                             k_start=k_start, band=band, lead=lead, gm=gm)
    return _call(body, grid=(b, nq, nj), in_specs=in_specs,
                 out_specs=pl.BlockSpec((1, tq, qw), lambda bi, qb, j: (bi, qb, 0)),
                 out_shape=jax.ShapeDtypeStruct((b, t, qw), F32),
                 scratch=_flash_scratch(groups, rep, tq),
                 sem=("parallel", "parallel", "arbitrary"),
                 name="attn_band" if band else "attn_bias")(*args)


def _attn_paged_body(_, q_ref, *refs, groups, rep, tq, page, n_pg, n_main, gm):
    pages = refs[:n_pg]
    tail_ref, bias_ref, bias_tail_ref, o_ref, q_scr, m_scr, l_scr, acc_scr = refs[n_pg:]
    j = pl.program_id(2)
    rpt = 2 * groups

    @pl.when(j == 0)
    def _():
        _flash_init(q_ref, q_scr, m_scr, l_scr, acc_scr, groups, rep, tq)

    @pl.when(j < n_main)
    def _():
        for g in range(groups):
            k = jnp.concatenate([pg[pl.ds(g, page, stride=rpt), :] for pg in pages], axis=0)
            v = jnp.concatenate([pg[pl.ds(groups + g, page, stride=rpt), :] for pg in pages], axis=0)
            _flash_step(g, k, v, bias_ref[0, g if gm > 1 else 0], q_scr, m_scr, l_scr, acc_scr, rep)

    @pl.when(j == n_main)
    def _():
        kv = tail_ref[0]
        for g in range(groups):
            _flash_step(g, kv[:, g * HD:(g + 1) * HD],
                        kv[:, (groups + g) * HD:(groups + g + 1) * HD],
                        bias_tail_ref[0, g if gm > 1 else 0], q_scr, m_scr, l_scr, acc_scr, rep)
        _flash_final(o_ref, m_scr, l_scr, acc_scr, groups, rep, tq)


def _attn_paged(q, pool, pt, off, tail, bias, *, groups, rep, page):
    b, t, qw = q.shape
    n_pages = pt.shape[1]
    n_pg = _tile(n_pages, PAGES_PER_STEP, 1)
    n_main = n_pages // n_pg
    rpt = 2 * groups
    gm = bias.shape[1]
    ix = lambda f: (lambda bi, qb, j, p: f(bi, qb, j))
    page_specs = [pl.BlockSpec(
        (page * rpt, LANES),
        functools.partial(lambda u, bi, qb, j, p:
                          (off + p[bi, jnp.minimum(j, n_main - 1) * n_pg + u], 0), u))
        for u in range(n_pg)]
    in_specs = ([pl.BlockSpec((1, t, qw), ix(lambda bi, qb, j: (bi, 0, 0)))] + page_specs + [
        pl.BlockSpec((1, page, tail.shape[2]), ix(lambda bi, qb, j: (bi, 0, 0))),
        pl.BlockSpec((1, gm, t, n_pg * page),
                     ix(lambda bi, qb, j: (bi, 0, 0, jnp.minimum(j, n_main - 1)))),
        pl.BlockSpec((1, gm, t, page), ix(lambda bi, qb, j: (bi, 0, 0, n_pages)))])
    body = functools.partial(_attn_paged_body, groups=groups, rep=rep, tq=t, page=page,
                             n_pg=n_pg, n_main=n_main, gm=gm)
    return _call(body, grid=(b, 1, n_main + 1), in_specs=in_specs,
                 out_specs=pl.BlockSpec((1, t, qw), ix(lambda bi, qb, j: (bi, 0, 0))),
                 out_shape=jax.ShapeDtypeStruct((b, t, qw), F32),
                 scratch=_flash_scratch(groups, rep, t),
                 sem=("parallel", "parallel", "arbitrary"), name="attn_paged",
                 prefetch=pt)(q, *([pool] * n_pg), tail, bias, bias)


def _nsa_cmp_body(q_ref, kc_ref, cc_ref, cs_ref, oc_ref, bias_ref, *, tq, nc, ns, nsp, s_pad,
                  chunk, pos0):
    qb = pl.program_id(1)
    half = nc // 2
    rep = H_B // KV_B
    kc = kc_ref[0]
    cos = cc_ref[...]
    sin = cs_ref[...]
    qpos1 = pos0 + qb * tq + lax.broadcasted_iota(I32, (tq, 1), 0)
    n_perm = lax.broadcasted_iota(I32, (tq, nc), 1)
    n_orig = jnp.where(n_perm < half, 2 * n_perm, 2 * (n_perm - half) + 1)
    c_ok = jnp.where(((n_orig + 1) * CMP_BLOCK - 1) <= qpos1, 1.0, 0.0)
    c_okr = jnp.concatenate([c_ok] * rep, axis=0) > 0.5
    blk = lax.broadcasted_iota(I32, (tq, nsp), 1)
    sel_shift = SEL_BLOCK.bit_length() - 1
    cur = qpos1 >> sel_shift
    forced = (blk == 0) | (blk == cur) | (blk == cur - 1)
    b_causal = blk * SEL_BLOCK <= qpos1
    vals = []
    for g in range(KV_B):
        k_raw = kc[:, g * HD:(g + 1) * HD]
        k_g = k_raw * cos + pltpu.roll(k_raw, HD // 2, 1) * sin
        v_g = kc[:, (KV_B + g) * HD:(KV_B + g + 1) * HD]
        qg = jnp.concatenate(
            [q_ref[0, :, (g * rep + r) * HD:(g * rep + r + 1) * HD] for r in range(rep)], axis=0)
        s = _dot_nt(qg, k_g) * SCALE
        s = jnp.where(c_okr, s, NEG)
        m = jnp.max(s, axis=1, keepdims=True)
        p = jnp.where(c_okr, jnp.exp(s - m), 0.0)
        den = jnp.sum(p, axis=1, keepdims=True)
        p = p / jnp.where(den > 0.0, den, 1.0)
        o = _dot(p, v_g)
        imp_c = p[0:tq]
        for r in range(rep):
            oc_ref[0, :, (g * rep + r) * HD:(g * rep + r + 1) * HD] = o[r * tq:(r + 1) * tq]
            if r > 0:
                imp_c = imp_c + p[r * tq:(r + 1) * tq]
        imp = imp_c[:, :half] + imp_c[:, half:]
        if nsp > half:
            imp = jnp.concatenate([imp, jnp.zeros((tq, nsp - half), F32)], axis=1)
        vals.append(jnp.where(forced, jnp.inf, jnp.where(b_causal, imp, -jnp.inf)))

    def ranks(val, cols, idx):
        rank = jnp.zeros(val.shape, F32)
        for i in range(ns):
            low = jnp.where(idx > i, 1.0, 0.0)
            col = cols(i)
            rank = rank + jnp.where(col > val, 1.0, jnp.where(col == val, low, 0.0))
        return jnp.where(rank < min(N_SEL, ns), 1.0, 0.0)

    span = LANES // KV_B
    if nsp == LANES and ns <= span:
        packed = vals[0]
        for g in range(1, KV_B):
            packed = jnp.where(blk >= g * span, pltpu.roll(vals[g], g * span, 1), packed)

        def cols(i):
            col = vals[0][:, i:i + 1]
            for g in range(1, KV_B):
                col = jnp.where(blk >= g * span, vals[g][:, i:i + 1], col)
            return col

        picked = ranks(packed, cols, blk & (span - 1))
        chosen_g = [picked if g == 0 else pltpu.roll(picked, LANES - g * span, 1)
                    for g in range(KV_B)]
        chosen_g = [jnp.where(blk < span, c, 0.0) for c in chosen_g]
    else:
        chosen_g = [ranks(v, lambda i, v=v: v[:, i:i + 1], blk) for v in vals]

    chosen_g = [c.astype(BF16) for c in chosen_g]
    row_blk = lax.broadcasted_iota(I32, (nsp, chunk), 0)
    for c in range(s_pad // chunk):
        tok_b = c * chunk + lax.broadcasted_iota(I32, (nsp, chunk), 1)
        expand = jnp.where((tok_b >> sel_shift) == row_blk, 1.0, 0.0).astype(BF16)
        tok = c * chunk + lax.broadcasted_iota(I32, (tq, chunk), 1)
        for g in range(KV_B):
            hit = jnp.dot(chosen_g[g], expand, preferred_element_type=F32)
            bias_ref[0, g, :, c * chunk:(c + 1) * chunk] = jnp.where(
                tok <= qpos1, jnp.where(hit > 0.5, 0.0, NEG), NEG).astype(bias_ref.dtype)


def _nsa_cmp(qb, kc, ctab, *, s_len, s_pad, pos0, tq):
    b, t, qw = qb.shape
    nc = kc.shape[1]
    ns = -(-s_len // SEL_BLOCK)
    nsp = -(-ns // LANES) * LANES
    chunk = next(c for c in (512, 384, 256, 128) if s_pad % c == 0)
    dtype = BF16 if tq % 16 == 0 else F32
    body = functools.partial(_nsa_cmp_body, tq=tq, nc=nc, ns=ns, nsp=nsp, s_pad=s_pad,
                             chunk=chunk, pos0=pos0)
    return pl.pallas_call(
        body,
        grid=(b, t // tq),
        in_specs=[pl.BlockSpec((1, tq, qw), lambda bi, qi: (bi, qi, 0)),
                  pl.BlockSpec((1, nc, kc.shape[2]), lambda bi, qi: (bi, 0, 0)),
                  pl.BlockSpec((nc, HD), lambda bi, qi: (0, 0)),
                  pl.BlockSpec((nc, HD), lambda bi, qi: (0, 0))],
        out_specs=[pl.BlockSpec((1, tq, qw), lambda bi, qi: (bi, qi, 0)),
                   pl.BlockSpec((1, KV_B, tq, s_pad), lambda bi, qi: (bi, 0, qi, 0))],
        out_shape=[jax.ShapeDtypeStruct((b, t, qw), F32),
                   jax.ShapeDtypeStruct((b, KV_B, t, s_pad), dtype)],
        compiler_params=_params(("parallel", "parallel")),
        name="nsa_cmp",
    )(qb, kc, *ctab)


def _cmp_rows_body(_, *refs, per_step, per_page, rpt):
    o_ref = refs[per_step]
    stride = CMP_BLOCK * rpt
    for i in range(stride):
        piece = jnp.concatenate([pg[pl.ds(i, per_page, stride=stride), :] for pg in refs[:per_step]],
                                axis=0)
        o_ref[0, :, i * LANES:(i + 1) * LANES] = piece


def _cmp_rows(pool, pt, off, *, page, rpt, per_step=PAGES_PER_STEP):
    b, n_pages = pt.shape
    per_step = _tile(n_pages, per_step, 1)
    per_page = page // CMP_BLOCK
    width = CMP_BLOCK * rpt * LANES

    def spec(u):
        return pl.BlockSpec((page * rpt, LANES),
                            lambda bi, i, p: (off + p[bi, i * per_step + u], 0))

    return _call(functools.partial(_cmp_rows_body, per_step=per_step, per_page=per_page, rpt=rpt),
                 grid=(b, n_pages // per_step), in_specs=[spec(u) for u in range(per_step)],
                 out_specs=pl.BlockSpec((1, per_step * per_page, width), lambda bi, i, p: (bi, i, 0)),
                 out_shape=jax.ShapeDtypeStruct((b, n_pages * per_page, width), F32),
                 sem=("parallel", "arbitrary"), name="cmp_rows",
                 prefetch=pt)(*([pool] * per_step))


def _mix_ln_body(oa_ref, oc_ref, os_ref, ow_ref, misc_ref, w_ref, r_ref, g_ref, b_ref, o_ref):
    gate = misc_ref[...]
    gated = []
    for h in range(H_B):
        c0 = MISC_G + 3 * h
        sl = slice(h * HD, (h + 1) * HD)
        gated.append(gate[:, c0:c0 + 1] * oc_ref[:, sl] + gate[:, c0 + 1:c0 + 2] * os_ref[:, sl]
                     + gate[:, c0 + 2:c0 + 3] * ow_ref[:, sl])
    wa = H_A * HD
    y = _dot(oa_ref[...], w_ref[0:wa, :]) + _dot(jnp.concatenate(gated, axis=1), w_ref[wa:, :])
    o_ref[...] = _ln(ALPHA * r_ref[...] + y, g_ref[...], b_ref[...])


def _mix_ln(o_a, o_c, o_s, o_w, misc, w, res, g, b, *, layer, tm=512):
    m = o_a.shape[0]
    d = w.shape[-1]
    tm = _tile(m, tm, 8)
    wa, wb = H_A * HD, H_B * HD
    return pl.pallas_call(
        _mix_ln_body,
        grid=(m // tm,),
        in_specs=[pl.BlockSpec((tm, wa), lambda i: (i, 0))]
        + [pl.BlockSpec((tm, wb), lambda i: (i, 0))] * 3
        + [pl.BlockSpec((tm, LANES), lambda i: (i, 0)),
           _wspec(w, layer, (wa + wb, d), lambda i: (0, 0)),
           pl.BlockSpec((tm, d), lambda i: (i, 0)),
           pl.BlockSpec((1, d), lambda i: (0, 0)),
           pl.BlockSpec((1, d), lambda i: (0, 0))],
        out_specs=pl.BlockSpec((tm, d), lambda i: (i, 0)),
        out_shape=jax.ShapeDtypeStruct((m, d), F32),
        compiler_params=_params(("parallel",)),
        name="mix_ln",
    )(o_a, o_c, o_s, o_w, misc, w, res, g.reshape(1, d), b.reshape(1, d))


def _sb_init(q_ref, q_scr, c_scr, acc_scr, tq):
    rep = H_C // KV_C
    c_scr[...] = jnp.zeros(c_scr.shape, F32)
    acc_scr[...] = jnp.zeros(acc_scr.shape, F32)
    for g in range(KV_C):
        for r in range(rep):
            h = g * rep + r
            q_scr[g, r * tq:(r + 1) * tq, :] = (q_ref[0, :, h * HD:(h + 1) * HD] * QSCALE).astype(MXU)


def _sb_block(ks, vs, before, q_scr, c_scr, acc_scr):
    rep = H_C // KV_C
    rows = q_scr.shape[1]
    nchunk = ks[0].shape[0] // LANES
    older = None if before is None else jnp.concatenate([before] * rep, axis=0) > 0.5
    ys, pieces = [], []
    for g in range(KV_C):
        y = _dot_nt(q_scr[g], ks[g])
        drop = jnp.maximum(y, 0.0) + jnp.log2(1.0 + jnp.exp2(-jnp.abs(y)))
        if older is not None:
            drop = jnp.where(older, drop, 0.0)
        ys.append(y)
        pieces += [drop[:, c * LANES:(c + 1) * LANES] for c in range(nchunk)]
    stacked = jnp.concatenate(pieces, axis=0)
    hi = stacked.astype(BF16)
    lo = (stacked - hi.astype(F32)).astype(BF16)
    r_i = lax.broadcasted_iota(I32, (LANES, LANES), 0)
    c_i = lax.broadcasted_iota(I32, (LANES, LANES), 1)
    later = jnp.where(r_i >= c_i, -1.0, 0.0).astype(BF16)
    incl = jnp.dot(jnp.concatenate([hi, lo], axis=1), jnp.concatenate([later, later], axis=0),
                   preferred_element_type=F32)
    for g in range(KV_C):
        run = c_scr[g]
        cols = [None] * nchunk
        for c in reversed(range(nchunk)):
            part = incl[(g * nchunk + c) * rows:(g * nchunk + c + 1) * rows]
            cols[c] = part + run
            run = run + jnp.broadcast_to(part[:, :1], part.shape)
        a = jnp.exp2(ys[g] + jnp.concatenate(cols, axis=1))
        if older is not None:
            a = jnp.where(older, a, 0.0)
        acc_scr[g] = acc_scr[g] + _dot(a, vs[g])
        c_scr[g] = run


def _sb_final(o_ref, acc_scr, tq):
    rep = H_C // KV_C
    for g in range(KV_C):
        for r in range(rep):
            h = g * rep + r
            o_ref[0, :, h * HD:(h + 1) * HD] = acc_scr[g][r * tq:(r + 1) * tq, :]


def _sb_scratch(tq):
    rows = (H_C // KV_C) * tq
    return [pltpu.VMEM((KV_C, rows, HD), MXU),
            pltpu.VMEM((KV_C, rows, LANES), F32),
            pltpu.VMEM((KV_C, rows, HD), F32)]


def _split_kv(kv):
    ks = [kv[:, g * HD:(g + 1) * HD] for g in range(KV_C)]
    vs = [kv[:, (KV_C + g) * HD:(KV_C + g + 1) * HD] for g in range(KV_C)]
    return ks, vs


def _older_mask(tq, tk, qpos0, kpos0):
    qpos = qpos0 + lax.broadcasted_iota(I32, (tq, tk), 0)
    kpos = kpos0 + lax.broadcasted_iota(I32, (tq, tk), 1)
    return jnp.where(kpos < qpos, 1.0, 0.0)


def _key_norm_body(kv_ref, o_ref):
    @pl.when(pl.program_id(1) == 0)
    def _():
        o_ref[...] = jnp.zeros(o_ref.shape, F32)

    kv = kv_ref[0]
    for g in range(KV_C):
        k = kv[:, g * HD:(g + 1) * HD]
        top = jnp.max(jnp.sum(k * k, axis=1, keepdims=True), axis=0, keepdims=True)
        o_ref[0, g:g + 1, :] = jnp.maximum(o_ref[0, g:g + 1, :], jnp.broadcast_to(top, (1, LANES)))


def _key_norm(kv, tk):
    b, s_len, cw = kv.shape
    return pl.pallas_call(
        _key_norm_body,
        grid=(b, s_len // tk),
        in_specs=[pl.BlockSpec((1, tk, cw), lambda bi, j: (bi, j, 0))],
        out_specs=pl.BlockSpec((1, 8, LANES), lambda bi, j: (bi, 0, 0)),
        out_shape=jax.ShapeDtypeStruct((b, 8, LANES), F32),
        compiler_params=_params(("parallel", "arbitrary")),
        name="key_norm",
    )(kv)


SB_DEAD = -160.0
NORM_SLACK = 1.02


def _sb_body(q_ref, kv_ref, kn_ref, o_ref, q_scr, c_scr, acc_scr, qn_scr, done_ref, *,
             tq, tk, nj, pos0):
    rep = H_C // KV_C
    qb = pl.program_id(1)
    j = pl.program_id(2)
    kb = (pos0 + (qb + 1) * tq - 1) // tk - j
    q0 = pos0 + qb * tq

    @pl.when(j == 0)
    def _():
        _sb_init(q_ref, q_scr, c_scr, acc_scr, tq)
        done_ref[0] = 0
        for g in range(KV_C):
            for r in range(rep):
                h = g * rep + r
                qs = q_ref[0, :, h * HD:(h + 1) * HD] * QSCALE
                qn_scr[g, r * tq:(r + 1) * tq, :] = jnp.broadcast_to(
                    jnp.sqrt(jnp.sum(qs * qs, axis=1, keepdims=True)), (tq, LANES))

    live = (kb >= 0) & (done_ref[0] == 0)

    @pl.when(live & ((kb + 1) * tk > q0))
    def _():
        ks, vs = _split_kv(kv_ref[0])
        _sb_block(ks, vs, _older_mask(tq, tk, q0, kb * tk), q_scr, c_scr, acc_scr)

    @pl.when(live & ((kb + 1) * tk <= q0))
    def _():
        ks, vs = _split_kv(kv_ref[0])
        _sb_block(ks, vs, None, q_scr, c_scr, acc_scr)

    @pl.when(live)
    def _():
        worst = None
        for g in range(KV_C):
            reach = qn_scr[g] * (jnp.sqrt(kn_ref[0, g:g + 1, :]) * NORM_SLACK) + c_scr[g]
            top = jnp.max(reach)
            worst = top if worst is None else jnp.maximum(worst, top)
        done_ref[0] = jnp.where(worst < SB_DEAD, 1, 0).astype(I32)

    @pl.when(j == nj - 1)
    def _():
        _sb_final(o_ref, acc_scr, tq)


def _sb_attn(q, kv, *, tq, tk, pos0):
    b, t, qw = q.shape
    cw = kv.shape[2]
    nj = kv.shape[1] // tk
    rows = (H_C // KV_C) * tq
    body = functools.partial(_sb_body, tq=tq, tk=tk, nj=nj, pos0=pos0)
    return _call(
        body, grid=(b, t // tq, nj),
        in_specs=[pl.BlockSpec((1, tq, qw), lambda bi, qb, j: (bi, qb, 0)),
                  pl.BlockSpec((1, tk, cw), lambda bi, qb, j: (
                      bi, jnp.maximum((pos0 + (qb + 1) * tq - 1) // tk - j, 0), 0)),
                  pl.BlockSpec((1, 8, LANES), lambda bi, qb, j: (bi, 0, 0))],
        out_specs=pl.BlockSpec((1, tq, qw), lambda bi, qb, j: (bi, qb, 0)),
        out_shape=jax.ShapeDtypeStruct((b, t, qw), F32),
        scratch=_sb_scratch(tq) + [pltpu.VMEM((KV_C, rows, LANES), F32),
                                   pltpu.SMEM((1,), I32)],
        sem=("parallel", "parallel", "arbitrary"), name="sb_attn")(q, kv, _key_norm(kv, tk))


def _sb_paged_body(_, q_ref, *refs, tq, page, n_pg, n_main, pos0):
    pages = refs[:n_pg]
    tail_ref, o_ref, q_scr, c_scr, acc_scr = refs[n_pg:]
    j = pl.program_id(2)
    rpt = 2 * KV_C

    @pl.when(j == 0)
    def _():
        _sb_init(q_ref, q_scr, c_scr, acc_scr, tq)
        ks, vs = _split_kv(tail_ref[0])
        _sb_block(ks, vs, _older_mask(tq, page, pos0, n_main * n_pg * page), q_scr, c_scr, acc_scr)

    @pl.when(j > 0)
    def _():
        ks = [jnp.concatenate([pg[pl.ds(g, page, stride=rpt), :] for pg in pages], axis=0)
              for g in range(KV_C)]
        vs = [jnp.concatenate([pg[pl.ds(KV_C + g, page, stride=rpt), :] for pg in pages], axis=0)
              for g in range(KV_C)]
        _sb_block(ks, vs, None, q_scr, c_scr, acc_scr)

    @pl.when(j == n_main)
    def _():
        _sb_final(o_ref, acc_scr, tq)


def _sb_attn_paged(q, pool, pt, off, tail, *, page, pos0, pages_per_step=8):
    b, t, qw = q.shape
    n_pages = pt.shape[1]
    n_pg = _tile(n_pages, pages_per_step, 1)
    n_main = n_pages // n_pg
    rpt = 2 * KV_C
    ix = lambda f: (lambda bi, qb, j, p: f(bi, qb, j))
    page_specs = [pl.BlockSpec(
        (page * rpt, LANES),
        functools.partial(lambda u, bi, qb, j, p:
                          (off + p[bi, jnp.clip(n_main - j, 0, n_main - 1) * n_pg + u], 0), u))
        for u in range(n_pg)]
    in_specs = ([pl.BlockSpec((1, t, qw), ix(lambda bi, qb, j: (bi, 0, 0)))] + page_specs
                + [pl.BlockSpec((1, page, tail.shape[2]), ix(lambda bi, qb, j: (bi, 0, 0)))])
    body = functools.partial(_sb_paged_body, tq=t, page=page, n_pg=n_pg, n_main=n_main, pos0=pos0)
    return _call(body, grid=(b, 1, n_main + 1), in_specs=in_specs,
                 out_specs=pl.BlockSpec((1, t, qw), ix(lambda bi, qb, j: (bi, 0, 0))),
                 out_shape=jax.ShapeDtypeStruct((b, t, qw), F32), scratch=_sb_scratch(t),
                 sem=("parallel", "parallel", "arbitrary"), name="sb_attn_paged",
                 prefetch=pt)(q, *([pool] * n_pg), tail)


def _mem_body(x_ref, kv_ref, wq_ref, wo_ref, g_ref, b_ref, o_ref):
    x = x_ref[0]
    kv = kv_ref[0]
    q = _dot(x, wq_ref[...])
    outs = []
    for h in range(MEM_HEADS):
        s = _dot_nt(q[:, h * HD:(h + 1) * HD], kv[:, h * HD:(h + 1) * HD]) * SCALE
        p = jnp.exp(s - jnp.max(s, axis=1, keepdims=True))
        p = p / jnp.sum(p, axis=1, keepdims=True)
        outs.append(_dot(p, kv[:, (MEM_HEADS + h) * HD:(MEM_HEADS + h + 1) * HD]))
    y = _dot(jnp.concatenate(outs, axis=1), wo_ref[...])
    o_ref[0] = _ln(ALPHA * x + y, g_ref[...], b_ref[...])


def _mem_layer(x, mem_kv, wq, wo, g, b, *, layer=None, tq):
    bsz, t, d = x.shape
    nm, cw = mem_kv.shape[1:]
    return pl.pallas_call(
        _mem_body,
        grid=(bsz, t // tq),
        in_specs=[pl.BlockSpec((1, tq, d), lambda bi, qi: (bi, qi, 0)),
                  pl.BlockSpec((1, nm, cw), lambda bi, qi: (bi, 0, 0)),
                  _wspec(wq, layer, wq.shape[-2:], lambda bi, qi: (0, 0)),
                  _wspec(wo, layer, wo.shape[-2:], lambda bi, qi: (0, 0)),
                  pl.BlockSpec((1, d), lambda bi, qi: (0, 0)),
                  pl.BlockSpec((1, d), lambda bi, qi: (0, 0))],
        out_specs=pl.BlockSpec((1, tq, d), lambda bi, qi: (bi, qi, 0)),
        out_shape=jax.ShapeDtypeStruct((bsz, t, d), F32),
        compiler_params=_params(("parallel", "parallel")),
        name="mem_layer",
    )(x, mem_kv, wq, wo, g.reshape(1, d), b.reshape(1, d))


def _ffn_body(x_ref, wg_ref, wu_ref, wd_ref, g_ref, b_ref, o_ref, xb_ref, acc_ref, *, nf):
    f = pl.program_id(1)

    @pl.when(f == 0)
    def _():
        xb_ref[...] = x_ref[...].astype(MXU)
        acc_ref[...] = jnp.zeros(acc_ref.shape, F32)

    xb = xb_ref[...]
    gate = jnp.dot(xb, wg_ref[...], preferred_element_type=F32)
    up = jnp.dot(xb, wu_ref[...], preferred_element_type=F32)
    hidden = gate / (1.0 + jnp.exp(-gate)) * up
    acc_ref[...] += _dot(hidden, wd_ref[...])

    @pl.when(f == nf - 1)
    def _():
        o_ref[...] = _ln(ALPHA * x_ref[...] + acc_ref[...], g_ref[...], b_ref[...])


def _ffn_layer(x, w_gu, w_d, g, b, *, layer=None, tm=1024):
    m, d = x.shape
    dff = w_d.shape[-2]
    tm = _tile(m, tm, 8)
    tf = _tile(dff, 256 if tm > 512 else 512, LANES)
    nf = dff // tf
    return pl.pallas_call(
        functools.partial(_ffn_body, nf=nf),
        grid=(m // tm, nf),
        in_specs=[pl.BlockSpec((tm, d), lambda i, f: (i, 0)),
                  _wspec(w_gu, layer, (d, tf), lambda i, f: (0, f)),
                  _wspec(w_gu, layer, (d, tf), lambda i, f: (0, nf + f)),
                  _wspec(w_d, layer, (tf, d), lambda i, f: (f, 0)),
                  pl.BlockSpec((1, d), lambda i, f: (0, 0)),
                  pl.BlockSpec((1, d), lambda i, f: (0, 0))],
        out_specs=pl.BlockSpec((tm, d), lambda i, f: (i, 0), pipeline_mode=pl.Buffered(1)),
        out_shape=jax.ShapeDtypeStruct((m, d), F32),
        scratch_shapes=[pltpu.VMEM((tm, d), MXU), pltpu.VMEM((tm, d), F32)],
        compiler_params=_params(("parallel", "arbitrary")),
        name="ffn_layer",
    )(x, w_gu, w_gu, w_d, g.reshape(1, d), b.reshape(1, d))


def _even_odd_rows(a):
    b, n2, c = a.shape
    return a.reshape(b, n2 // 2, 2, c).transpose(0, 2, 1, 3).reshape(b, n2, c)


def _pad_rows(a, rows):
    return jnp.pad(a, ((0, 0), (0, rows - a.shape[1]), (0, 0)))


def _ab_mixer(x, w, layer, past, *, pos0):
    b, t, d = x.shape
    m = b * t
    pos = pos0 + jnp.arange(t, dtype=I32)
    tabs = _rope_tables(pos, HD) + _rope_tables(pos, D_IDX)
    h = _mm(x.reshape(m, d), w["in_ab"][layer], tn=640)
    q_a, kv_a, q_i, q_b, cmp_kv, sel_kv, win_kv, misc = _post_ab(h, tabs, t)
    r3 = lambda a: a.reshape(b, t, a.shape[1])
    q_a, kv_a, q_i, q_b, cmp_kv, sel_kv, win_kv, misc = map(
        r3, (q_a, kv_a, q_i, q_b, cmp_kv, sel_kv, win_kv, misc))
    cw = 2 * KV_B * HD
    rep_a, rep_b = H_A // KV_A, H_B // KV_B
    if past is None:
        s_len = t
        tq = _tile(t, 256, 8)
        tk = _tile(t, 1024, LANES)
        topk = min(TOPK_MAX, s_len // 4)
        bias_a = _dsa_index(q_i, misc, misc, topk=topk, pos0=0, tq=tq, tk=_tile(t, 512, LANES))
        o_a = _attn(q_a, kv_a, groups=KV_A, rep=rep_a, tq=tq, tk=tk, pos0=0, bias=bias_a)
        nc = s_len // CMP_BLOCK
        kc = _mm(cmp_kv.reshape(m // CMP_BLOCK, CMP_BLOCK * cw), w["phi"][layer], tn=cw)
        kc = _even_odd_rows(kc.reshape(b, nc, cw))
        s_pad = s_len
        o_w = _attn(q_b, win_kv, groups=KV_B, rep=rep_b, tq=tq, tk=tq, pos0=0)
        sel_attn = lambda bias: _attn(q_b, sel_kv, groups=KV_B, rep=rep_b, tq=tq, tk=tk,
                                      pos0=0, bias=bias)
        win_state = win_kv[:, t - min(WINDOW, t):]
    else:
        pt, caches, win_prev = past
        n_pages = pt.shape[1]
        page = caches["page"]
        n_pool = caches["n_pool"]
        off = layer * n_pool
        s_len = n_pages * page + t
        s_pad = (n_pages + 1) * page
        tq = t
        topk = min(TOPK_MAX, s_len // 4)
        bias_a = _dsa_index(q_i, misc, caches["a_kidx"], topk=topk, pos0=pos0, tq=tq, tk=page,
                            page=(pt, off, _pad_rows(misc[:, :, :D_IDX], page)))
        o_a = _attn_paged(q_a, caches["a_kv"], pt, off, _pad_rows(kv_a, page), bias_a,
                          groups=KV_A, rep=rep_a, page=page)
        nc = s_len // CMP_BLOCK
        per_page = page // CMP_BLOCK
        assert nc == n_pages * per_page, "new rows must not complete a compressed block"
        raw = _cmp_rows(caches["cmp"], pt, off, page=page, rpt=cw // LANES)
        kc = _mm(raw.reshape(b * nc, raw.shape[2]), w["phi"][layer], tn=cw)
        kc = _even_odd_rows(kc.reshape(b, nc, cw))
        win_all = jnp.concatenate([win_prev, win_kv], axis=1)
        k_start = n_pages * page - win_prev.shape[1]
        o_w = _attn(q_b, _pad_rows(win_all, -(-win_all.shape[1] // page) * page),
                    groups=KV_B, rep=rep_b, tq=tq, tk=page, pos0=pos0, k_start=k_start)
        sel_attn = lambda bias: _attn_paged(q_b, caches["sel"], pt, off, _pad_rows(sel_kv, page),
                                            bias, groups=KV_B, rep=rep_b, page=page)
        win_state = win_all[:, win_all.shape[1] - win_prev.shape[1]:]
    assert nc % 2 == 0
    c_pos = (jnp.arange(nc, dtype=I32) + 1) * CMP_BLOCK - 1
    ctab = tuple(_even_odd_rows(tb[None])[0] for tb in _rope_tables(c_pos, HD))
    o_c, bias_s = _nsa_cmp(q_b, kc, ctab, s_len=s_len, s_pad=s_pad, pos0=pos0, tq=tq)
    o_s = sel_attn(bias_s)
    f2 = lambda a: a.reshape(m, a.shape[2])
    pieces = (f2(o_a), f2(o_c), f2(o_s), f2(o_w), f2(misc))
    states = (kv_a, misc[:, :, :D_IDX], cmp_kv, sel_kv, win_state)
    return pieces, states


def _sb_mixer(x, w, layer, past, *, pos0):
    b, t, d = x.shape
    m = b * t
    x2 = x.reshape(m, d)
    q = _mm(x2, w["in_c"], layer=layer, cols=(0, H_C * HD)).reshape(b, t, H_C * HD)
    kv = _mm(x2, w["in_c"], layer=layer,
             cols=(H_C * HD, 2 * KV_C * HD)).reshape(b, t, 2 * KV_C * HD)
    if past is None:
        tq = _tile(t, 256, 8)
        o = _sb_attn(q, kv, tq=tq, tk=tq, pos0=0)
    else:
        pt, caches = past
        page = caches["page"]
        o = _sb_attn_paged(q, caches["c_kv"], pt, layer * caches["n_pool"], _pad_rows(kv, page),
                           page=page, pos0=pos0)
    return o.reshape(m, H_C * HD), kv


def _run_group(x, w, mem_kvs, pasts, ln_g, ln_b, *, pos0):
    b, t, d = x.shape
    m = b * t
    even, odd = [], []
    for i in range(DEPTH):
        if i % 2 == 0:
            pieces, st = _ab_mixer(x, w, i // 2, pasts[i], pos0=pos0)
            even.append(st)
            x2 = _mix_ln(*pieces, w["out"], x.reshape(m, d), ln_g[i, 0], ln_b[i, 0], layer=i)
        else:
            mix, st = _sb_mixer(x, w, i // 2, pasts[i], pos0=pos0)
            odd.append(st)
            x2 = _mm_ln(mix, w["out"], x.reshape(m, d), ln_g[i, 0], ln_b[i, 0], layer=i)
        x3 = _mem_layer(x2.reshape(b, t, d), mem_kvs[i], w["mem_q"], w["mem_o"],
                        ln_g[i, 1], ln_b[i, 1], layer=i, tq=_tile(t, 256, 8))
        x = _ffn_layer(x3.reshape(m, d), w["gate_up"], w["down"],
                       ln_g[i, 2], ln_b[i, 2], layer=i).reshape(b, t, d)
    return x, even, odd


def _reorder_in_ab(w):
    sizes = (H_A * HD, 2 * KV_A * HD, H_IDX * D_IDX, D_IDX, H_IDX,
             H_B * HD, 2 * KV_B * HD, 2 * KV_B * HD, 2 * KV_B * HD, 3 * H_B)
    offs = np.concatenate([[0], np.cumsum(sizes)])
    piece = lambda i: w[..., offs[i]:offs[i + 1]]
    order = [0, 1, 2, 5, 6, 7, 8, 3, 4, 9]
    cols = [piece(i) for i in order]
    pad = C_END - int(offs[-1])
    cols.append(jnp.zeros(w.shape[:-1] + (pad,), w.dtype))
    return jnp.concatenate(cols, axis=-1)


def _phi_matrix(w_phi):
    phi = w_phi.reshape(2, CMP_BLOCK, HD, HD)
    nb = 2 * KV_B
    per_piece = jnp.stack([phi[c // KV_B] for c in range(nb)], axis=1)
    same = jnp.eye(nb, dtype=bool)[None, :, None, :, None]
    out = jnp.where(same, per_piece[:, :, :, None, :], jnp.zeros((), w_phi.dtype))
    return out.reshape(CMP_BLOCK * nb * HD, nb * HD)


def kernel(x_prompt, x_sample, cache_a_kv, cache_a_kidx, cache_b_cmp_kv, cache_b_sel_kv,
           state_b_win_kv, cache_c_kv, cache_mem_kv, page_table, mem_prompt, w_in_ab, w_cmp_phi,
           w_in_c, w_out, w_mem_q, w_mem_kv, w_mem_o, w_gate_up, w_down, ln_g, ln_b):
    b_p, t_p, d = x_prompt.shape
    b_s, t_s, _ = x_sample.shape
    n_even, n_pool, page = cache_a_kv.shape[:3]
    n_odd = cache_c_kv.shape[0]
    n_mem = mem_prompt.shape[1]
    past_len = page_table.shape[1] * page

    w = {
        "in_ab": [_reorder_in_ab(wl).astype(MXU) for wl in w_in_ab],
        "phi": [_phi_matrix(wl.astype(MXU)) for wl in w_cmp_phi],
        "in_c": w_in_c.astype(MXU),
        "out": w_out.astype(MXU),
        "mem_q": w_mem_q.astype(MXU),
        "mem_o": w_mem_o.astype(MXU),
        "gate_up": w_gate_up.astype(MXU),
        "down": w_down.astype(MXU),
    }
    w_mkv = w_mem_kv.astype(MXU)

    mem_flat = mem_prompt.reshape(b_p * n_mem, d)
    mem_p = [_mm(mem_flat, w_mkv, layer=i).reshape(b_p, n_mem, 2 * MEM_HEADS * HD)
             for i in range(DEPTH)]
    y_p, ev_p, od_p = _run_group(x_prompt, w, mem_p, [None] * DEPTH, ln_g, ln_b, pos0=0)

    cw_b = 2 * KV_B * HD
    caches = {
        "page": page, "n_pool": n_pool,
        "a_kv": cache_a_kv.reshape(-1, LANES),
        "a_kidx": cache_a_kidx.reshape(n_even * n_pool, page, D_IDX),
        "cmp": cache_b_cmp_kv.reshape(-1, LANES),
        "sel": cache_b_sel_kv.reshape(-1, LANES),
    }
    caches_c = {"page": page, "n_pool": n_pool, "c_kv": cache_c_kv.reshape(-1, LANES)}
    win_prev = state_b_win_kv.reshape(n_even, b_s, state_b_win_kv.shape[2], cw_b)
    pasts = []
    for i in range(DEPTH):
        if i % 2 == 0:
            pasts.append((page_table, caches, win_prev[i // 2]))
        else:
            pasts.append((page_table, caches_c))
    mem_s = [cache_mem_kv[i].reshape(b_s, n_mem, 2 * MEM_HEADS * HD) for i in range(DEPTH)]
    y_s, ev_s, od_s = _run_group(x_sample, w, mem_s, pasts, ln_g, ln_b, pos0=past_len)

    def pack(states, bsz):
        def kv5(a, g):
            return a.reshape(bsz, a.shape[1], 2, g, HD)
        return (jnp.stack([kv5(s[0], KV_A) for s in states]),
                jnp.stack([s[1] for s in states]),
                jnp.stack([kv5(s[2], KV_B) for s in states]),
                jnp.stack([kv5(s[3], KV_B) for s in states]),
                jnp.stack([kv5(s[4], KV_B) for s in states]))

    a_p, ki_p, cmp_p, sel_p, win_p = pack(ev_p, b_p)
    a_s, ki_s, cmp_s, sel_s, win_s = pack(ev_s, b_s)
    c_p = jnp.stack([s.reshape(b_p, t_p, 2, KV_C, HD) for s in od_p])
    c_s = jnp.stack([s.reshape(b_s, t_s, 2, KV_C, HD) for s in od_s])
    mem_out = jnp.stack([mk.reshape(b_p, n_mem, 2, MEM_HEADS, HD) for mk in mem_p])
    return (y_p, y_s, a_p, ki_p, cmp_p, sel_p, win_p, c_p, mem_out,
            a_s, ki_s, cmp_s, sel_s, win_s, c_s)
```

```python
import functools
import math

import jax
import jax.numpy as jnp
import numpy as np
from jax import lax
from jax.experimental import pallas as pl
from jax.experimental.pallas import tpu as pltpu

F32 = jnp.float32
BF16 = jnp.bfloat16
I32 = jnp.int32

HD = 128
LANES = 128
H_A, KV_A, H_IDX, D_IDX = 8, 2, 4, 64
H_B, KV_B = 8, 2
H_C, KV_C = 16, 4
MEM_HEADS = 4
TOPK_MAX = 256
CMP_BLOCK, SEL_BLOCK, N_SEL, WINDOW = 32, 64, 16, 512
ROPE_THETA = 10000.0
LN_EPS = 1e-5
DEPTH = 4
ALPHA = (2 * DEPTH) ** 0.25
SCALE = HD ** -0.5
QSCALE = SCALE * math.log2(math.e)
NEG = -1e30
INT_MIN = -2 ** 31
VMEM_LIMIT = 56 * 1024 * 1024
PAGES_PER_STEP = 16

C_QA, C_KVA, C_QI, C_QB, C_CMP, C_SEL, C_WIN, C_MISC, C_END = (
    0, 1024, 1536, 1792, 2816, 3328, 3840, 4352, 4480)
MISC_WI = D_IDX
MISC_G = D_IDX + H_IDX


def _params(sem):
    return pltpu.CompilerParams(dimension_semantics=sem, vmem_limit_bytes=VMEM_LIMIT)


def _tile(n, pref, align):
    t = (min(pref, n) // align) * align
    while t >= align:
        if n % t == 0:
            return t
        t -= align
    return n


def _ln(y, g, b):
    mu = jnp.mean(y, axis=-1, keepdims=True)
    d = y - mu
    var = jnp.mean(d * d, axis=-1, keepdims=True)
    return d * lax.rsqrt(var + LN_EPS) * g + b


MXU = BF16


def _dot(a, b):
    return jnp.dot(a.astype(MXU), b.astype(MXU), preferred_element_type=F32)


def _dot_nt(a, b):
    return lax.dot_general(a.astype(MXU), b.astype(MXU), (((1,), (1,)), ((), ())),
                           preferred_element_type=F32)


def _call(body, *, grid, in_specs, out_specs, out_shape, scratch=(), sem, name, prefetch=None):
    if prefetch is None:
        return pl.pallas_call(body, grid=grid, in_specs=in_specs, out_specs=out_specs,
                              out_shape=out_shape, scratch_shapes=list(scratch),
                              compiler_params=_params(sem), name=name)
    gs = pltpu.PrefetchScalarGridSpec(num_scalar_prefetch=1, grid=grid, in_specs=in_specs,
                                      out_specs=out_specs, scratch_shapes=list(scratch))
    return functools.partial(
        pl.pallas_call(body, grid_spec=gs, out_shape=out_shape, compiler_params=_params(sem),
                       name=name), prefetch)


def _mm_body(x_ref, w_ref, o_ref, acc_ref, *, nk):
    part = _dot(x_ref[...], w_ref[...])
    if nk == 1:
        o_ref[...] = part
        return
    k = pl.program_id(2)

    @pl.when(k == 0)
    def _():
        acc_ref[...] = part

    @pl.when(k > 0)
    def _():
        acc_ref[...] += part

    @pl.when(k == nk - 1)
    def _():
        o_ref[...] = acc_ref[...]


def _wspec(w, layer, block, index):
    if w.ndim == 2:
        return pl.BlockSpec(block, index)
    return pl.BlockSpec((None,) + block, lambda *a: (layer,) + index(*a))


def _mm(x, w, *, layer=None, cols=None, tm=1024, tn=512, tk=2048):
    m, kd = x.shape
    col0, n = cols if cols is not None else (0, w.shape[-1])
    tm = _tile(m, tm, 8)
    tn = _tile(math.gcd(n, col0), tn, LANES)
    tk = _tile(kd, tk, LANES)
    nk = kd // tk
    cb0 = col0 // tn
    return pl.pallas_call(
        functools.partial(_mm_body, nk=nk),
        grid=(m // tm, n // tn, nk),
        in_specs=[pl.BlockSpec((tm, tk), lambda i, j, k: (i, k)),
                  _wspec(w, layer, (tk, tn), lambda i, j, k: (k, cb0 + j))],
        out_specs=pl.BlockSpec((tm, tn), lambda i, j, k: (i, j)),
        out_shape=jax.ShapeDtypeStruct((m, n), F32),
        scratch_shapes=[pltpu.VMEM((tm, tn) if nk > 1 else (8, LANES), F32)],
        compiler_params=_params(("parallel", "parallel", "arbitrary")),
        name="mm",
    )(x, w)


def _mm_ln_body(x_ref, w_ref, r_ref, g_ref, b_ref, o_ref, acc_ref, *, nk):
    k = pl.program_id(1)
    part = _dot(x_ref[...], w_ref[...])
    if nk == 1:
        o_ref[...] = _ln(ALPHA * r_ref[...] + part, g_ref[...], b_ref[...])
        return

    @pl.when(k == 0)
    def _():
        acc_ref[...] = part

    @pl.when(k > 0)
    def _():
        acc_ref[...] += part

    @pl.when(k == nk - 1)
    def _():
        o_ref[...] = _ln(ALPHA * r_ref[...] + acc_ref[...], g_ref[...], b_ref[...])


def _mm_ln(x, w, res, g, b, *, layer=None, tm=512, tk=2048):
    m, kd = x.shape
    d = w.shape[-1]
    tm = _tile(m, tm, 8)
    tk = _tile(kd, tk, LANES)
    nk = kd // tk
    return pl.pallas_call(
        functools.partial(_mm_ln_body, nk=nk),
        grid=(m // tm, nk),
        in_specs=[pl.BlockSpec((tm, tk), lambda i, k: (i, k)),
                  _wspec(w, layer, (tk, d), lambda i, k: (k, 0)),
                  pl.BlockSpec((tm, d), lambda i, k: (i, 0)),
                  pl.BlockSpec((1, d), lambda i, k: (0, 0)),
                  pl.BlockSpec((1, d), lambda i, k: (0, 0))],
        out_specs=pl.BlockSpec((tm, d), lambda i, k: (i, 0)),
        out_shape=jax.ShapeDtypeStruct((m, d), F32),
        scratch_shapes=[pltpu.VMEM((tm, d) if nk > 1 else (8, LANES), F32)],
        compiler_params=_params(("parallel", "arbitrary")),
        name="mm_ln",
    )(x, w, res, g.reshape(1, d), b.reshape(1, d))


def _post_ab_body(h_ref, c128_ref, s128_ref, c64_ref, s64_ref,
                  qa_ref, kva_ref, qi_ref, qb_ref, cmp_ref, sel_ref, win_ref, misc_ref):
    cos = c128_ref[...]
    sin = s128_ref[...]
    ci = c64_ref[...]
    si = s64_ref[...]
    lane = lax.broadcasted_iota(I32, cos.shape, 1)
    first_half = (lane & (D_IDX - 1)) < (D_IDX // 2)

    def rope128(x):
        return x * cos + pltpu.roll(x, HD // 2, 1) * sin

    def rope64(x):
        partner = jnp.where(first_half, pltpu.roll(x, LANES - D_IDX // 2, 1),
                            pltpu.roll(x, D_IDX // 2, 1))
        return x * ci + partner * si

    def head(c0, i):
        return h_ref[:, c0 + i * HD:c0 + (i + 1) * HD]

    for i in range(H_A):
        qa_ref[:, i * HD:(i + 1) * HD] = rope128(head(C_QA, i))
    for i in range(H_B):
        qb_ref[:, i * HD:(i + 1) * HD] = rope128(head(C_QB, i))
    for src, dst, kv in ((C_KVA, kva_ref, KV_A), (C_SEL, sel_ref, KV_B), (C_WIN, win_ref, KV_B)):
        for i in range(kv):
            dst[:, i * HD:(i + 1) * HD] = rope128(head(src, i))
            dst[:, (kv + i) * HD:(kv + i + 1) * HD] = head(src, kv + i)
    cmp_ref[...] = h_ref[:, C_CMP:C_SEL]
    for i in range(H_IDX * D_IDX // LANES):
        qi_ref[:, i * LANES:(i + 1) * LANES] = rope64(head(C_QI, i))
    raw = h_ref[:, C_MISC:C_END]
    misc_ref[...] = jnp.where(
        lane < MISC_WI, rope64(raw),
        jnp.where(lane < MISC_G, raw * (H_IDX ** -0.5),
                  jnp.where(lane < MISC_G + 3 * H_B, 1.0 / (1.0 + jnp.exp(-raw)), 0.0)))


def _post_ab(h, tabs, t):
    m = h.shape[0]
    tm = _tile(t, 256, 8)
    nt = t // tm
    widths = (H_A * HD, 2 * KV_A * HD, H_IDX * D_IDX, H_B * HD, 2 * KV_B * HD,
              2 * KV_B * HD, 2 * KV_B * HD, LANES)
    tab_spec = pl.BlockSpec((tm, LANES), lambda i: (i % nt, 0))
    return pl.pallas_call(
        _post_ab_body,
        grid=(m // tm,),
        in_specs=[pl.BlockSpec((tm, C_END), lambda i: (i, 0))] + [tab_spec] * 4,
        out_specs=[pl.BlockSpec((tm, w), lambda i: (i, 0)) for w in widths],
        out_shape=[jax.ShapeDtypeStruct((m, w), F32) for w in widths],
        compiler_params=_params(("parallel",)),
        name="post_ab",
    )(h, *tabs)


def _rope_tables(pos, d):
    half = d // 2
    inv = ROPE_THETA ** (-(jnp.arange(half, dtype=F32) * 2.0 / d))
    ang = pos.astype(F32)[:, None] * inv[None, :]
    cos, sin = jnp.cos(ang), jnp.sin(ang)
    reps = LANES // d
    return (jnp.tile(jnp.concatenate([cos, cos], axis=1), (1, reps)),
            jnp.tile(jnp.concatenate([-sin, sin], axis=1), (1, reps)))


def _sortable(x):
    bits = lax.bitcast_convert_type(jnp.where(x == 0.0, 0.0, x), I32)
    return bits ^ ((bits >> 31) & 0x7FFFFFFF)


def _dsa_index_body(*refs, tq, wk, n_kx, n_main, tail_w, topk, pos0, n_bucket):
    refs = list(refs)
    if tail_w:
        refs.pop(0)
    qi_ref, wi_ref = refs[0], refs[1]
    kx_refs = refs[2:2 + n_kx]
    i = 2 + n_kx
    tail_ref = None
    if tail_w:
        tail_ref = refs[i]
        i += 1
    bias_ref, key_ref, kth_ref, need_ref = refs[i:i + 4]
    qb = pl.program_id(1)
    j = pl.program_id(2)
    nch = key_ref.shape[0]
    nj = n_main + (1 if tail_w else 0)
    last_q = pos0 + (qb + 1) * tq - 1

    def put_keys(kx, chunk0, width, kpos0):
        qi = qi_ref[0]
        wi = wi_ref[0]
        score = jnp.zeros((tq, width), F32)
        for h in range(H_IDX):
            dots = _dot_nt(qi[:, h * D_IDX:(h + 1) * D_IDX], kx) * (D_IDX ** -0.5)
            score = score + wi[:, MISC_WI + h:MISC_WI + h + 1] * jnp.maximum(dots, 0.0)
        qpos = pos0 + qb * tq + lax.broadcasted_iota(I32, (tq, width), 0)
        kpos = kpos0 + lax.broadcasted_iota(I32, (tq, width), 1)
        keys = jnp.where(kpos <= qpos, _sortable(score), INT_MIN)
        for u in range(width // LANES):
            key_ref[chunk0 + u] = keys[:, u * LANES:(u + 1) * LANES]

    def main_step():
        kx = jnp.concatenate([r[0][:, :D_IDX] for r in kx_refs], axis=0)
        put_keys(kx, j * (wk // LANES), wk, j * wk)

    if tail_w:
        pl.when(j < n_main)(main_step)

        @pl.when(j == n_main)
        def _():
            put_keys(tail_ref[0][:, :D_IDX], n_main * (wk // LANES), tail_w, n_main * wk)
    else:
        pl.when(j * wk <= last_q)(main_step)

        @pl.when(j * wk > last_q)
        def _():
            for u in range(wk // LANES):
                key_ref[j * (wk // LANES) + u] = jnp.full((tq, LANES), INT_MIN, I32)

    def search(lim):
        n_part = max(1, min(4, tq // 64))
        pr = tq // n_part

        def count(pred, p):
            hit = pred(key_ref[0:lim, p * pr:(p + 1) * pr, :]).astype(I32)
            return jnp.sum(jnp.sum(hit, axis=0), axis=1, keepdims=True)

        ans0 = tuple(jnp.where(count(lambda k: k >= 0, p) >= topk, 0, INT_MIN)
                     for p in range(n_part))

        def bit_step(b, ans):
            bit = jnp.left_shift(jnp.int32(1), 30 - b)
            out = []
            for p in range(n_part):
                cand = ans[p] | bit
                out.append(jnp.where(count(lambda k: k >= cand, p) >= topk, cand, ans[p]))
            return tuple(out)

        kth = lax.fori_loop(0, 31, bit_step, ans0)
        for p in range(n_part):
            rows = slice(p * pr, (p + 1) * pr)
            kth_ref[rows, :] = jnp.broadcast_to(kth[p], (pr, LANES))
            need_ref[rows, :] = jnp.broadcast_to(
                (topk - count(lambda k: k > kth[p], p)).astype(F32), (pr, LANES))

    @pl.when(j == nj - 1)
    def _():
        n_live = last_q // LANES + 1
        lims = [nch * (i + 1) // n_bucket for i in range(n_bucket)]
        for i, lim in enumerate(lims):
            lo = lims[i - 1] if i else 0
            pl.when((n_live > lo) & ((n_live <= lim) | (i == n_bucket - 1)))(
                functools.partial(search, lim))
        kth = kth_ref[...]
        need = need_ref[...]
        r_i = lax.broadcasted_iota(I32, (LANES, LANES), 0)
        c_i = lax.broadcasted_iota(I32, (LANES, LANES), 1)
        strict = jnp.where(r_i < c_i, 1.0, 0.0).astype(BF16)
        ones = jnp.ones((LANES, LANES), BF16)
        seen = jnp.zeros((tq, LANES), F32)
        for c in range(nch):
            kc = key_ref[c]
            tie = jnp.where(kc == kth, 1.0, 0.0).astype(BF16)
            rank = seen + jnp.dot(tie, strict, preferred_element_type=F32)
            tie_bias = jnp.where(kc == kth, jnp.where(rank < need, 0.0, NEG), NEG)
            take = jnp.where(kc > kth, 0.0, tie_bias)
            bias_ref[0, 0, :, c * LANES:(c + 1) * LANES] = (
                jnp.where(kc > INT_MIN, take, NEG).astype(bias_ref.dtype))
            seen = seen + jnp.dot(tie, ones, preferred_element_type=F32)


def _dsa_index(qi, misc, kx, *, topk, pos0, tq, tk, page=None):
    b, t, _ = qi.shape
    nq = t // tq
    if page is not None:
        pt, off, tail = page
        n_pages = pt.shape[1]
        n_kx = _tile(n_pages, PAGES_PER_STEP, 1)
        n_main = n_pages // n_kx
        tail_w = tk
        wk = n_kx * tk
        ix = lambda f: (lambda bi, qb, j, p: f(bi, qb, j))
        kx_specs = [pl.BlockSpec(
            (1, tk, kx.shape[2]),
            functools.partial(lambda u, bi, qb, j, p:
                              (off + p[bi, jnp.minimum(j, n_main - 1) * n_kx + u], 0, 0), u))
            for u in range(n_kx)]
        extra = [pl.BlockSpec((1, tk, tail.shape[2]), ix(lambda bi, qb, j: (bi, 0, 0)))]
        args = [qi, misc] + [kx] * n_kx + [tail]
        prefetch = pt
        s_pad = n_pages * tk + tail_w
    else:
        n_kx, n_main, tail_w, wk = 1, kx.shape[1] // tk, 0, tk
        ix = lambda f: f
        kx_specs = [pl.BlockSpec((1, tk, kx.shape[2]), lambda bi, qb, j: (bi, j, 0))]
        extra = []
        args = [qi, misc, kx]
        prefetch = None
        s_pad = kx.shape[1]
    nj = n_main + (1 if tail_w else 0)
    dtype = BF16 if tq % 16 == 0 else F32
    in_specs = [pl.BlockSpec((1, tq, qi.shape[2]), ix(lambda bi, qb, j: (bi, qb, 0))),
                pl.BlockSpec((1, tq, LANES), ix(lambda bi, qb, j: (bi, qb, 0)))] + kx_specs + extra
    body = functools.partial(_dsa_index_body, tq=tq, wk=wk, n_kx=n_kx, n_main=n_main,
                             tail_w=tail_w, topk=topk, pos0=pos0,
                             n_bucket=min(8, nq) if not tail_w else 1)
    return _call(body, grid=(b, nq, nj), in_specs=in_specs,
                 out_specs=pl.BlockSpec((1, 1, tq, s_pad), ix(lambda bi, qb, j: (bi, 0, qb, 0))),
                 out_shape=jax.ShapeDtypeStruct((b, 1, t, s_pad), dtype),
                 scratch=[pltpu.VMEM((s_pad // LANES, tq, LANES), I32),
                          pltpu.VMEM((tq, LANES), I32), pltpu.VMEM((tq, LANES), F32)],
                 sem=("parallel", "parallel", "arbitrary"), name="dsa_index",
                 prefetch=prefetch)(*args)


def _flash_init(q_ref, q_scr, m_scr, l_scr, acc_scr, groups, rep, tq):
    m_scr[...] = jnp.full(m_scr.shape, NEG, F32)
    l_scr[...] = jnp.zeros(l_scr.shape, F32)
    acc_scr[...] = jnp.zeros(acc_scr.shape, F32)
    for g in range(groups):
        for r in range(rep):
            h = g * rep + r
            q_scr[g, r * tq:(r + 1) * tq, :] = (q_ref[0, :, h * HD:(h + 1) * HD] * QSCALE).astype(MXU)


def _flash_step(g, k, v, bias, q_scr, m_scr, l_scr, acc_scr, rep):
    s = _dot_nt(q_scr[g], k) + jnp.concatenate([bias.astype(F32)] * rep, axis=0)
    m_prev = m_scr[g]
    m_new = jnp.maximum(m_prev, jnp.max(s, axis=1, keepdims=True))
    p = jnp.exp2(s - jnp.concatenate([m_new] * (k.shape[0] // LANES), axis=1))
    a = jnp.exp2(m_prev - m_new)
    l_scr[g] = a * l_scr[g] + jnp.sum(p, axis=1, keepdims=True)
    acc_scr[g] = a * acc_scr[g] + _dot(p, v)
    m_scr[g] = m_new


def _flash_final(o_ref, m_scr, l_scr, acc_scr, groups, rep, tq):
    for g in range(groups):
        for r in range(rep):
            h = g * rep + r
            rows = slice(r * tq, (r + 1) * tq)
            seen = m_scr[g][rows, :] > 0.5 * NEG
            o_ref[0, :, h * HD:(h + 1) * HD] = jnp.where(
                seen, acc_scr[g][rows, :] / l_scr[g][rows, :], 0.0)


def _flash_scratch(groups, rep, tq):
    rows = rep * tq
    return [pltpu.VMEM((groups, rows, HD), MXU),
            pltpu.VMEM((groups, rows, LANES), F32),
            pltpu.VMEM((groups, rows, LANES), F32),
            pltpu.VMEM((groups, rows, HD), F32)]


def _attn_body(*refs, groups, rep, tq, tk, nj, pos0, k_start, band, lead, gm):
    q_ref, kv_ref = refs[0], refs[1]
    bias_ref = None if band else refs[2]
    o_ref, q_scr, m_scr, l_scr, acc_scr = refs[2 if band else 3:]
    qb = pl.program_id(1)
    j = pl.program_id(2)
    kb = qb - lead + j if band else j

    @pl.when(j == 0)
    def _():
        _flash_init(q_ref, q_scr, m_scr, l_scr, acc_scr, groups, rep, tq)

    last_q = pos0 + qb * tq + tq - 1

    @pl.when((kb >= 0) & (k_start + kb * tk <= last_q))
    def _():
        kv = kv_ref[0]
        if band:
            qpos = pos0 + qb * tq + lax.broadcasted_iota(I32, (tq, tk), 0)
            kpos = k_start + kb * tk + lax.broadcasted_iota(I32, (tq, tk), 1)
            bias = jnp.where(kpos <= qpos, jnp.where(kpos > qpos - WINDOW, 0.0, NEG), NEG)
        for g in range(groups):
            if not band:
                bias = bias_ref[0, g if gm > 1 else 0]
            _flash_step(g, kv[:, g * HD:(g + 1) * HD],
                        kv[:, (groups + g) * HD:(groups + g + 1) * HD],
                        bias, q_scr, m_scr, l_scr, acc_scr, rep)

    @pl.when(j == nj - 1)
    def _():
        _flash_final(o_ref, m_scr, l_scr, acc_scr, groups, rep, tq)


def _attn(q, kv, *, groups, rep, tq, tk, pos0, bias=None, k_start=0):
    b, t, qw = q.shape
    band = bias is None
    cw = kv.shape[2]
    nq = t // tq
    lead = WINDOW // tk if (band and nq > 1) else 0
    nj = (lead + 1) if (band and nq > 1) else kv.shape[1] // tk

    def kblock(qb, j):
        if band:
            return jnp.maximum(qb - lead + j, 0)
        return jnp.minimum(j, (pos0 + (qb + 1) * tq - 1) // tk)

    in_specs = [pl.BlockSpec((1, tq, qw), lambda bi, qb, j: (bi, qb, 0)),
                pl.BlockSpec((1, tk, cw), lambda bi, qb, j: (bi, kblock(qb, j), 0))]
    args = [q, kv]
    gm = 1
    if not band:
        gm = bias.shape[1]
        in_specs.append(pl.BlockSpec((1, gm, tq, tk), lambda bi, qb, j: (bi, 0, qb, kblock(qb, j))))
        args.append(bias)
    body = functools.partial(_attn_body, groups=groups, rep=rep, tq=tq, tk=tk, nj=nj, pos0=pos0,
                             k_start=k_start, band=band, lead=lead, gm=gm)
    return _call(body, grid=(b, nq, nj), in_specs=in_specs,
                 out_specs=pl.BlockSpec((1, tq, qw), lambda bi, qb, j: (bi, qb, 0)),
                 out_shape=jax.ShapeDtypeStruct((b, t, qw), F32),
                 scratch=_flash_scratch(groups, rep, tq),
                 sem=("parallel", "parallel", "arbitrary"),
                 name="attn_band" if band else "attn_bias")(*args)


def _attn_paged_body(_, q_ref, *refs, groups, rep, tq, page, n_pg, n_main, gm):
    pages = refs[:n_pg]
    tail_ref, bias_ref, bias_tail_ref, o_ref, q_scr, m_scr, l_scr, acc_scr = refs[n_pg:]
    j = pl.program_id(2)
    rpt = 2 * groups

    @pl.when(j == 0)
    def _():
        _flash_init(q_ref, q_scr, m_scr, l_scr, acc_scr, groups, rep, tq)

    @pl.when(j < n_main)
    def _():
        for g in range(groups):
            k = jnp.concatenate([pg[pl.ds(g, page, stride=rpt), :] for pg in pages], axis=0)
            v = jnp.concatenate([pg[pl.ds(groups + g, page, stride=rpt), :] for pg in pages], axis=0)
            _flash_step(g, k, v, bias_ref[0, g if gm > 1 else 0], q_scr, m_scr, l_scr, acc_scr, rep)

    @pl.when(j == n_main)
    def _():
        kv = tail_ref[0]
        for g in range(groups):
            _flash_step(g, kv[:, g * HD:(g + 1) * HD],
                        kv[:, (groups + g) * HD:(groups + g + 1) * HD],
                        bias_tail_ref[0, g if gm > 1 else 0], q_scr, m_scr, l_scr, acc_scr, rep)
        _flash_final(o_ref, m_scr, l_scr, acc_scr, groups, rep, tq)


def _attn_paged(q, pool, pt, off, tail, bias, *, groups, rep, page):
    b, t, qw = q.shape
    n_pages = pt.shape[1]
    n_pg = _tile(n_pages, PAGES_PER_STEP, 1)
    n_main = n_pages // n_pg
    rpt = 2 * groups
    gm = bias.shape[1]
    ix = lambda f: (lambda bi, qb, j, p: f(bi, qb, j))
    page_specs = [pl.BlockSpec(
        (page * rpt, LANES),
        functools.partial(lambda u, bi, qb, j, p:
                          (off + p[bi, jnp.minimum(j, n_main - 1) * n_pg + u], 0), u))
        for u in range(n_pg)]
    in_specs = ([pl.BlockSpec((1, t, qw), ix(lambda bi, qb, j: (bi, 0, 0)))] + page_specs + [
        pl.BlockSpec((1, page, tail.shape[2]), ix(lambda bi, qb, j: (bi, 0, 0))),
        pl.BlockSpec((1, gm, t, n_pg * page),
                     ix(lambda bi, qb, j: (bi, 0, 0, jnp.minimum(j, n_main - 1)))),
        pl.BlockSpec((1, gm, t, page), ix(lambda bi, qb, j: (bi, 0, 0, n_pages)))])
    body = functools.partial(_attn_paged_body, groups=groups, rep=rep, tq=t, page=page,
                             n_pg=n_pg, n_main=n_main, gm=gm)
    return _call(body, grid=(b, 1, n_main + 1), in_specs=in_specs,
                 out_specs=pl.BlockSpec((1, t, qw), ix(lambda bi, qb, j: (bi, 0, 0))),
                 out_shape=jax.ShapeDtypeStruct((b, t, qw), F32),
                 scratch=_flash_scratch(groups, rep, t),
                 sem=("parallel", "parallel", "arbitrary"), name="attn_paged",
                 prefetch=pt)(q, *([pool] * n_pg), tail, bias, bias)


def _nsa_cmp_body(q_ref, kc_ref, cc_ref, cs_ref, oc_ref, bias_ref, *, tq, nc, ns, nsp, s_pad,
                  chunk, pos0):
    qb = pl.program_id(1)
    half = nc // 2
    rep = H_B // KV_B
    kc = kc_ref[0]
    cos = cc_ref[...]
    sin = cs_ref[...]
    qpos1 = pos0 + qb * tq + lax.broadcasted_iota(I32, (tq, 1), 0)
    n_perm = lax.broadcasted_iota(I32, (tq, nc), 1)
    n_orig = jnp.where(n_perm < half, 2 * n_perm, 2 * (n_perm - half) + 1)
    c_ok = jnp.where(((n_orig + 1) * CMP_BLOCK - 1) <= qpos1, 1.0, 0.0)
    c_okr = jnp.concatenate([c_ok] * rep, axis=0) > 0.5
    blk = lax.broadcasted_iota(I32, (tq, nsp), 1)
    sel_shift = SEL_BLOCK.bit_length() - 1
    cur = qpos1 >> sel_shift
    forced = (blk == 0) | (blk == cur) | (blk == cur - 1)
    b_causal = blk * SEL_BLOCK <= qpos1
    vals = []
    for g in range(KV_B):
        k_raw = kc[:, g * HD:(g + 1) * HD]
        k_g = k_raw * cos + pltpu.roll(k_raw, HD // 2, 1) * sin
        v_g = kc[:, (KV_B + g) * HD:(KV_B + g + 1) * HD]
        qg = jnp.concatenate(
            [q_ref[0, :, (g * rep + r) * HD:(g * rep + r + 1) * HD] for r in range(rep)], axis=0)
        s = _dot_nt(qg, k_g) * SCALE
        s = jnp.where(c_okr, s, NEG)
        m = jnp.max(s, axis=1, keepdims=True)
        p = jnp.where(c_okr, jnp.exp(s - m), 0.0)
        den = jnp.sum(p, axis=1, keepdims=True)
        p = p / jnp.where(den > 0.0, den, 1.0)
        o = _dot(p, v_g)
        imp_c = p[0:tq]
        for r in range(rep):
            oc_ref[0, :, (g * rep + r) * HD:(g * rep + r + 1) * HD] = o[r * tq:(r + 1) * tq]
            if r > 0:
                imp_c = imp_c + p[r * tq:(r + 1) * tq]
        imp = imp_c[:, :half] + imp_c[:, half:]
        if nsp > half:
            imp = jnp.concatenate([imp, jnp.zeros((tq, nsp - half), F32)], axis=1)
        vals.append(jnp.where(forced, jnp.inf, jnp.where(b_causal, imp, -jnp.inf)))

    def ranks(val, cols, idx):
        rank = jnp.zeros(val.shape, F32)
        for i in range(ns):
            low = jnp.where(idx > i, 1.0, 0.0)
            col = cols(i)
            rank = rank + jnp.where(col > val, 1.0, jnp.where(col == val, low, 0.0))
        return jnp.where(rank < min(N_SEL, ns), 1.0, 0.0)

    span = LANES // KV_B
    if nsp == LANES and ns <= span:
        packed = vals[0]
        for g in range(1, KV_B):
            packed = jnp.where(blk >= g * span, pltpu.roll(vals[g], g * span, 1), packed)

        def cols(i):
            col = vals[0][:, i:i + 1]
            for g in range(1, KV_B):
                col = jnp.where(blk >= g * span, vals[g][:, i:i + 1], col)
            return col

        picked = ranks(packed, cols, blk & (span - 1))
        chosen_g = [picked if g == 0 else pltpu.roll(picked, LANES - g * span, 1)
                    for g in range(KV_B)]
        chosen_g = [jnp.where(blk < span, c, 0.0) for c in chosen_g]
    else:
        chosen_g = [ranks(v, lambda i, v=v: v[:, i:i + 1], blk) for v in vals]

    chosen_g = [c.astype(BF16) for c in chosen_g]
    row_blk = lax.broadcasted_iota(I32, (nsp, chunk), 0)
    for c in range(s_pad // chunk):
        tok_b = c * chunk + lax.broadcasted_iota(I32, (nsp, chunk), 1)
        expand = jnp.where((tok_b >> sel_shift) == row_blk, 1.0, 0.0).astype(BF16)
        tok = c * chunk + lax.broadcasted_iota(I32, (tq, chunk), 1)
        for g in range(KV_B):
            hit = jnp.dot(chosen_g[g], expand, preferred_element_type=F32)
            bias_ref[0, g, :, c * chunk:(c + 1) * chunk] = jnp.where(
                tok <= qpos1, jnp.where(hit > 0.5, 0.0, NEG), NEG).astype(bias_ref.dtype)


def _nsa_cmp(qb, kc, ctab, *, s_len, s_pad, pos0, tq):
    b, t, qw = qb.shape
    nc = kc.shape[1]
    ns = -(-s_len // SEL_BLOCK)
    nsp = -(-ns // LANES) * LANES
    chunk = next(c for c in (512, 384, 256, 128) if s_pad % c == 0)
    dtype = BF16 if tq % 16 == 0 else F32
    body = functools.partial(_nsa_cmp_body, tq=tq, nc=nc, ns=ns, nsp=nsp, s_pad=s_pad,
                             chunk=chunk, pos0=pos0)
    return pl.pallas_call(
        body,
        grid=(b, t // tq),
        in_specs=[pl.BlockSpec((1, tq, qw), lambda bi, qi: (bi, qi, 0)),
                  pl.BlockSpec((1, nc, kc.shape[2]), lambda bi, qi: (bi, 0, 0)),
                  pl.BlockSpec((nc, HD), lambda bi, qi: (0, 0)),
                  pl.BlockSpec((nc, HD), lambda bi, qi: (0, 0))],
        out_specs=[pl.BlockSpec((1, tq, qw), lambda bi, qi: (bi, qi, 0)),
                   pl.BlockSpec((1, KV_B, tq, s_pad), lambda bi, qi: (bi, 0, qi, 0))],
        out_shape=[jax.ShapeDtypeStruct((b, t, qw), F32),
                   jax.ShapeDtypeStruct((b, KV_B, t, s_pad), dtype)],
        compiler_params=_params(("parallel", "parallel")),
        name="nsa_cmp",
    )(qb, kc, *ctab)


def _cmp_rows_body(_, *refs, per_step, per_page, rpt):
    o_ref = refs[per_step]
    stride = CMP_BLOCK * rpt
    for i in range(stride):
        piece = jnp.concatenate([pg[pl.ds(i, per_page, stride=stride), :] for pg in refs[:per_step]],
                                axis=0)
        o_ref[0, :, i * LANES:(i + 1) * LANES] = piece


def _cmp_rows(pool, pt, off, *, page, rpt, per_step=PAGES_PER_STEP):
    b, n_pages = pt.shape
    per_step = _tile(n_pages, per_step, 1)
    per_page = page // CMP_BLOCK
    width = CMP_BLOCK * rpt * LANES

    def spec(u):
        return pl.BlockSpec((page * rpt, LANES),
                            lambda bi, i, p: (off + p[bi, i * per_step + u], 0))

    return _call(functools.partial(_cmp_rows_body, per_step=per_step, per_page=per_page, rpt=rpt),
                 grid=(b, n_pages // per_step), in_specs=[spec(u) for u in range(per_step)],
                 out_specs=pl.BlockSpec((1, per_step * per_page, width), lambda bi, i, p: (bi, i, 0)),
                 out_shape=jax.ShapeDtypeStruct((b, n_pages * per_page, width), F32),
                 sem=("parallel", "arbitrary"), name="cmp_rows",
                 prefetch=pt)(*([pool] * per_step))


def _mix_ln_body(oa_ref, oc_ref, os_ref, ow_ref, misc_ref, w_ref, r_ref, g_ref, b_ref, o_ref):
    gate = misc_ref[...]
    gated = []
    for h in range(H_B):
        c0 = MISC_G + 3 * h
        sl = slice(h * HD, (h + 1) * HD)
        gated.append(gate[:, c0:c0 + 1] * oc_ref[:, sl] + gate[:, c0 + 1:c0 + 2] * os_ref[:, sl]
                     + gate[:, c0 + 2:c0 + 3] * ow_ref[:, sl])
    wa = H_A * HD
    y = _dot(oa_ref[...], w_ref[0:wa, :]) + _dot(jnp.concatenate(gated, axis=1), w_ref[wa:, :])
    o_ref[...] = _ln(ALPHA * r_ref[...] + y, g_ref[...], b_ref[...])


def _mix_ln(o_a, o_c, o_s, o_w, misc, w, res, g, b, *, layer, tm=512):
    m = o_a.shape[0]
    d = w.shape[-1]
    tm = _tile(m, tm, 8)
    wa, wb = H_A * HD, H_B * HD
    return pl.pallas_call(
        _mix_ln_body,
        grid=(m // tm,),
        in_specs=[pl.BlockSpec((tm, wa), lambda i: (i, 0))]
        + [pl.BlockSpec((tm, wb), lambda i: (i, 0))] * 3
        + [pl.BlockSpec((tm, LANES), lambda i: (i, 0)),
           _wspec(w, layer, (wa + wb, d), lambda i: (0, 0)),
           pl.BlockSpec((tm, d), lambda i: (i, 0)),
           pl.BlockSpec((1, d), lambda i: (0, 0)),
           pl.BlockSpec((1, d), lambda i: (0, 0))],
        out_specs=pl.BlockSpec((tm, d), lambda i: (i, 0)),
        out_shape=jax.ShapeDtypeStruct((m, d), F32),
        compiler_params=_params(("parallel",)),
        name="mix_ln",
    )(o_a, o_c, o_s, o_w, misc, w, res, g.reshape(1, d), b.reshape(1, d))


def _sb_init(q_ref, q_scr, c_scr, acc_scr, tq):
    rep = H_C // KV_C
    c_scr[...] = jnp.zeros(c_scr.shape, F32)
    acc_scr[...] = jnp.zeros(acc_scr.shape, F32)
    for g in range(KV_C):
        for r in range(rep):
            h = g * rep + r
            q_scr[g, r * tq:(r + 1) * tq, :] = (q_ref[0, :, h * HD:(h + 1) * HD] * QSCALE).astype(MXU)


def _sb_block(ks, vs, before, q_scr, c_scr, acc_scr):
    rep = H_C // KV_C
    rows = q_scr.shape[1]
    nchunk = ks[0].shape[0] // LANES
    older = None if before is None else jnp.concatenate([before] * rep, axis=0) > 0.5
    ys, pieces = [], []
    for g in range(KV_C):
        y = _dot_nt(q_scr[g], ks[g])
        drop = jnp.maximum(y, 0.0) + jnp.log2(1.0 + jnp.exp2(-jnp.abs(y)))
        if older is not None:
            drop = jnp.where(older, drop, 0.0)
        ys.append(y)
        pieces += [drop[:, c * LANES:(c + 1) * LANES] for c in range(nchunk)]
    stacked = jnp.concatenate(pieces, axis=0)
    hi = stacked.astype(BF16)
    lo = (stacked - hi.astype(F32)).astype(BF16)
    r_i = lax.broadcasted_iota(I32, (LANES, LANES), 0)
    c_i = lax.broadcasted_iota(I32, (LANES, LANES), 1)
    later = jnp.where(r_i >= c_i, -1.0, 0.0).astype(BF16)
    incl = jnp.dot(jnp.concatenate([hi, lo], axis=1), jnp.concatenate([later, later], axis=0),
                   preferred_element_type=F32)
    for g in range(KV_C):
        run = c_scr[g]
        cols = [None] * nchunk
        for c in reversed(range(nchunk)):
            part = incl[(g * nchunk + c) * rows:(g * nchunk + c + 1) * rows]
            cols[c] = part + run
            run = run + jnp.broadcast_to(part[:, :1], part.shape)
        a = jnp.exp2(ys[g] + jnp.concatenate(cols, axis=1))
        if older is not None:
            a = jnp.where(older, a, 0.0)
        acc_scr[g] = acc_scr[g] + _dot(a, vs[g])
        c_scr[g] = run


def _sb_final(o_ref, acc_scr, tq):
    rep = H_C // KV_C
    for g in range(KV_C):
        for r in range(rep):
            h = g * rep + r
            o_ref[0, :, h * HD:(h + 1) * HD] = acc_scr[g][r * tq:(r + 1) * tq, :]


def _sb_scratch(tq):
    rows = (H_C // KV_C) * tq
    return [pltpu.VMEM((KV_C, rows, HD), MXU),
            pltpu.VMEM((KV_C, rows, LANES), F32),
            pltpu.VMEM((KV_C, rows, HD), F32)]


def _split_kv(kv):
    ks = [kv[:, g * HD:(g + 1) * HD] for g in range(KV_C)]
    vs = [kv[:, (KV_C + g) * HD:(KV_C + g + 1) * HD] for g in range(KV_C)]
    return ks, vs


def _older_mask(tq, tk, qpos0, kpos0):
    qpos = qpos0 + lax.broadcasted_iota(I32, (tq, tk), 0)
    kpos = kpos0 + lax.broadcasted_iota(I32, (tq, tk), 1)
    return jnp.where(kpos < qpos, 1.0, 0.0)


def _key_norm_body(kv_ref, o_ref):
    @pl.when(pl.program_id(1) == 0)
    def _():
        o_ref[...] = jnp.zeros(o_ref.shape, F32)

    kv = kv_ref[0]
    for g in range(KV_C):
        k = kv[:, g * HD:(g + 1) * HD]
        top = jnp.max(jnp.sum(k * k, axis=1, keepdims=True), axis=0, keepdims=True)
        o_ref[0, g:g + 1, :] = jnp.maximum(o_ref[0, g:g + 1, :], jnp.broadcast_to(top, (1, LANES)))


def _key_norm(kv, tk):
    b, s_len, cw = kv.shape
    return pl.pallas_call(
        _key_norm_body,
        grid=(b, s_len // tk),
        in_specs=[pl.BlockSpec((1, tk, cw), lambda bi, j: (bi, j, 0))],
        out_specs=pl.BlockSpec((1, 8, LANES), lambda bi, j: (bi, 0, 0)),
        out_shape=jax.ShapeDtypeStruct((b, 8, LANES), F32),
        compiler_params=_params(("parallel", "arbitrary")),
        name="key_norm",
    )(kv)


SB_DEAD = -160.0
NORM_SLACK = 1.02


def _sb_body(q_ref, kv_ref, kn_ref, o_ref, q_scr, c_scr, acc_scr, qn_scr, done_ref, *,
             tq, tk, nj, pos0):
    rep = H_C // KV_C
    qb = pl.program_id(1)
    j = pl.program_id(2)
    kb = (pos0 + (qb + 1) * tq - 1) // tk - j
    q0 = pos0 + qb * tq

    @pl.when(j == 0)
    def _():
        _sb_init(q_ref, q_scr, c_scr, acc_scr, tq)
        done_ref[0] = 0
        for g in range(KV_C):
            for r in range(rep):
                h = g * rep + r
                qs = q_ref[0, :, h * HD:(h + 1) * HD] * QSCALE
                qn_scr[g, r * tq:(r + 1) * tq, :] = jnp.broadcast_to(
                    jnp.sqrt(jnp.sum(qs * qs, axis=1, keepdims=True)), (tq, LANES))

    live = (kb >= 0) & (done_ref[0] == 0)

    @pl.when(live & ((kb + 1) * tk > q0))
    def _():
        ks, vs = _split_kv(kv_ref[0])
        _sb_block(ks, vs, _older_mask(tq, tk, q0, kb * tk), q_scr, c_scr, acc_scr)

    @pl.when(live & ((kb + 1) * tk <= q0))
    def _():
        ks, vs = _split_kv(kv_ref[0])
        _sb_block(ks, vs, None, q_scr, c_scr, acc_scr)

    @pl.when(live)
    def _():
        worst = None
        for g in range(KV_C):
            reach = qn_scr[g] * (jnp.sqrt(kn_ref[0, g:g + 1, :]) * NORM_SLACK) + c_scr[g]
            top = jnp.max(reach)
            worst = top if worst is None else jnp.maximum(worst, top)
        done_ref[0] = jnp.where(worst < SB_DEAD, 1, 0).astype(I32)

    @pl.when(j == nj - 1)
    def _():
        _sb_final(o_ref, acc_scr, tq)


def _sb_attn(q, kv, *, tq, tk, pos0):
    b, t, qw = q.shape
    cw = kv.shape[2]
    nj = kv.shape[1] // tk
    rows = (H_C // KV_C) * tq
    body = functools.partial(_sb_body, tq=tq, tk=tk, nj=nj, pos0=pos0)
    return _call(
        body, grid=(b, t // tq, nj),
        in_specs=[pl.BlockSpec((1, tq, qw), lambda bi, qb, j: (bi, qb, 0)),
                  pl.BlockSpec((1, tk, cw), lambda bi, qb, j: (
                      bi, jnp.maximum((pos0 + (qb + 1) * tq - 1) // tk - j, 0), 0)),
                  pl.BlockSpec((1, 8, LANES), lambda bi, qb, j: (bi, 0, 0))],
        out_specs=pl.BlockSpec((1, tq, qw), lambda bi, qb, j: (bi, qb, 0)),
        out_shape=jax.ShapeDtypeStruct((b, t, qw), F32),
        scratch=_sb_scratch(tq) + [pltpu.VMEM((KV_C, rows, LANES), F32),
                                   pltpu.SMEM((1,), I32)],
        sem=("parallel", "parallel", "arbitrary"), name="sb_attn")(q, kv, _key_norm(kv, tk))


def _sb_paged_body(_, q_ref, *refs, tq, page, n_pg, n_main, pos0):
    pages = refs[:n_pg]
    tail_ref, o_ref, q_scr, c_scr, acc_scr = refs[n_pg:]
    j = pl.program_id(2)
    rpt = 2 * KV_C

    @pl.when(j == 0)
    def _():
        _sb_init(q_ref, q_scr, c_scr, acc_scr, tq)
        ks, vs = _split_kv(tail_ref[0])
        _sb_block(ks, vs, _older_mask(tq, page, pos0, n_main * n_pg * page), q_scr, c_scr, acc_scr)

    @pl.when(j > 0)
    def _():
        ks = [jnp.concatenate([pg[pl.ds(g, page, stride=rpt), :] for pg in pages], axis=0)
              for g in range(KV_C)]
        vs = [jnp.concatenate([pg[pl.ds(KV_C + g, page, stride=rpt), :] for pg in pages], axis=0)
              for g in range(KV_C)]
        _sb_block(ks, vs, None, q_scr, c_scr, acc_scr)

    @pl.when(j == n_main)
    def _():
        _sb_final(o_ref, acc_scr, tq)


def _sb_attn_paged(q, pool, pt, off, tail, *, page, pos0, pages_per_step=8):
    b, t, qw = q.shape
    n_pages = pt.shape[1]
    n_pg = _tile(n_pages, pages_per_step, 1)
    n_main = n_pages // n_pg
    rpt = 2 * KV_C
    ix = lambda f: (lambda bi, qb, j, p: f(bi, qb, j))
    page_specs = [pl.BlockSpec(
        (page * rpt, LANES),
        functools.partial(lambda u, bi, qb, j, p:
                          (off + p[bi, jnp.clip(n_main - j, 0, n_main - 1) * n_pg + u], 0), u))
        for u in range(n_pg)]
    in_specs = ([pl.BlockSpec((1, t, qw), ix(lambda bi, qb, j: (bi, 0, 0)))] + page_specs
                + [pl.BlockSpec((1, page, tail.shape[2]), ix(lambda bi, qb, j: (bi, 0, 0)))])
    body = functools.partial(_sb_paged_body, tq=t, page=page, n_pg=n_pg, n_main=n_main, pos0=pos0)
    return _call(body, grid=(b, 1, n_main + 1), in_specs=in_specs,
                 out_specs=pl.BlockSpec((1, t, qw), ix(lambda bi, qb, j: (bi, 0, 0))),
                 out_shape=jax.ShapeDtypeStruct((b, t, qw), F32), scratch=_sb_scratch(t),
                 sem=("parallel", "parallel", "arbitrary"), name="sb_attn_paged",
                 prefetch=pt)(q, *([pool] * n_pg), tail)


def _mem_body(x_ref, kv_ref, wq_ref, wo_ref, g_ref, b_ref, o_ref):
    x = x_ref[0]
    kv = kv_ref[0]
    q = _dot(x, wq_ref[...])
    outs = []
    for h in range(MEM_HEADS):
        s = _dot_nt(q[:, h * HD:(h + 1) * HD], kv[:, h * HD:(h + 1) * HD]) * SCALE
        p = jnp.exp(s - jnp.max(s, axis=1, keepdims=True))
        p = p / jnp.sum(p, axis=1, keepdims=True)
        outs.append(_dot(p, kv[:, (MEM_HEADS + h) * HD:(MEM_HEADS + h + 1) * HD]))
    y = _dot(jnp.concatenate(outs, axis=1), wo_ref[...])
    o_ref[0] = _ln(ALPHA * x + y, g_ref[...], b_ref[...])


def _mem_layer(x, mem_kv, wq, wo, g, b, *, layer=None, tq):
    bsz, t, d = x.shape
    nm, cw = mem_kv.shape[1:]
    return pl.pallas_call(
        _mem_body,
        grid=(bsz, t // tq),
        in_specs=[pl.BlockSpec((1, tq, d), lambda bi, qi: (bi, qi, 0)),
                  pl.BlockSpec((1, nm, cw), lambda bi, qi: (bi, 0, 0)),
                  _wspec(wq, layer, wq.shape[-2:], lambda bi, qi: (0, 0)),
                  _wspec(wo, layer, wo.shape[-2:], lambda bi, qi: (0, 0)),
                  pl.BlockSpec((1, d), lambda bi, qi: (0, 0)),
                  pl.BlockSpec((1, d), lambda bi, qi: (0, 0))],
        out_specs=pl.BlockSpec((1, tq, d), lambda bi, qi: (bi, qi, 0)),
        out_shape=jax.ShapeDtypeStruct((bsz, t, d), F32),
        compiler_params=_params(("parallel", "parallel")),
        name="mem_layer",
    )(x, mem_kv, wq, wo, g.reshape(1, d), b.reshape(1, d))


def _ffn_body(x_ref, wg_ref, wu_ref, wd_ref, g_ref, b_ref, o_ref, xb_ref, acc_ref, *, nf):
    f = pl.program_id(1)

    @pl.when(f == 0)
    def _():
        xb_ref[...] = x_ref[...].astype(MXU)
        acc_ref[...] = jnp.zeros(acc_ref.shape, F32)

    xb = xb_ref[...]
    gate = jnp.dot(xb, wg_ref[...], preferred_element_type=F32)
    up = jnp.dot(xb, wu_ref[...], preferred_element_type=F32)
    hidden = gate / (1.0 + jnp.exp(-gate)) * up
    acc_ref[...] += _dot(hidden, wd_ref[...])

    @pl.when(f == nf - 1)
    def _():
        o_ref[...] = _ln(ALPHA * x_ref[...] + acc_ref[...], g_ref[...], b_ref[...])


def _ffn_layer(x, w_gu, w_d, g, b, *, layer=None, tm=512, tf=512):
    m, d = x.shape
    dff = w_d.shape[-2]
    tm = _tile(m, tm, 8)
    tf = _tile(dff, tf, LANES)
    nf = dff // tf
    return pl.pallas_call(
        functools.partial(_ffn_body, nf=nf),
        grid=(m // tm, nf),
        in_specs=[pl.BlockSpec((tm, d), lambda i, f: (i, 0)),
                  _wspec(w_gu, layer, (d, tf), lambda i, f: (0, f)),
                  _wspec(w_gu, layer, (d, tf), lambda i, f: (0, nf + f)),
                  _wspec(w_d, layer, (tf, d), lambda i, f: (f, 0)),
                  pl.BlockSpec((1, d), lambda i, f: (0, 0)),
                  pl.BlockSpec((1, d), lambda i, f: (0, 0))],
        out_specs=pl.BlockSpec((tm, d), lambda i, f: (i, 0)),
        out_shape=jax.ShapeDtypeStruct((m, d), F32),
        scratch_shapes=[pltpu.VMEM((tm, d), MXU), pltpu.VMEM((tm, d), F32)],
        compiler_params=_params(("parallel", "arbitrary")),
        name="ffn_layer",
    )(x, w_gu, w_gu, w_d, g.reshape(1, d), b.reshape(1, d))


def _even_odd_rows(a):
    b, n2, c = a.shape
    return a.reshape(b, n2 // 2, 2, c).transpose(0, 2, 1, 3).reshape(b, n2, c)


def _pad_rows(a, rows):
    return jnp.pad(a, ((0, 0), (0, rows - a.shape[1]), (0, 0)))


def _ab_mixer(x, w, layer, past, *, pos0):
    b, t, d = x.shape
    m = b * t
    pos = pos0 + jnp.arange(t, dtype=I32)
    tabs = _rope_tables(pos, HD) + _rope_tables(pos, D_IDX)
    h = _mm(x.reshape(m, d), w["in_ab"][layer], tn=640)
    q_a, kv_a, q_i, q_b, cmp_kv, sel_kv, win_kv, misc = _post_ab(h, tabs, t)
    r3 = lambda a: a.reshape(b, t, a.shape[1])
    q_a, kv_a, q_i, q_b, cmp_kv, sel_kv, win_kv, misc = map(
        r3, (q_a, kv_a, q_i, q_b, cmp_kv, sel_kv, win_kv, misc))
    cw = 2 * KV_B * HD
    rep_a, rep_b = H_A // KV_A, H_B // KV_B
    if past is None:
        s_len = t
        tq = _tile(t, 256, 8)
        tk = _tile(t, 1024, LANES)
        topk = min(TOPK_MAX, s_len // 4)
        bias_a = _dsa_index(q_i, misc, misc, topk=topk, pos0=0, tq=tq, tk=_tile(t, 512, LANES))
        o_a = _attn(q_a, kv_a, groups=KV_A, rep=rep_a, tq=tq, tk=tk, pos0=0, bias=bias_a)
        nc = s_len // CMP_BLOCK
        kc = _mm(cmp_kv.reshape(m // CMP_BLOCK, CMP_BLOCK * cw), w["phi"][layer], tn=cw)
        kc = _even_odd_rows(kc.reshape(b, nc, cw))
        s_pad = s_len
        o_w = _attn(q_b, win_kv, groups=KV_B, rep=rep_b, tq=tq, tk=tq, pos0=0)
        sel_attn = lambda bias: _attn(q_b, sel_kv, groups=KV_B, rep=rep_b, tq=tq, tk=tk,
                                      pos0=0, bias=bias)
        win_state = win_kv[:, t - min(WINDOW, t):]
    else:
        pt, caches, win_prev = past
        n_pages = pt.shape[1]
        page = caches["page"]
        n_pool = caches["n_pool"]
        off = layer * n_pool
        s_len = n_pages * page + t
        s_pad = (n_pages + 1) * page
        tq = t
        topk = min(TOPK_MAX, s_len // 4)
        bias_a = _dsa_index(q_i, misc, caches["a_kidx"], topk=topk, pos0=pos0, tq=tq, tk=page,
                            page=(pt, off, _pad_rows(misc[:, :, :D_IDX], page)))
        o_a = _attn_paged(q_a, caches["a_kv"], pt, off, _pad_rows(kv_a, page), bias_a,
                          groups=KV_A, rep=rep_a, page=page)
        nc = s_len // CMP_BLOCK
        per_page = page // CMP_BLOCK
        assert nc == n_pages * per_page, "new rows must not complete a compressed block"
        raw = _cmp_rows(caches["cmp"], pt, off, page=page, rpt=cw // LANES)
        kc = _mm(raw.reshape(b * nc, raw.shape[2]), w["phi"][layer], tn=cw)
        kc = _even_odd_rows(kc.reshape(b, nc, cw))
        win_all = jnp.concatenate([win_prev, win_kv], axis=1)
        k_start = n_pages * page - win_prev.shape[1]
        o_w = _attn(q_b, _pad_rows(win_all, -(-win_all.shape[1] // page) * page),
                    groups=KV_B, rep=rep_b, tq=tq, tk=page, pos0=pos0, k_start=k_start)
        sel_attn = lambda bias: _attn_paged(q_b, caches["sel"], pt, off, _pad_rows(sel_kv, page),
                                            bias, groups=KV_B, rep=rep_b, page=page)
        win_state = win_all[:, win_all.shape[1] - win_prev.shape[1]:]
    assert nc % 2 == 0
    c_pos = (jnp.arange(nc, dtype=I32) + 1) * CMP_BLOCK - 1
    ctab = tuple(_even_odd_rows(tb[None])[0] for tb in _rope_tables(c_pos, HD))
    o_c, bias_s = _nsa_cmp(q_b, kc, ctab, s_len=s_len, s_pad=s_pad, pos0=pos0, tq=tq)
    o_s = sel_attn(bias_s)
    f2 = lambda a: a.reshape(m, a.shape[2])
    pieces = (f2(o_a), f2(o_c), f2(o_s), f2(o_w), f2(misc))
    states = (kv_a, misc[:, :, :D_IDX], cmp_kv, sel_kv, win_state)
    return pieces, states


def _sb_mixer(x, w, layer, past, *, pos0):
    b, t, d = x.shape
    m = b * t
    x2 = x.reshape(m, d)
    q = _mm(x2, w["in_c"], layer=layer, cols=(0, H_C * HD)).reshape(b, t, H_C * HD)
    kv = _mm(x2, w["in_c"], layer=layer,
             cols=(H_C * HD, 2 * KV_C * HD)).reshape(b, t, 2 * KV_C * HD)
    if past is None:
        tq = _tile(t, 256, 8)
        o = _sb_attn(q, kv, tq=tq, tk=tq, pos0=0)
    else:
        pt, caches = past
        page = caches["page"]
        o = _sb_attn_paged(q, caches["c_kv"], pt, layer * caches["n_pool"], _pad_rows(kv, page),
                           page=page, pos0=pos0)
    return o.reshape(m, H_C * HD), kv


def _run_group(x, w, mem_kvs, pasts, ln_g, ln_b, *, pos0):
    b, t, d = x.shape
    m = b * t
    even, odd = [], []
    for i in range(DEPTH):
        if i % 2 == 0:
            pieces, st = _ab_mixer(x, w, i // 2, pasts[i], pos0=pos0)
            even.append(st)
            x2 = _mix_ln(*pieces, w["out"], x.reshape(m, d), ln_g[i, 0], ln_b[i, 0], layer=i)
        else:
            mix, st = _sb_mixer(x, w, i // 2, pasts[i], pos0=pos0)
            odd.append(st)
            x2 = _mm_ln(mix, w["out"], x.reshape(m, d), ln_g[i, 0], ln_b[i, 0], layer=i)
        x3 = _mem_layer(x2.reshape(b, t, d), mem_kvs[i], w["mem_q"], w["mem_o"],
                        ln_g[i, 1], ln_b[i, 1], layer=i, tq=_tile(t, 256, 8))
        x = _ffn_layer(x3.reshape(m, d), w["gate_up"], w["down"],
                       ln_g[i, 2], ln_b[i, 2], layer=i).reshape(b, t, d)
    return x, even, odd


def _reorder_in_ab(w):
    sizes = (H_A * HD, 2 * KV_A * HD, H_IDX * D_IDX, D_IDX, H_IDX,
             H_B * HD, 2 * KV_B * HD, 2 * KV_B * HD, 2 * KV_B * HD, 3 * H_B)
    offs = np.concatenate([[0], np.cumsum(sizes)])
    piece = lambda i: w[..., offs[i]:offs[i + 1]]
    order = [0, 1, 2, 5, 6, 7, 8, 3, 4, 9]
    cols = [piece(i) for i in order]
    pad = C_END - int(offs[-1])
    cols.append(jnp.zeros(w.shape[:-1] + (pad,), w.dtype))
    return jnp.concatenate(cols, axis=-1)


def _phi_matrix(w_phi):
    phi = w_phi.reshape(2, CMP_BLOCK, HD, HD)
    nb = 2 * KV_B
    per_piece = jnp.stack([phi[c // KV_B] for c in range(nb)], axis=1)
    same = jnp.eye(nb, dtype=bool)[None, :, None, :, None]
    out = jnp.where(same, per_piece[:, :, :, None, :], jnp.zeros((), w_phi.dtype))
    return out.reshape(CMP_BLOCK * nb * HD, nb * HD)


def kernel(x_prompt, x_sample, cache_a_kv, cache_a_kidx, cache_b_cmp_kv, cache_b_sel_kv,
           state_b_win_kv, cache_c_kv, cache_mem_kv, page_table, mem_prompt, w_in_ab, w_cmp_phi,
           w_in_c, w_out, w_mem_q, w_mem_kv, w_mem_o, w_gate_up, w_down, ln_g, ln_b):
    b_p, t_p, d = x_prompt.shape
    b_s, t_s, _ = x_sample.shape
    n_even, n_pool, page = cache_a_kv.shape[:3]
    n_odd = cache_c_kv.shape[0]
    n_mem = mem_prompt.shape[1]
    past_len = page_table.shape[1] * page

    w = {
        "in_ab": [_reorder_in_ab(wl).astype(MXU) for wl in w_in_ab],
        "phi": [_phi_matrix(wl.astype(MXU)) for wl in w_cmp_phi],
        "in_c": w_in_c.astype(MXU),
        "out": w_out.astype(MXU),
        "mem_q": w_mem_q.astype(MXU),
        "mem_o": w_mem_o.astype(MXU),
        "gate_up": w_gate_up.astype(MXU),
        "down": w_down.astype(MXU),
    }
    w_mkv = w_mem_kv.astype(MXU)

    mem_flat = mem_prompt.reshape(b_p * n_mem, d)
    mem_p = [_mm(mem_flat, w_mkv, layer=i).reshape(b_p, n_mem, 2 * MEM_HEADS * HD)
             for i in range(DEPTH)]
    y_p, ev_p, od_p = _run_group(x_prompt, w, mem_p, [None] * DEPTH, ln_g, ln_b, pos0=0)

    cw_b = 2 * KV_B * HD
    caches = {
        "page": page, "n_pool": n_pool,
        "a_kv": cache_a_kv.reshape(-1, LANES),
        "a_kidx": cache_a_kidx.reshape(n_even * n_pool, page, D_IDX),
        "cmp": cache_b_cmp_kv.reshape(-1, LANES),
        "sel": cache_b_sel_kv.reshape(-1, LANES),
    }
    caches_c = {"page": page, "n_pool": n_pool, "c_kv": cache_c_kv.reshape(-1, LANES)}
    win_prev = state_b_win_kv.reshape(n_even, b_s, state_b_win_kv.shape[2], cw_b)
    pasts = []
    for i in range(DEPTH):
        if i % 2 == 0:
            pasts.append((page_table, caches, win_prev[i // 2]))
        else:
            pasts.append((page_table, caches_c))
    mem_s = [cache_mem_kv[i].reshape(b_s, n_mem, 2 * MEM_HEADS * HD) for i in range(DEPTH)]
    y_s, ev_s, od_s = _run_group(x_sample, w, mem_s, pasts, ln_g, ln_b, pos0=past_len)

    def pack(states, bsz):
        def kv5(a, g):
            return a.reshape(bsz, a.shape[1], 2, g, HD)
        return (jnp.stack([kv5(s[0], KV_A) for s in states]),
                jnp.stack([s[1] for s in states]),
                jnp.stack([kv5(s[2], KV_B) for s in states]),
                jnp.stack([kv5(s[3], KV_B) for s in states]),
                jnp.stack([kv5(s[4], KV_B) for s in states]))

    a_p, ki_p, cmp_p, sel_p, win_p = pack(ev_p, b_p)
    a_s, ki_s, cmp_s, sel_s, win_s = pack(ev_s, b_s)
    c_p = jnp.stack([s.reshape(b_p, t_p, 2, KV_C, HD) for s in od_p])
    c_s = jnp.stack([s.reshape(b_s, t_s, 2, KV_C, HD) for s in od_s])
    mem_out = jnp.stack([mk.reshape(b_p, n_mem, 2, MEM_HEADS, HD) for mk in mem_p])
    return (y_p, y_s, a_p, ki_p, cmp_p, sel_p, win_p, c_p, mem_out,
            a_s, ki_s, cmp_s, sel_s, win_s, c_s)
```
